```python
import math
import jax
import jax.numpy as jnp
from jax import lax
import numpy as np

D_MODEL = 1024
BATCH = 8
SEQ = 4096
DEPTH = 1

S5_WIDTH = D_MODEL // 2
S5_GROUP = 16
S5_GROUPS = S5_WIDTH // S5_GROUP
S5_STATE = 64
CONV_WIDTH = D_MODEL // 2
CONV_KERNEL = 31
N_EXPERTS = 64
TOP_K = 8
EXPERT_HIDDEN = 256
SHARED_HIDDEN = 256
ROUTED_SCALE = 2.5
N_MOD = 6
IN_COLS = S5_WIDTH + 2 * CONV_WIDTH + 2 * D_MODEL
EPS = 1e-6
DT_MIN = 1e-3
DT_MAX = 1e-1

kernel_name = 'hybrid_s5_conformer_moe_block'


def rmsnorm(x, g):
    xf = x.astype(jnp.float32)
    y = xf * lax.rsqrt(jnp.mean(xf * xf, axis=-1, keepdims=True) + EPS)
    return (y * g.astype(jnp.float32)).astype(x.dtype)


def layernorm(x, g, b):
    xf = x.astype(jnp.float32)
    mu = jnp.mean(xf, axis=-1, keepdims=True)
    var = jnp.mean(jnp.square(xf - mu), axis=-1, keepdims=True)
    y = (xf - mu) * lax.rsqrt(var + EPS)
    return (y * g.astype(jnp.float32) + b.astype(jnp.float32)).astype(x.dtype)


def modulate(h, shift, scale):
    return h * (1.0 + scale[:, None, :]) + shift[:, None, :]


def complex_linear_combine(e1, e2):
    a1r, a1i, b1r, b1i = e1
    a2r, a2i, b2r, b2i = e2
    ar = a1r * a2r - a1i * a2i
    ai = a1r * a2i + a1i * a2r
    br = a2r * b1r - a2i * b1i + b2r
    bi = a2r * b1i + a2i * b1r + b2i
    return (ar, ai, br, bi)


def s5_branch(u, lam_re, lam_im, log_dt, b_re, b_im, c_re, c_im, d_skip):
    bsz, seq, _ = u.shape
    uf = u.astype(jnp.float32).reshape(bsz, seq, S5_GROUPS, S5_GROUP)
    lr = lam_re.astype(jnp.float32)
    li = lam_im.astype(jnp.float32)
    dt = jnp.exp(log_dt.astype(jnp.float32))[:, None]
    mag = jnp.exp(lr * dt)
    abar_r = mag * jnp.cos(li * dt)
    abar_i = mag * jnp.sin(li * dt)
    den = lr * lr + li * li
    nr = abar_r - 1.0
    ni = abar_i
    k_r = (nr * lr + ni * li) / den
    k_i = (ni * lr - nr * li) / den
    br = b_re.astype(jnp.float32)
    bi = b_im.astype(jnp.float32)
    bbar_r = k_r[..., None] * br - k_i[..., None] * bi
    bbar_i = k_r[..., None] * bi + k_i[..., None] * br
    bu_r = jnp.einsum('bsgh,gph->bsgp', uf, bbar_r)
    bu_i = jnp.einsum('bsgh,gph->bsgp', uf, bbar_i)
    a_r = jnp.broadcast_to(abar_r[None, None], (1, seq, S5_GROUPS, S5_STATE))
    a_i = jnp.broadcast_to(abar_i[None, None], (1, seq, S5_GROUPS, S5_STATE))
    _, _, s_r, s_i = lax.associative_scan(complex_linear_combine, (a_r, a_i, bu_r, bu_i), axis=1)
    y = (jnp.einsum('bsgp,ghp->bsgh', s_r, c_re.astype(jnp.float32))
         - jnp.einsum('bsgp,ghp->bsgh', s_i, c_im.astype(jnp.float32))
         + d_skip.astype(jnp.float32)[None, None] * uf)
    return y.reshape(bsz, seq, S5_WIDTH).astype(u.dtype)


def conv_branch(v, g, w_dw, b_dw, ln_g, ln_b, w_pw):
    z = v * jax.nn.sigmoid(g)
    z = lax.conv_general_dilated(
        z, w_dw[:, None, :], window_strides=(1,), padding=[(CONV_KERNEL - 1, 0)],
        dimension_numbers=('NWC', 'WIO', 'NWC'), feature_group_count=CONV_WIDTH) + b_dw
    z = jax.nn.silu(layernorm(z, ln_g, ln_b))
    return z @ w_pw


def moe(h, router_w, router_bias, w_gate, w_up, w_down, sw_gate, sw_up, sw_down):
    bsz, seq, d = h.shape
    t = h.reshape(-1, d)
    scores = jax.nn.sigmoid((t @ router_w).astype(jnp.float32))
    _, idx = lax.top_k(scores + router_bias.astype(jnp.float32), TOP_K)
    w = jnp.take_along_axis(scores, idx, axis=-1)
    w = w / jnp.sum(w, axis=-1, keepdims=True) * ROUTED_SCALE
    flat_e = idx.reshape(-1)
    order = jnp.argsort(flat_e)
    tok = order // TOP_K
    xs = t[tok]
    group_sizes = jnp.bincount(flat_e, length=N_EXPERTS).astype(jnp.int32)
    hg = lax.ragged_dot(xs, w_gate, group_sizes)
    hu = lax.ragged_dot(xs, w_up, group_sizes)
    out = lax.ragged_dot(jax.nn.silu(hg) * hu, w_down, group_sizes)
    wsel = w.reshape(-1)[order].astype(out.dtype)
    routed = jnp.zeros_like(t).at[tok].add(out * wsel[:, None])
    shared = (jax.nn.silu(t @ sw_gate) * (t @ sw_up)) @ sw_down
    return (routed + shared).reshape(bsz, seq, d)


def setup_inputs(seed: int = 0) -> dict:
    key = jax.random.key(seed)
    ks = jax.random.split(key, 40)
    f32 = jnp.float32
    L, D = DEPTH, D_MODEL
    G, P, H = S5_GROUPS, S5_STATE, S5_GROUP

    def nrm(k, shape, scale):
        return jax.random.normal(k, shape, f32) * scale

    lam_im_base = jnp.pi * jnp.arange(P, dtype=f32)
    return {
        'x': nrm(ks[0], (BATCH, SEQ, D), 1.0),
        'c': nrm(ks[1], (BATCH, D), 1.0),
        'w_ada': nrm(ks[2], (L, D, N_MOD * D), 0.5 * D ** -0.5),
        'b_ada': nrm(ks[3], (L, N_MOD * D), 0.01),
        'norm1_g': 1.0 + nrm(ks[4], (L, D), 0.01),
        'w_in': nrm(ks[5], (L, D, IN_COLS), D ** -0.5),
        's5_lambda_re': -0.5 + nrm(ks[6], (L, G, P), 0.01),
        's5_lambda_im': lam_im_base[None, None, :] + nrm(ks[7], (L, G, P), 0.01),
        's5_log_dt': jax.random.uniform(ks[8], (L, G), f32, math.log(DT_MIN), math.log(DT_MAX)),
        's5_b_re': nrm(ks[9], (L, G, P, H), (2 * H) ** -0.5),
        's5_b_im': nrm(ks[10], (L, G, P, H), (2 * H) ** -0.5),
        's5_c_re': nrm(ks[11], (L, G, H, P), (2 * P) ** -0.5),
        's5_c_im': nrm(ks[12], (L, G, H, P), (2 * P) ** -0.5),
        's5_d': nrm(ks[13], (L, G, H), 1.0),
        's5_glu_wv': nrm(ks[14], (L, S5_WIDTH, D), S5_WIDTH ** -0.5),
        's5_glu_wg': nrm(ks[15], (L, S5_WIDTH, D), S5_WIDTH ** -0.5),
        'conv_dw_w': nrm(ks[16], (L, CONV_KERNEL, CONV_WIDTH), CONV_KERNEL ** -0.5),
        'conv_dw_b': nrm(ks[17], (L, CONV_WIDTH), 0.01),
        'conv_ln_g': 1.0 + nrm(ks[18], (L, CONV_WIDTH), 0.01),
        'conv_ln_b': nrm(ks[19], (L, CONV_WIDTH), 0.01),
        'conv_pw_w': nrm(ks[20], (L, CONV_WIDTH, D), CONV_WIDTH ** -0.5),
        'w_out': nrm(ks[21], (L, D, D), D ** -0.5),
        'norm2_g': 1.0 + nrm(ks[22], (L, D), 0.01),
        'router_w': nrm(ks[23], (L, D, N_EXPERTS), D ** -0.5),
        'router_bias': nrm(ks[24], (L, N_EXPERTS), 0.01),
        'exp_w_gate': nrm(ks[25], (L, N_EXPERTS, D, EXPERT_HIDDEN), D ** -0.5),
        'exp_w_up': nrm(ks[26], (L, N_EXPERTS, D, EXPERT_HIDDEN), D ** -0.5),
        'exp_w_down': nrm(ks[27], (L, N_EXPERTS, EXPERT_HIDDEN, D), EXPERT_HIDDEN ** -0.5),
        'shared_w_gate': nrm(ks[28], (L, D, SHARED_HIDDEN), D ** -0.5),
        'shared_w_up': nrm(ks[29], (L, D, SHARED_HIDDEN), D ** -0.5),
        'shared_w_down': nrm(ks[30], (L, SHARED_HIDDEN, D), SHARED_HIDDEN ** -0.5),
        'final_norm_g': 1.0 + nrm(ks[31], (D,), 0.01),
    }


def reference(x, c, w_ada, b_ada, norm1_g, w_in, s5_lambda_re, s5_lambda_im, s5_log_dt,
              s5_b_re, s5_b_im, s5_c_re, s5_c_im, s5_d, s5_glu_wv, s5_glu_wg,
              conv_dw_w, conv_dw_b, conv_ln_g, conv_ln_b, conv_pw_w, w_out, norm2_g,
              router_w, router_bias, exp_w_gate, exp_w_up, exp_w_down,
              shared_w_gate, shared_w_up, shared_w_down, final_norm_g):
    o1 = S5_WIDTH
    o2 = o1 + CONV_WIDTH
    o3 = o2 + CONV_WIDTH
    o4 = o3 + D_MODEL
    c_act = jax.nn.silu(c)
    for l in range(DEPTH):
        mod = c_act @ w_ada[l] + b_ada[l]
        sh1, sc1, gt1, sh2, sc2, gt2 = jnp.split(mod, N_MOD, axis=-1)

        h = modulate(rmsnorm(x, norm1_g[l]), sh1, sc1)
        proj = h @ w_in[l]
        u_s5 = proj[..., :o1]
        conv_v = proj[..., o1:o2]
        conv_g = proj[..., o2:o3]
        gate_a = jax.nn.sigmoid(proj[..., o3:o4])
        gate_b = jax.nn.sigmoid(proj[..., o4:])
        y = s5_branch(u_s5, s5_lambda_re[l], s5_lambda_im[l], s5_log_dt[l],
                      s5_b_re[l], s5_b_im[l], s5_c_re[l], s5_c_im[l], s5_d[l])
        z = jax.nn.gelu(y)
        y_a = (z @ s5_glu_wv[l]) * jax.nn.sigmoid(z @ s5_glu_wg[l])
        y_b = conv_branch(conv_v, conv_g, conv_dw_w[l], conv_dw_b[l],
                          conv_ln_g[l], conv_ln_b[l], conv_pw_w[l])
        mixed = (gate_a * y_a + gate_b * y_b) @ w_out[l]
        x = x + gt1[:, None, :] * mixed

        h = modulate(rmsnorm(x, norm2_g[l]), sh2, sc2)
        ff = moe(h, router_w[l], router_bias[l], exp_w_gate[l], exp_w_up[l], exp_w_down[l],
                 shared_w_gate[l], shared_w_up[l], shared_w_down[l])
        x = x + gt2[:, None, :] * ff
    return rmsnorm(x, final_norm_g)
```

```python
import functools

import jax
import jax.numpy as jnp
from jax import lax
from jax.experimental import pallas as pl
from jax.experimental.pallas import tpu as pltpu

D_MODEL = 1024
BATCH = 8
SEQ = 4096
N_TOK = BATCH * SEQ
S5_WIDTH = 512
S5_GROUP = 16
S5_GROUPS = 32
S5_STATE = 64
N_STATE = S5_GROUPS * S5_STATE
CONV_WIDTH = 512
CONV_KERNEL = 31
N_EXPERTS = 64
TOP_K = 8
EXPERT_HIDDEN = 256
SHARED_HIDDEN = 256
ROUTED_SCALE = 2.5
N_MOD = 6
EPS = 1e-6

LANES = 128
SUBLANES = 8
VMEM_LIMIT = 56 * 1024 * 1024

PROJ_ROWS = 512
MIX_STEPS = 64
MIX_ROWS = MIX_STEPS * BATCH
CONV_HALO = (CONV_KERNEL - 1) * BATCH
CONV_CHUNK = 64
SCAN_LANES = 512
MOE_ROWS = 1024

F32 = jnp.float32
BF16 = jnp.bfloat16


def _bdot(a, b):
    return jnp.dot(a.astype(BF16), b.astype(BF16), preferred_element_type=F32)


def _sigmoid(x):
    return jax.nn.sigmoid(x)


def _const_spec(shape):
    zeros = (0,) * len(shape)
    return pl.BlockSpec(shape, lambda *_: zeros, pipeline_mode=pl.Buffered(1))


def _ada_kernel(c_ref, w_ref, b_ref, o_ref):
    c = c_ref[...]
    c_act = c * _sigmoid(c)
    o_ref[...] = jnp.dot(c_act, w_ref[...], preferred_element_type=F32,
                         precision=lax.Precision.HIGHEST) + b_ref[...]


def _ada(c, w_ada, b_ada):
    n_out = N_MOD * D_MODEL
    blk = 1536
    return pl.pallas_call(
        _ada_kernel,
        out_shape=jax.ShapeDtypeStruct((BATCH, n_out), F32),
        grid=(n_out // blk,),
        in_specs=[pl.BlockSpec((BATCH, D_MODEL), lambda j: (0, 0)),
                  pl.BlockSpec((D_MODEL, blk), lambda j: (0, j)),
                  pl.BlockSpec((1, blk), lambda j: (0, j))],
        out_specs=pl.BlockSpec((BATCH, blk), lambda j: (0, j)),
        compiler_params=pltpu.CompilerParams(vmem_limit_bytes=VMEM_LIMIT),
        name="ada",
    )(c, w_ada, b_ada.reshape(1, n_out))


def _s5prep_kernel(lr_ref, li_ref, ldt_ref, br_ref, bi_ref, ar_ref, ai_ref, bbr_ref, bbi_ref):
    lr = lr_ref[...]
    li = li_ref[...]
    dt = jnp.exp(ldt_ref[...])
    mag = jnp.exp(lr * dt)
    abar_r = mag * jnp.cos(li * dt)
    abar_i = mag * jnp.sin(li * dt)
    den = lr * lr + li * li
    nr = abar_r - 1.0
    ni = abar_i
    k_r = (nr * lr + ni * li) / den
    k_i = (ni * lr - nr * li) / den
    br = br_ref[...]
    bi = bi_ref[...]
    ar_ref[...] = abar_r
    ai_ref[...] = abar_i
    bbr_ref[...] = k_r * br - k_i * bi
    bbi_ref[...] = k_r * bi + k_i * br


def _s5prep(lam_re, lam_im, log_dt, b_re, b_im):
    rows = S5_GROUPS * S5_GROUP
    rep = lambda a: jnp.broadcast_to(a[:, None, :], (S5_GROUPS, S5_GROUP, S5_STATE)).reshape(rows, S5_STATE)
    lr = rep(lam_re)
    li = rep(lam_im)
    ldt = rep(jnp.broadcast_to(log_dt[:, None], (S5_GROUPS, S5_STATE)))
    br = jnp.transpose(b_re, (0, 2, 1)).reshape(rows, S5_STATE)
    bi = jnp.transpose(b_im, (0, 2, 1)).reshape(rows, S5_STATE)
    shp = jax.ShapeDtypeStruct((rows, S5_STATE), F32)
    return pl.pallas_call(_s5prep_kernel, out_shape=(shp, shp, shp, shp), name="s5prep")(lr, li, ldt, br, bi)


def _block_diag(blocks):
    g, r, c = blocks.shape
    eye = jnp.eye(g, dtype=blocks.dtype)
    return jnp.einsum("grc,gk->grkc", blocks, eye).reshape(g * r, g * c)


def _proj_kernel(x_ref, sh_ref, sc_ref, g_ref, wu_ref, wv_ref, wg_ref, wa_ref, wb_ref,
                 u_ref, zc_ref, ga_ref, gb_ref):
    x = x_ref[...]
    ms = jnp.mean(x * x, axis=-1, keepdims=True)
    h = x * lax.rsqrt(ms + EPS) * g_ref[...]
    h = (h * (1.0 + sc_ref[...]) + sh_ref[...]).astype(BF16)
    u_ref[...] = jnp.dot(h, wu_ref[...], preferred_element_type=F32).astype(BF16)
    v = jnp.dot(h, wv_ref[...], preferred_element_type=F32)
    g = jnp.dot(h, wg_ref[...], preferred_element_type=F32)
    zc_ref[...] = (v * _sigmoid(g)).astype(BF16)
    ga_ref[...] = _sigmoid(jnp.dot(h, wa_ref[...], preferred_element_type=F32)).astype(BF16)
    gb_ref[...] = _sigmoid(jnp.dot(h, wb_ref[...], preferred_element_type=F32)).astype(BF16)


def _proj(x, sh1, sc1, g1, w_in):
    o1 = S5_WIDTH
    o2 = o1 + CONV_WIDTH
    o3 = o2 + CONV_WIDTH
    o4 = o3 + D_MODEL
    w = w_in.astype(BF16)
    ws = (w[:, :o1], w[:, o1:o2], w[:, o2:o3], w[:, o3:o4], w[:, o4:])
    ts = PROJ_ROWS
    mod_spec = pl.BlockSpec((None, 1, D_MODEL), lambda b, i: (b, 0, 0))
    out_shapes = (jax.ShapeDtypeStruct((SEQ, BATCH * S5_WIDTH), BF16),
                  jax.ShapeDtypeStruct((SEQ, BATCH * CONV_WIDTH), BF16),
                  jax.ShapeDtypeStruct((SEQ, BATCH * D_MODEL), BF16),
                  jax.ShapeDtypeStruct((SEQ, BATCH * D_MODEL), BF16))
    out_specs = (pl.BlockSpec((ts, S5_WIDTH), lambda b, i: (i, b)),
                 pl.BlockSpec((ts, CONV_WIDTH), lambda b, i: (i, b)),
                 pl.BlockSpec((ts, D_MODEL), lambda b, i: (i, b)),
                 pl.BlockSpec((ts, D_MODEL), lambda b, i: (i, b)))
    return pl.pallas_call(
        _proj_kernel,
        out_shape=out_shapes,
        grid=(BATCH, SEQ // ts),
        in_specs=[pl.BlockSpec((None, ts, D_MODEL), lambda b, i: (b, i, 0)),
                  mod_spec, mod_spec,
                  _const_spec((1, D_MODEL))] + [_const_spec(wi.shape) for wi in ws],
        out_specs=out_specs,
        compiler_params=pltpu.CompilerParams(
            dimension_semantics=("parallel", "parallel"), vmem_limit_bytes=VMEM_LIMIT),
        name="proj",
    )(x, sh1, sc1, g1, *ws)


def _gelu_tanh(x):
    sqrt_2_over_pi = 0.7978845608028654
    cdf = 0.5 * (1.0 + jnp.tanh(sqrt_2_over_pi * (x + 0.044715 * (x * x * x))))
    return x * cdf


def _mix_kernel(u_ref, zc_ref, ga_ref, gb_ref, bmat_ref, cr_ref, ci_ref, ar_ref, ai_ref, d_ref,
                wv_ref, wg_ref, dww_ref, dwb_ref, lng_ref, lnb_ref, pw_ref, wout_ref,
                out_ref, bu_ref, sr_ref, si_ref, cbuf_ref, cv_ref):
    step = pl.program_id(0)

    @pl.when(step == 0)
    def _():
        sr_ref[...] = jnp.zeros_like(sr_ref)
        si_ref[...] = jnp.zeros_like(si_ref)
        cbuf_ref[0:CONV_HALO, :] = jnp.zeros((CONV_HALO, CONV_WIDTH), F32)

    u = u_ref[...]
    bu_ref[...] = jnp.dot(u, bmat_ref[...], preferred_element_type=F32)

    for lg in range(N_STATE // SCAN_LANES):
        lo = lg * SCAN_LANES
        re = slice(lo, lo + SCAN_LANES)
        im = slice(N_STATE + lo, N_STATE + lo + SCAN_LANES)
        a_r = jnp.broadcast_to(ar_ref[:, re], (SUBLANES, SCAN_LANES))
        a_i = jnp.broadcast_to(ai_ref[:, re], (SUBLANES, SCAN_LANES))

        def body(t, carry, re=re, im=im, a_r=a_r, a_i=a_i):
            s_r, s_i = carry
            rows = pl.ds(pl.multiple_of(t * SUBLANES, SUBLANES), SUBLANES)
            n_r = a_r * s_r - a_i * s_i + bu_ref[rows, re]
            n_i = a_r * s_i + a_i * s_r + bu_ref[rows, im]
            bu_ref[rows, re] = n_r
            bu_ref[rows, im] = n_i
            return n_r, n_i

        s_r, s_i = lax.fori_loop(0, MIX_STEPS, body, (sr_ref[:, re], si_ref[:, re]))
        sr_ref[:, re] = s_r
        si_ref[:, re] = s_i

    y = (jnp.dot(bu_ref[:, 0:N_STATE].astype(BF16), cr_ref[...], preferred_element_type=F32)
         - jnp.dot(bu_ref[:, N_STATE:2 * N_STATE].astype(BF16), ci_ref[...], preferred_element_type=F32)
         + d_ref[...] * u.astype(F32))
    z = _gelu_tanh(y).astype(BF16)
    y_a = (jnp.dot(z, wv_ref[...], preferred_element_type=F32)
           * _sigmoid(jnp.dot(z, wg_ref[...], preferred_element_type=F32)))

    cbuf_ref[CONV_HALO:CONV_HALO + MIX_ROWS, :] = zc_ref[...].astype(F32)
    bias = dwb_ref[...]

    def conv_chunk(ci, carry):
        r0 = pl.multiple_of(ci * CONV_CHUNK, CONV_CHUNK)
        acc = jnp.broadcast_to(bias, (CONV_CHUNK, CONV_WIDTH))
        for k in range(CONV_KERNEL):
            acc = acc + dww_ref[k:k + 1, :] * cbuf_ref[pl.ds(r0 + k * BATCH, CONV_CHUNK), :]
        cv_ref[pl.ds(r0, CONV_CHUNK), :] = acc
        return carry

    lax.fori_loop(0, MIX_ROWS // CONV_CHUNK, conv_chunk, 0)
    cbuf_ref[0:CONV_HALO, :] = cbuf_ref[MIX_ROWS:MIX_ROWS + CONV_HALO, :]

    cv = cv_ref[...]
    mu = jnp.mean(cv, axis=-1, keepdims=True)
    var = jnp.mean(jnp.square(cv - mu), axis=-1, keepdims=True)
    ln = (cv - mu) * lax.rsqrt(var + EPS) * lng_ref[...] + lnb_ref[...]
    zb = (ln * _sigmoid(ln)).astype(BF16)
    y_b = jnp.dot(zb, pw_ref[...], preferred_element_type=F32)

    m = ga_ref[...].astype(F32) * y_a + gb_ref[...].astype(F32) * y_b
    out_ref[...] = jnp.dot(m.astype(BF16), wout_ref[...], preferred_element_type=F32).astype(BF16)


def _mix(u2, zc2, ga2, gb2, bmat, c_r, c_i, a_r, a_i, d_row, wv, wg, dww, dwb, lng, lnb, pw, wout):
    rows = MIX_ROWS
    row_spec = lambda c: pl.BlockSpec((rows, c), lambda i: (i, 0))
    consts = (bmat, c_r, c_i, a_r, a_i, d_row, wv, wg, dww, dwb, lng, lnb, pw, wout)
    return pl.pallas_call(
        _mix_kernel,
        out_shape=jax.ShapeDtypeStruct((SEQ * BATCH, D_MODEL), BF16),
        grid=(SEQ // MIX_STEPS,),
        in_specs=[row_spec(S5_WIDTH), row_spec(CONV_WIDTH), row_spec(D_MODEL), row_spec(D_MODEL)]
                 + [_const_spec(a.shape) for a in consts],
        out_specs=row_spec(D_MODEL),
        scratch_shapes=[pltpu.VMEM((rows, 2 * N_STATE), F32),
                        pltpu.VMEM((SUBLANES, N_STATE), F32),
                        pltpu.VMEM((SUBLANES, N_STATE), F32),
                        pltpu.VMEM((CONV_HALO + rows, CONV_WIDTH), F32),
                        pltpu.VMEM((rows, CONV_WIDTH), F32)],
        compiler_params=pltpu.CompilerParams(
            dimension_semantics=("arbitrary",), vmem_limit_bytes=VMEM_LIMIT),
        name="mix",
    )(u2, zc2, ga2, gb2, *consts)


def _post_kernel(x_ref, mx_ref, gt1_ref, sh_ref, sc_ref, gt2_ref, g_ref, rw_ref, rb_ref,
                 sgu_ref, sd_ref, x1s_ref, h2_ref, wc_ref):
    x1 = x_ref[...] + gt1_ref[...] * mx_ref[...].astype(F32)
    ms = jnp.mean(x1 * x1, axis=-1, keepdims=True)
    h = x1 * lax.rsqrt(ms + EPS) * g_ref[...]
    h = h * (1.0 + sc_ref[...]) + sh_ref[...]
    hb = h.astype(BF16)
    h2_ref[...] = hb

    scores = _sigmoid(jnp.dot(hb, rw_ref[...], preferred_element_type=F32))
    lane = lax.broadcasted_iota(jnp.int32, scores.shape, 1)
    neg_inf = jnp.float32(-jnp.inf)
    work = jnp.where(lane < N_EXPERTS, scores + rb_ref[...], neg_inf)
    sel = jnp.zeros(scores.shape, jnp.bool_)
    for _ in range(TOP_K):
        m = jnp.max(work, axis=-1, keepdims=True)
        idx = jnp.min(jnp.where(work == m, lane, LANES), axis=-1, keepdims=True)
        hit = lane == idx
        sel = jnp.logical_or(sel, hit)
        work = jnp.where(hit, neg_inf, work)
    w_sel = jnp.where(sel, scores, 0.0)
    wc_ref[...] = w_sel / jnp.sum(w_sel, axis=-1, keepdims=True) * ROUTED_SCALE

    gu = jnp.dot(hb, sgu_ref[...], preferred_element_type=F32)
    g = gu[:, :SHARED_HIDDEN]
    act = (g * _sigmoid(g)) * gu[:, SHARED_HIDDEN:]
    shared = jnp.dot(act.astype(BF16), sd_ref[...], preferred_element_type=F32)
    x1s_ref[...] = x1 + gt2_ref[...] * shared


def _post(x, mixed, gt1, sh2, sc2, gt2, g2, rw_pad, rb_pad, sgu, sd):
    ts = PROJ_ROWS
    mod_spec = pl.BlockSpec((None, 1, D_MODEL), lambda b, i: (b, 0, 0))
    tok_spec = lambda c: pl.BlockSpec((None, ts, c), lambda b, i: (b, i, 0))
    return pl.pallas_call(
        _post_kernel,
        out_shape=(jax.ShapeDtypeStruct((BATCH, SEQ, D_MODEL), F32),
                   jax.ShapeDtypeStruct((BATCH, SEQ, D_MODEL), BF16),
                   jax.ShapeDtypeStruct((BATCH, SEQ, LANES), F32)),
        grid=(BATCH, SEQ // ts),
        in_specs=[tok_spec(D_MODEL),
                  pl.BlockSpec((ts, D_MODEL), lambda b, i: (i, b)),
                  mod_spec, mod_spec, mod_spec, mod_spec,
                  _const_spec((1, D_MODEL)), _const_spec(rw_pad.shape), _const_spec(rb_pad.shape),
                  _const_spec(sgu.shape), _const_spec(sd.shape)],
        out_specs=(tok_spec(D_MODEL), tok_spec(D_MODEL), tok_spec(LANES)),
        compiler_params=pltpu.CompilerParams(
            dimension_semantics=("parallel", "parallel"), vmem_limit_bytes=VMEM_LIMIT),
        name="post",
    )(x, mixed, gt1, sh2, sc2, gt2, g2, rw_pad, rb_pad, sgu, sd)


def _moe_kernel(h_ref, wc_ref, x1s_ref, gt2_ref, wg_ref, wu_ref, wd_ref, fg_ref, o_ref, acc_ref):
    e = pl.program_id(1)

    @pl.when(e == 0)
    def _():
        acc_ref[...] = jnp.zeros_like(acc_ref)

    h = h_ref[...]
    g = jnp.dot(h, wg_ref[...], preferred_element_type=F32)
    u = jnp.dot(h, wu_ref[...], preferred_element_type=F32)
    wc = wc_ref[...]
    lane = lax.broadcasted_iota(jnp.int32, wc.shape, 1)
    w_e = jnp.sum(jnp.where(lane == e, wc, 0.0), axis=-1, keepdims=True)
    act = (g * _sigmoid(g)) * u * w_e
    acc_ref[...] += jnp.dot(act.astype(BF16), wd_ref[...], preferred_element_type=F32)

    @pl.when(e == N_EXPERTS - 1)
    def _():
        x2 = x1s_ref[...] + gt2_ref[...] * acc_ref[...]
        ms = jnp.mean(x2 * x2, axis=-1, keepdims=True)
        o_ref[...] = x2 * lax.rsqrt(ms + EPS) * fg_ref[...]


def _moe(h2, wc, x1s, gt2, wg, wu, wd, fg):
    tm = MOE_ROWS
    tiles_per_seq = SEQ // tm
    tok_spec = lambda c: pl.BlockSpec((tm, c), lambda i, e: (i, 0))
    return pl.pallas_call(
        _moe_kernel,
        out_shape=jax.ShapeDtypeStruct((N_TOK, D_MODEL), F32),
        grid=(N_TOK // tm, N_EXPERTS),
        in_specs=[tok_spec(D_MODEL), tok_spec(LANES), tok_spec(D_MODEL),
                  pl.BlockSpec((None, 1, D_MODEL), lambda i, e: (i // tiles_per_seq, 0, 0)),
                  pl.BlockSpec((None, D_MODEL, EXPERT_HIDDEN), lambda i, e: (e, 0, 0)),
                  pl.BlockSpec((None, D_MODEL, EXPERT_HIDDEN), lambda i, e: (e, 0, 0)),
                  pl.BlockSpec((None, EXPERT_HIDDEN, D_MODEL), lambda i, e: (e, 0, 0)),
                  pl.BlockSpec((1, D_MODEL), lambda i, e: (0, 0))],
        out_specs=tok_spec(D_MODEL),
        scratch_shapes=[pltpu.VMEM((tm, D_MODEL), F32)],
        compiler_params=pltpu.CompilerParams(
            dimension_semantics=("parallel", "arbitrary"), vmem_limit_bytes=VMEM_LIMIT),
        name="moe",
    )(h2, wc, x1s, gt2, wg, wu, wd, fg)


def kernel(x, c, w_ada, b_ada, norm1_g, w_in, s5_lambda_re, s5_lambda_im, s5_log_dt, s5_b_re, s5_b_im, s5_c_re, s5_c_im, s5_d, s5_glu_wv, s5_glu_wg, conv_dw_w, conv_dw_b, conv_ln_g, conv_ln_b, conv_pw_w, w_out, norm2_g, router_w, router_bias, exp_w_gate, exp_w_up, exp_w_down, shared_w_gate, shared_w_up, shared_w_down, final_norm_g):
    l = 0
    row = lambda a: a.reshape(1, -1)

    mod = _ada(c, w_ada[l], b_ada[l])
    sh1, sc1, gt1, sh2, sc2, gt2 = [m.reshape(BATCH, 1, D_MODEL) for m in jnp.split(mod, N_MOD, axis=-1)]

    abar_r, abar_i, bbar_r, bbar_i = _s5prep(s5_lambda_re[l], s5_lambda_im[l], s5_log_dt[l],
                                             s5_b_re[l], s5_b_im[l])
    per_group = lambda a: a.reshape(S5_GROUPS, S5_GROUP, S5_STATE)
    a_r = per_group(abar_r)[:, 0, :].reshape(1, N_STATE)
    a_i = per_group(abar_i)[:, 0, :].reshape(1, N_STATE)
    bmat = jnp.concatenate([_block_diag(per_group(bbar_r)), _block_diag(per_group(bbar_i))],
                           axis=1).astype(BF16)
    c_r = _block_diag(jnp.transpose(s5_c_re[l], (0, 2, 1))).astype(BF16)
    c_i = _block_diag(jnp.transpose(s5_c_im[l], (0, 2, 1))).astype(BF16)

    u2, zc2, ga2, gb2 = _proj(x, sh1, sc1, row(norm1_g[l]), w_in[l])
    tb = lambda a, cdim: a.reshape(SEQ * BATCH, cdim)
    mixed = _mix(tb(u2, S5_WIDTH), tb(zc2, CONV_WIDTH), tb(ga2, D_MODEL), tb(gb2, D_MODEL),
                 bmat, c_r, c_i, a_r, a_i, row(s5_d[l]),
                 s5_glu_wv[l].astype(BF16), s5_glu_wg[l].astype(BF16),
                 conv_dw_w[l], row(conv_dw_b[l]), row(conv_ln_g[l]), row(conv_ln_b[l]),
                 conv_pw_w[l].astype(BF16), w_out[l].astype(BF16))
    mixed = mixed.reshape(SEQ, BATCH * D_MODEL)

    pad = LANES - N_EXPERTS
    rw_pad = jnp.pad(router_w[l], ((0, 0), (0, pad))).astype(BF16)
    rb_pad = jnp.pad(row(router_bias[l]), ((0, 0), (0, pad)))
    sgu = jnp.concatenate([shared_w_gate[l], shared_w_up[l]], axis=1).astype(BF16)
    x1s, h2, wc = _post(x, mixed, gt1, sh2, sc2, gt2, row(norm2_g[l]), rw_pad, rb_pad,
                        sgu, shared_w_down[l].astype(BF16))

    out = _moe(h2.reshape(N_TOK, D_MODEL), wc.reshape(N_TOK, LANES), x1s.reshape(N_TOK, D_MODEL), gt2,
               exp_w_gate[l].astype(BF16), exp_w_up[l].astype(BF16), exp_w_down[l].astype(BF16),
               row(final_norm_g))
    return out.reshape(BATCH, SEQ, D_MODEL)
```

```python
import functools

import jax
import jax.numpy as jnp
from jax import lax
from jax.experimental import pallas as pl
from jax.experimental.pallas import tpu as pltpu

D_MODEL = 1024
BATCH = 8
SEQ = 4096
N_TOK = BATCH * SEQ
S5_WIDTH = 512
S5_GROUP = 16
S5_GROUPS = 32
S5_STATE = 64
N_STATE = S5_GROUPS * S5_STATE
CONV_WIDTH = 512
CONV_KERNEL = 31
N_EXPERTS = 64
TOP_K = 8
EXPERT_HIDDEN = 256
SHARED_HIDDEN = 256
ROUTED_SCALE = 2.5
N_MOD = 6
EPS = 1e-6

LANES = 128
SUBLANES = 8
VMEM_LIMIT = 56 * 1024 * 1024

PROJ_ROWS = 512
MIX_STEPS = 64
MIX_ROWS = MIX_STEPS * BATCH
CONV_HALO = (CONV_KERNEL - 1) * BATCH
CONV_CHUNK = 64
SCAN_LANES = 512
MOE_BLOCK = 2048
MOE_BLOCKS = N_TOK // MOE_BLOCK
MOE_CHUNK = 256
SLAB = 4
STAGE_ROWS = TOP_K * MOE_BLOCK + MOE_CHUNK

F32 = jnp.float32
BF16 = jnp.bfloat16
U32 = jnp.uint32
I32 = jnp.int32


def _pack_rows(x):
    half = D_MODEL // 2
    return pltpu.pack_elementwise([x[:, :half], x[:, half:]], packed_dtype=BF16)


def _unpack_words(w):
    lo = pltpu.unpack_elementwise(w, index=0, packed_dtype=BF16, unpacked_dtype=F32)
    hi = pltpu.unpack_elementwise(w, index=1, packed_dtype=BF16, unpacked_dtype=F32)
    return lo, hi


def _store_slabs(ref, base, words):
    rows = words.shape[0]
    for s in range(SLAB):
        ref[pl.ds(base + s, rows, stride=SLAB), :] = words[:, s * LANES:(s + 1) * LANES]


def _load_slabs(ref, base, rows):
    return [ref[pl.ds(base + s, rows, stride=SLAB), :] for s in range(SLAB)]


def _bdot(a, b):
    return jnp.dot(a.astype(BF16), b.astype(BF16), preferred_element_type=F32)


def _sigmoid(x):
    return jax.nn.sigmoid(x)


def _const_spec(shape):
    zeros = (0,) * len(shape)
    return pl.BlockSpec(shape, lambda *_: zeros, pipeline_mode=pl.Buffered(1))


def _ada_kernel(c_ref, w_ref, b_ref, o_ref):
    c = c_ref[...]
    c_act = c * _sigmoid(c)
    o_ref[...] = jnp.dot(c_act, w_ref[...], preferred_element_type=F32,
                         precision=lax.Precision.HIGHEST) + b_ref[...]


def _ada(c, w_ada, b_ada):
    n_out = N_MOD * D_MODEL
    blk = 1536
    return pl.pallas_call(
        _ada_kernel,
        out_shape=jax.ShapeDtypeStruct((BATCH, n_out), F32),
        grid=(n_out // blk,),
        in_specs=[pl.BlockSpec((BATCH, D_MODEL), lambda j: (0, 0)),
                  pl.BlockSpec((D_MODEL, blk), lambda j: (0, j)),
                  pl.BlockSpec((1, blk), lambda j: (0, j))],
        out_specs=pl.BlockSpec((BATCH, blk), lambda j: (0, j)),
        compiler_params=pltpu.CompilerParams(vmem_limit_bytes=VMEM_LIMIT),
        name="ada",
    )(c, w_ada, b_ada.reshape(1, n_out))


def _s5prep_kernel(lr_ref, li_ref, ldt_ref, br_ref, bi_ref, ar_ref, ai_ref, bbr_ref, bbi_ref):
    lr = lr_ref[...]
    li = li_ref[...]
    dt = jnp.exp(ldt_ref[...])
    mag = jnp.exp(lr * dt)
    abar_r = mag * jnp.cos(li * dt)
    abar_i = mag * jnp.sin(li * dt)
    den = lr * lr + li * li
    nr = abar_r - 1.0
    ni = abar_i
    k_r = (nr * lr + ni * li) / den
    k_i = (ni * lr - nr * li) / den
    br = br_ref[...]
    bi = bi_ref[...]
    ar_ref[...] = abar_r
    ai_ref[...] = abar_i
    bbr_ref[...] = k_r * br - k_i * bi
    bbi_ref[...] = k_r * bi + k_i * br


def _s5prep(lam_re, lam_im, log_dt, b_re, b_im):
    rows = S5_GROUPS * S5_GROUP
    rep = lambda a: jnp.broadcast_to(a[:, None, :], (S5_GROUPS, S5_GROUP, S5_STATE)).reshape(rows, S5_STATE)
    lr = rep(lam_re)
    li = rep(lam_im)
    ldt = rep(jnp.broadcast_to(log_dt[:, None], (S5_GROUPS, S5_STATE)))
    br = jnp.transpose(b_re, (0, 2, 1)).reshape(rows, S5_STATE)
    bi = jnp.transpose(b_im, (0, 2, 1)).reshape(rows, S5_STATE)
    shp = jax.ShapeDtypeStruct((rows, S5_STATE), F32)
    return pl.pallas_call(_s5prep_kernel, out_shape=(shp, shp, shp, shp), name="s5prep")(lr, li, ldt, br, bi)


def _block_diag(blocks):
    g, r, c = blocks.shape
    eye = jnp.eye(g, dtype=blocks.dtype)
    return jnp.einsum("grc,gk->grkc", blocks, eye).reshape(g * r, g * c)


def _proj_kernel(x_ref, sh_ref, sc_ref, g_ref, wu_ref, wv_ref, wg_ref, wa_ref, wb_ref,
                 u_ref, zc_ref, ga_ref, gb_ref):
    x = x_ref[...]
    ms = jnp.mean(x * x, axis=-1, keepdims=True)
    h = x * lax.rsqrt(ms + EPS) * g_ref[...]
    h = (h * (1.0 + sc_ref[...]) + sh_ref[...]).astype(BF16)
    u_ref[...] = jnp.dot(h, wu_ref[...], preferred_element_type=F32).astype(BF16)
    v = jnp.dot(h, wv_ref[...], preferred_element_type=F32)
    g = jnp.dot(h, wg_ref[...], preferred_element_type=F32)
    zc_ref[...] = (v * _sigmoid(g)).astype(BF16)
    ga_ref[...] = _sigmoid(jnp.dot(h, wa_ref[...], preferred_element_type=F32)).astype(BF16)
    gb_ref[...] = _sigmoid(jnp.dot(h, wb_ref[...], preferred_element_type=F32)).astype(BF16)


def _proj(x, sh1, sc1, g1, w_in):
    o1 = S5_WIDTH
    o2 = o1 + CONV_WIDTH
    o3 = o2 + CONV_WIDTH
    o4 = o3 + D_MODEL
    w = w_in.astype(BF16)
    ws = (w[:, :o1], w[:, o1:o2], w[:, o2:o3], w[:, o3:o4], w[:, o4:])
    ts = PROJ_ROWS
    mod_spec = pl.BlockSpec((None, 1, D_MODEL), lambda b, i: (b, 0, 0))
    out_shapes = (jax.ShapeDtypeStruct((SEQ, BATCH * S5_WIDTH), BF16),
                  jax.ShapeDtypeStruct((SEQ, BATCH * CONV_WIDTH), BF16),
                  jax.ShapeDtypeStruct((SEQ, BATCH * D_MODEL), BF16),
                  jax.ShapeDtypeStruct((SEQ, BATCH * D_MODEL), BF16))
    out_specs = (pl.BlockSpec((ts, S5_WIDTH), lambda b, i: (i, b)),
                 pl.BlockSpec((ts, CONV_WIDTH), lambda b, i: (i, b)),
                 pl.BlockSpec((ts, D_MODEL), lambda b, i: (i, b)),
                 pl.BlockSpec((ts, D_MODEL), lambda b, i: (i, b)))
    return pl.pallas_call(
        _proj_kernel,
        out_shape=out_shapes,
        grid=(BATCH, SEQ // ts),
        in_specs=[pl.BlockSpec((None, ts, D_MODEL), lambda b, i: (b, i, 0)),
                  mod_spec, mod_spec,
                  _const_spec((1, D_MODEL))] + [_const_spec(wi.shape) for wi in ws],
        out_specs=out_specs,
        compiler_params=pltpu.CompilerParams(
            dimension_semantics=("parallel", "parallel"), vmem_limit_bytes=VMEM_LIMIT),
        name="proj",
    )(x, sh1, sc1, g1, *ws)


def _gelu_tanh(x):
    sqrt_2_over_pi = 0.7978845608028654
    cdf = 0.5 * (1.0 + jnp.tanh(sqrt_2_over_pi * (x + 0.044715 * (x * x * x))))
    return x * cdf


def _mix_kernel(u_ref, zc_ref, ga_ref, gb_ref, bmat_ref, cr_ref, ci_ref, ar_ref, ai_ref, d_ref,
                wv_ref, wg_ref, dww_ref, dwb_ref, lng_ref, lnb_ref, pw_ref, wout_ref,
                out_ref, bu_ref, sr_ref, si_ref, cbuf_ref, cv_ref):
    step = pl.program_id(0)

    @pl.when(step == 0)
    def _():
        sr_ref[...] = jnp.zeros_like(sr_ref)
        si_ref[...] = jnp.zeros_like(si_ref)
        cbuf_ref[0:CONV_HALO, :] = jnp.zeros((CONV_HALO, CONV_WIDTH), F32)

    u = u_ref[...]
    bu_ref[...] = jnp.dot(u, bmat_ref[...], preferred_element_type=F32)

    for lg in range(N_STATE // SCAN_LANES):
        lo = lg * SCAN_LANES
        re = slice(lo, lo + SCAN_LANES)
        im = slice(N_STATE + lo, N_STATE + lo + SCAN_LANES)
        a_r = jnp.broadcast_to(ar_ref[:, re], (SUBLANES, SCAN_LANES))
        a_i = jnp.broadcast_to(ai_ref[:, re], (SUBLANES, SCAN_LANES))

        def body(t, carry, re=re, im=im, a_r=a_r, a_i=a_i):
            s_r, s_i = carry
            rows = pl.ds(pl.multiple_of(t * SUBLANES, SUBLANES), SUBLANES)
            n_r = a_r * s_r - a_i * s_i + bu_ref[rows, re]
            n_i = a_r * s_i + a_i * s_r + bu_ref[rows, im]
            bu_ref[rows, re] = n_r
            bu_ref[rows, im] = n_i
            return n_r, n_i

        s_r, s_i = lax.fori_loop(0, MIX_STEPS, body, (sr_ref[:, re], si_ref[:, re]))
        sr_ref[:, re] = s_r
        si_ref[:, re] = s_i

    y = (jnp.dot(bu_ref[:, 0:N_STATE].astype(BF16), cr_ref[...], preferred_element_type=F32)
         - jnp.dot(bu_ref[:, N_STATE:2 * N_STATE].astype(BF16), ci_ref[...], preferred_element_type=F32)
         + d_ref[...] * u.astype(F32))
    z = _gelu_tanh(y).astype(BF16)
    y_a = (jnp.dot(z, wv_ref[...], preferred_element_type=F32)
           * _sigmoid(jnp.dot(z, wg_ref[...], preferred_element_type=F32)))

    cbuf_ref[CONV_HALO:CONV_HALO + MIX_ROWS, :] = zc_ref[...].astype(F32)
    bias = dwb_ref[...]

    def conv_chunk(ci, carry):
        r0 = pl.multiple_of(ci * CONV_CHUNK, CONV_CHUNK)
        acc = jnp.broadcast_to(bias, (CONV_CHUNK, CONV_WIDTH))
        for k in range(CONV_KERNEL):
            acc = acc + dww_ref[k:k + 1, :] * cbuf_ref[pl.ds(r0 + k * BATCH, CONV_CHUNK), :]
        cv_ref[pl.ds(r0, CONV_CHUNK), :] = acc
        return carry

    lax.fori_loop(0, MIX_ROWS // CONV_CHUNK, conv_chunk, 0)
    cbuf_ref[0:CONV_HALO, :] = cbuf_ref[MIX_ROWS:MIX_ROWS + CONV_HALO, :]

    cv = cv_ref[...]
    mu = jnp.mean(cv, axis=-1, keepdims=True)
    var = jnp.mean(jnp.square(cv - mu), axis=-1, keepdims=True)
    ln = (cv - mu) * lax.rsqrt(var + EPS) * lng_ref[...] + lnb_ref[...]
    zb = (ln * _sigmoid(ln)).astype(BF16)
    y_b = jnp.dot(zb, pw_ref[...], preferred_element_type=F32)

    m = ga_ref[...].astype(F32) * y_a + gb_ref[...].astype(F32) * y_b
    out_ref[...] = jnp.dot(m.astype(BF16), wout_ref[...], preferred_element_type=F32).astype(BF16)


def _mix(u2, zc2, ga2, gb2, bmat, c_r, c_i, a_r, a_i, d_row, wv, wg, dww, dwb, lng, lnb, pw, wout):
    rows = MIX_ROWS
    row_spec = lambda c: pl.BlockSpec((rows, c), lambda i: (i, 0))
    consts = (bmat, c_r, c_i, a_r, a_i, d_row, wv, wg, dww, dwb, lng, lnb, pw, wout)
    return pl.pallas_call(
        _mix_kernel,
        out_shape=jax.ShapeDtypeStruct((SEQ * BATCH, D_MODEL), BF16),
        grid=(SEQ // MIX_STEPS,),
        in_specs=[row_spec(S5_WIDTH), row_spec(CONV_WIDTH), row_spec(D_MODEL), row_spec(D_MODEL)]
                 + [_const_spec(a.shape) for a in consts],
        out_specs=row_spec(D_MODEL),
        scratch_shapes=[pltpu.VMEM((rows, 2 * N_STATE), F32),
                        pltpu.VMEM((SUBLANES, N_STATE), F32),
                        pltpu.VMEM((SUBLANES, N_STATE), F32),
                        pltpu.VMEM((CONV_HALO + rows, CONV_WIDTH), F32),
                        pltpu.VMEM((rows, CONV_WIDTH), F32)],
        compiler_params=pltpu.CompilerParams(
            dimension_semantics=("arbitrary",), vmem_limit_bytes=VMEM_LIMIT),
        name="mix",
    )(u2, zc2, ga2, gb2, *consts)


def _post_kernel(x_ref, mx_ref, gt1_ref, sh_ref, sc_ref, gt2_ref, g_ref, rw_ref, rb_ref,
                 sgu_ref, sd_ref, tri_ref, x1s_ref, h2_ref, ek_ref, rk_ref, wk_ref, cnt_ref):
    x1 = x_ref[...] + gt1_ref[...] * mx_ref[...].astype(F32)
    ms = jnp.mean(x1 * x1, axis=-1, keepdims=True)
    h = x1 * lax.rsqrt(ms + EPS) * g_ref[...]
    h = h * (1.0 + sc_ref[...]) + sh_ref[...]
    hb = h.astype(BF16)
    _store_slabs(h2_ref, 0, _pack_rows(hb.astype(F32)))

    scores = _sigmoid(jnp.dot(hb, rw_ref[...], preferred_element_type=F32))
    lane = lax.broadcasted_iota(I32, scores.shape, 1)
    neg_inf = jnp.float32(-jnp.inf)
    work = jnp.where(lane < N_EXPERTS, scores + rb_ref[...], neg_inf)
    sel = jnp.zeros(scores.shape, jnp.bool_)
    hits = []
    e_k = jnp.zeros(scores.shape, I32)
    for k in range(TOP_K):
        m = jnp.max(work, axis=-1, keepdims=True)
        idx = jnp.min(jnp.where(work == m, lane, LANES), axis=-1, keepdims=True)
        hit = lane == idx
        hits.append(hit)
        e_k = jnp.where(lane == k, idx, e_k)
        sel = jnp.logical_or(sel, hit)
        work = jnp.where(hit, neg_inf, work)
    w_sel = jnp.where(sel, scores, 0.0)
    wc = w_sel / jnp.sum(w_sel, axis=-1, keepdims=True) * ROUTED_SCALE

    @pl.when(pl.program_id(1) % (MOE_BLOCK // PROJ_ROWS) == 0)
    def _():
        cnt_ref[...] = jnp.zeros_like(cnt_ref)

    sel_f = sel.astype(F32)
    carry = cnt_ref[...]
    prefix = jnp.dot(tri_ref[...], sel_f.astype(BF16), preferred_element_type=F32) + carry
    cnt_ref[...] = carry + jnp.sum(sel_f, axis=0, keepdims=True)
    r_k = jnp.zeros(scores.shape, F32)
    w_k = jnp.zeros(scores.shape, F32)
    for k in range(TOP_K):
        r_k = jnp.where(lane == k, jnp.sum(jnp.where(hits[k], prefix, 0.0), axis=-1, keepdims=True), r_k)
        w_k = jnp.where(lane == k, jnp.sum(jnp.where(hits[k], wc, 0.0), axis=-1, keepdims=True), w_k)
    ek_ref[...] = e_k
    rk_ref[...] = r_k.astype(I32)
    wk_ref[...] = w_k

    gu = jnp.dot(hb, sgu_ref[...], preferred_element_type=F32)
    g = gu[:, :SHARED_HIDDEN]
    act = (g * _sigmoid(g)) * gu[:, SHARED_HIDDEN:]
    shared = jnp.dot(act.astype(BF16), sd_ref[...], preferred_element_type=F32)
    x1s_ref[...] = x1 + gt2_ref[...] * shared


def _post(x, mixed, gt1, sh2, sc2, gt2, g2, rw_pad, rb_pad, sgu, sd):
    ts = PROJ_ROWS
    tiles = SEQ // ts
    tiles_per_block = MOE_BLOCK // ts
    mod_spec = pl.BlockSpec((None, 1, D_MODEL), lambda b, i: (b, 0, 0))
    tok3_spec = pl.BlockSpec((None, ts, D_MODEL), lambda b, i: (b, i, 0))
    flat_spec = lambda rows, c: pl.BlockSpec((rows, c), lambda b, i: (b * tiles + i, 0))
    tri = (lax.broadcasted_iota(I32, (ts, ts), 0) > lax.broadcasted_iota(I32, (ts, ts), 1)).astype(BF16)
    return pl.pallas_call(
        _post_kernel,
        out_shape=(jax.ShapeDtypeStruct((N_TOK, D_MODEL), F32),
                   jax.ShapeDtypeStruct((N_TOK * SLAB, LANES), U32),
                   jax.ShapeDtypeStruct((N_TOK, LANES), I32),
                   jax.ShapeDtypeStruct((N_TOK, LANES), I32),
                   jax.ShapeDtypeStruct((N_TOK, LANES), F32),
                   jax.ShapeDtypeStruct((MOE_BLOCKS, 1, LANES), F32)),
        grid=(BATCH, tiles),
        in_specs=[tok3_spec,
                  pl.BlockSpec((ts, D_MODEL), lambda b, i: (i, b)),
                  mod_spec, mod_spec, mod_spec, mod_spec,
                  _const_spec((1, D_MODEL)), _const_spec(rw_pad.shape), _const_spec(rb_pad.shape),
                  _const_spec(sgu.shape), _const_spec(sd.shape), _const_spec(tri.shape)],
        out_specs=(flat_spec(ts, D_MODEL), flat_spec(ts * SLAB, LANES),
                   flat_spec(ts, LANES), flat_spec(ts, LANES), flat_spec(ts, LANES),
                   pl.BlockSpec((None, 1, LANES), lambda b, i: ((b * tiles + i) // tiles_per_block, 0, 0))),
        compiler_params=pltpu.CompilerParams(
            dimension_semantics=("arbitrary", "arbitrary"), vmem_limit_bytes=VMEM_LIMIT),
        name="post",
    )(x, mixed, gt1, sh2, sc2, gt2, g2, rw_pad, rb_pad, sgu, sd, tri)


def _moe_kernel(cnt_ref, off_ref, dst_ref, w_ref, h_ref, wgu_ref, wd_ref, o_ref, stage_ref):
    blk = pl.program_id(0)
    step = pl.program_id(1)

    @pl.when(jnp.logical_and(blk == 0, step == 0))
    def _():
        stage_ref[TOP_K * MOE_BLOCK * SLAB:, :] = jnp.zeros((MOE_CHUNK * SLAB, LANES), U32)

    @pl.when(step == 0)
    def _():
        def dispatch(t, carry):
            slab = h_ref[pl.ds(pl.multiple_of(t * SLAB, SLAB), SLAB), :]
            for k in range(TOP_K):
                d = dst_ref[0, t * TOP_K + k]
                stage_ref[pl.ds(pl.multiple_of(d * SLAB, SLAB), SLAB), :] = slab
            return carry

        lax.fori_loop(0, MOE_BLOCK, dispatch, 0)

    @pl.when(jnp.logical_and(step >= 1, step <= N_EXPERTS))
    def _():
        e = step - 1
        n_rows = cnt_ref[blk * N_EXPERTS + e]
        first = off_ref[blk * N_EXPERTS + e]
        row = lax.broadcasted_iota(I32, (MOE_CHUNK, LANES), 0)

        def chunk(c, carry):
            base = (first + c * MOE_CHUNK) * SLAB
            words = _load_slabs(stage_ref, base, MOE_CHUNK)
            halves = [_unpack_words(w) for w in words]
            xs = jnp.concatenate([lo for lo, _ in halves] + [hi for _, hi in halves], axis=-1)
            gu = jnp.dot(xs.astype(BF16), wgu_ref[...], preferred_element_type=F32)
            g = gu[:, :EXPERT_HIDDEN]
            act = (g * _sigmoid(g)) * gu[:, EXPERT_HIDDEN:]
            out = _pack_rows(jnp.dot(act.astype(BF16), wd_ref[...], preferred_element_type=F32))
            live = row < (n_rows - c * MOE_CHUNK)
            for s in range(SLAB):
                merged = jnp.where(live, out[:, s * LANES:(s + 1) * LANES], words[s])
                stage_ref[pl.ds(base + s, MOE_CHUNK, stride=SLAB), :] = merged
            return carry

        lax.fori_loop(0, (n_rows + MOE_CHUNK - 1) // MOE_CHUNK, chunk, 0)

    @pl.when(step == N_EXPERTS + 1)
    def _():
        def combine(t, carry):
            acc_lo = jnp.zeros((SLAB, LANES), F32)
            acc_hi = jnp.zeros((SLAB, LANES), F32)
            for k in range(TOP_K):
                d = dst_ref[0, t * TOP_K + k]
                w = w_ref[0, t * TOP_K + k]
                lo, hi = _unpack_words(stage_ref[pl.ds(pl.multiple_of(d * SLAB, SLAB), SLAB), :])
                acc_lo = acc_lo + lo * w
                acc_hi = acc_hi + hi * w
            o_ref[pl.ds(pl.multiple_of(t * SLAB, SLAB), SLAB), :] = pltpu.pack_elementwise(
                [acc_lo, acc_hi], packed_dtype=BF16)
            return carry

        lax.fori_loop(0, MOE_BLOCK, combine, 0)


def _moe(cnt, off, dst, w, h2s, wgu, wd):
    expert = lambda b, s, *_: (jnp.clip(s - 1, 0, N_EXPERTS - 1), 0, 0)
    smem_spec = pl.BlockSpec((None, 1, TOP_K * MOE_BLOCK), lambda b, s, *_: (b, 0, 0), memory_space=pltpu.SMEM)
    return pl.pallas_call(
        _moe_kernel,
        out_shape=jax.ShapeDtypeStruct((N_TOK * SLAB, LANES), U32),
        grid_spec=pltpu.PrefetchScalarGridSpec(
            num_scalar_prefetch=2,
            grid=(MOE_BLOCKS, N_EXPERTS + 2),
            in_specs=[smem_spec, smem_spec,
                      pl.BlockSpec((MOE_BLOCK * SLAB, LANES), lambda b, s, *_: (b, 0),
                                   pipeline_mode=pl.Buffered(1)),
                      pl.BlockSpec((None, D_MODEL, 2 * EXPERT_HIDDEN), expert),
                      pl.BlockSpec((None, EXPERT_HIDDEN, D_MODEL), expert)],
            out_specs=pl.BlockSpec((MOE_BLOCK * SLAB, LANES), lambda b, s, *_: (b, 0)),
            scratch_shapes=[pltpu.VMEM((STAGE_ROWS * SLAB, LANES), U32)]),
        compiler_params=pltpu.CompilerParams(
            dimension_semantics=("arbitrary", "arbitrary"), vmem_limit_bytes=VMEM_LIMIT),
        name="moe",
    )(cnt, off, dst, w, h2s, wgu, wd)


def _fin_kernel(x1s_ref, r_ref, gt2_ref, fg_ref, o_ref):
    halves = [_unpack_words(w) for w in _load_slabs(r_ref, 0, PROJ_ROWS)]
    routed = jnp.concatenate([lo for lo, _ in halves] + [hi for _, hi in halves], axis=-1)
    x2 = x1s_ref[...] + gt2_ref[...] * routed
    ms = jnp.mean(x2 * x2, axis=-1, keepdims=True)
    o_ref[...] = x2 * lax.rsqrt(ms + EPS) * fg_ref[...]


def _fin(x1s, routed, gt2, fg):
    ts = PROJ_ROWS
    tiles_per_seq = SEQ // ts
    return pl.pallas_call(
        _fin_kernel,
        out_shape=jax.ShapeDtypeStruct((N_TOK, D_MODEL), F32),
        grid=(N_TOK // ts,),
        in_specs=[pl.BlockSpec((ts, D_MODEL), lambda i: (i, 0)),
                  pl.BlockSpec((ts * SLAB, LANES), lambda i: (i, 0)),
                  pl.BlockSpec((None, 1, D_MODEL), lambda i: (i // tiles_per_seq, 0, 0)),
                  pl.BlockSpec((1, D_MODEL), lambda i: (0, 0))],
        out_specs=pl.BlockSpec((ts, D_MODEL), lambda i: (i, 0)),
        compiler_params=pltpu.CompilerParams(
            dimension_semantics=("parallel",), vmem_limit_bytes=VMEM_LIMIT),
        name="fin",
    )(x1s, routed, gt2, fg)


def kernel(x, c, w_ada, b_ada, norm1_g, w_in, s5_lambda_re, s5_lambda_im, s5_log_dt, s5_b_re, s5_b_im, s5_c_re, s5_c_im, s5_d, s5_glu_wv, s5_glu_wg, conv_dw_w, conv_dw_b, conv_ln_g, conv_ln_b, conv_pw_w, w_out, norm2_g, router_w, router_bias, exp_w_gate, exp_w_up, exp_w_down, shared_w_gate, shared_w_up, shared_w_down, final_norm_g):
    l = 0
    row = lambda a: a.reshape(1, -1)

    mod = _ada(c, w_ada[l], b_ada[l])
    sh1, sc1, gt1, sh2, sc2, gt2 = [m.reshape(BATCH, 1, D_MODEL) for m in jnp.split(mod, N_MOD, axis=-1)]

    abar_r, abar_i, bbar_r, bbar_i = _s5prep(s5_lambda_re[l], s5_lambda_im[l], s5_log_dt[l],
                                             s5_b_re[l], s5_b_im[l])
    per_group = lambda a: a.reshape(S5_GROUPS, S5_GROUP, S5_STATE)
    a_r = per_group(abar_r)[:, 0, :].reshape(1, N_STATE)
    a_i = per_group(abar_i)[:, 0, :].reshape(1, N_STATE)
    bmat = jnp.concatenate([_block_diag(per_group(bbar_r)), _block_diag(per_group(bbar_i))],
                           axis=1).astype(BF16)
    c_r = _block_diag(jnp.transpose(s5_c_re[l], (0, 2, 1))).astype(BF16)
    c_i = _block_diag(jnp.transpose(s5_c_im[l], (0, 2, 1))).astype(BF16)

    u2, zc2, ga2, gb2 = _proj(x, sh1, sc1, row(norm1_g[l]), w_in[l])
    tb = lambda a, cdim: a.reshape(SEQ * BATCH, cdim)
    mixed = _mix(tb(u2, S5_WIDTH), tb(zc2, CONV_WIDTH), tb(ga2, D_MODEL), tb(gb2, D_MODEL),
                 bmat, c_r, c_i, a_r, a_i, row(s5_d[l]),
                 s5_glu_wv[l].astype(BF16), s5_glu_wg[l].astype(BF16),
                 conv_dw_w[l], row(conv_dw_b[l]), row(conv_ln_g[l]), row(conv_ln_b[l]),
                 conv_pw_w[l].astype(BF16), w_out[l].astype(BF16))
    mixed = mixed.reshape(SEQ, BATCH * D_MODEL)

    pad = LANES - N_EXPERTS
    rw_pad = jnp.pad(router_w[l], ((0, 0), (0, pad))).astype(BF16)
    rb_pad = jnp.pad(row(router_bias[l]), ((0, 0), (0, pad)))
    sgu = jnp.concatenate([shared_w_gate[l], shared_w_up[l]], axis=1).astype(BF16)
    x1s, h2s, e_k, r_k, w_k, cnt = _post(x, mixed, gt1, sh2, sc2, gt2, row(norm2_g[l]), rw_pad, rb_pad,
                                         sgu, shared_w_down[l].astype(BF16))

    cnt = cnt[:, 0, :N_EXPERTS].astype(I32)
    off = jnp.cumsum(cnt, axis=1) - cnt
    e_k = e_k[:, :TOP_K].reshape(MOE_BLOCKS, MOE_BLOCK * TOP_K)
    r_k = r_k[:, :TOP_K].reshape(MOE_BLOCKS, MOE_BLOCK * TOP_K)
    dst = (jnp.take_along_axis(off, e_k, axis=1) + r_k).reshape(MOE_BLOCKS, 1, MOE_BLOCK * TOP_K)
    w_k = w_k[:, :TOP_K].reshape(MOE_BLOCKS, 1, MOE_BLOCK * TOP_K)

    wgu = jnp.concatenate([exp_w_gate[l], exp_w_up[l]], axis=2).astype(BF16)
    routed = _moe(cnt.reshape(-1), off.reshape(-1), dst, w_k, h2s, wgu, exp_w_down[l].astype(BF16))
    out = _fin(x1s, routed, gt2, row(final_norm_g))
    return out.reshape(BATCH, SEQ, D_MODEL)
```

```python
import functools

import jax
import jax.numpy as jnp
from jax import lax
from jax.experimental import pallas as pl
from jax.experimental.pallas import tpu as pltpu

D_MODEL = 1024
BATCH = 8
SEQ = 4096
N_TOK = BATCH * SEQ
S5_WIDTH = 512
S5_GROUP = 16
S5_GROUPS = 32
S5_STATE = 64
N_STATE = S5_GROUPS * S5_STATE
CONV_WIDTH = 512
CONV_KERNEL = 31
N_EXPERTS = 64
TOP_K = 8
EXPERT_HIDDEN = 256
SHARED_HIDDEN = 256
ROUTED_SCALE = 2.5
N_MOD = 6
EPS = 1e-6

LANES = 128
SUBLANES = 8
VMEM_LIMIT = 56 * 1024 * 1024

TILE_STEPS = 64
TILE_ROWS = TILE_STEPS * BATCH
MIX_STEPS = TILE_STEPS
MIX_ROWS = TILE_ROWS
CONV_HALO = (CONV_KERNEL - 1) * BATCH
CONV_CHUNK = 64
SCAN_LANES = 512
MOE_BLOCK = 2048
MOE_BLOCKS = N_TOK // MOE_BLOCK
MOE_CHUNK = 320
SLAB = 4
STAGE_ROWS = TOP_K * MOE_BLOCK + MOE_CHUNK

F32 = jnp.float32
BF16 = jnp.bfloat16
U32 = jnp.uint32
I32 = jnp.int32


_X_SPEC = pl.BlockSpec((BATCH, TILE_STEPS, D_MODEL), lambda i: (0, i, 0))


def _time_major(a):
    return jnp.swapaxes(a, 0, 1).reshape(a.shape[0] * a.shape[1], a.shape[2])


def _batch_major(a):
    return jnp.swapaxes(a.reshape(a.shape[0] // BATCH, BATCH, a.shape[1]), 0, 1)


def _per_batch(rows, mod, op):
    r3 = rows.reshape(rows.shape[0] // BATCH, BATCH, rows.shape[1])
    return op(r3, mod).reshape(rows.shape)


def _pack_rows(x):
    half = D_MODEL // 2
    return pltpu.pack_elementwise([x[:, :half], x[:, half:]], packed_dtype=BF16)


def _unpack_words(w):
    lo = pltpu.unpack_elementwise(w, index=0, packed_dtype=BF16, unpacked_dtype=F32)
    hi = pltpu.unpack_elementwise(w, index=1, packed_dtype=BF16, unpacked_dtype=F32)
    return lo, hi


def _store_slabs(ref, base, words):
    rows = words.shape[0]
    for s in range(SLAB):
        ref[pl.ds(base + s, rows, stride=SLAB), :] = words[:, s * LANES:(s + 1) * LANES]


def _load_slabs(ref, base, rows):
    return [ref[pl.ds(base + s, rows, stride=SLAB), :] for s in range(SLAB)]


def _bdot(a, b):
    return jnp.dot(a.astype(BF16), b.astype(BF16), preferred_element_type=F32)


def _sigmoid(x):
    return jax.nn.sigmoid(x)


def _const_spec(shape):
    zeros = (0,) * len(shape)
    return pl.BlockSpec(shape, lambda *_: zeros, pipeline_mode=pl.Buffered(1))


def _ada_kernel(c_ref, w_ref, b_ref, o_ref):
    c = c_ref[...]
    c_act = c * _sigmoid(c)
    o_ref[...] = jnp.dot(c_act, w_ref[...], preferred_element_type=F32,
                         precision=lax.Precision.HIGHEST) + b_ref[...]


def _ada(c, w_ada, b_ada):
    n_out = N_MOD * D_MODEL
    blk = 1536
    return pl.pallas_call(
        _ada_kernel,
        out_shape=jax.ShapeDtypeStruct((BATCH, n_out), F32),
        grid=(n_out // blk,),
        in_specs=[pl.BlockSpec((BATCH, D_MODEL), lambda j: (0, 0)),
                  pl.BlockSpec((D_MODEL, blk), lambda j: (0, j)),
                  pl.BlockSpec((1, blk), lambda j: (0, j))],
        out_specs=pl.BlockSpec((BATCH, blk), lambda j: (0, j)),
        compiler_params=pltpu.CompilerParams(vmem_limit_bytes=VMEM_LIMIT),
        name="ada",
    )(c, w_ada, b_ada.reshape(1, n_out))


def _s5prep_kernel(lr_ref, li_ref, ldt_ref, br_ref, bi_ref, ar_ref, ai_ref, bbr_ref, bbi_ref):
    lr = lr_ref[...]
    li = li_ref[...]
    dt = jnp.exp(ldt_ref[...])
    mag = jnp.exp(lr * dt)
    abar_r = mag * jnp.cos(li * dt)
    abar_i = mag * jnp.sin(li * dt)
    den = lr * lr + li * li
    nr = abar_r - 1.0
    ni = abar_i
    k_r = (nr * lr + ni * li) / den
    k_i = (ni * lr - nr * li) / den
    br = br_ref[...]
    bi = bi_ref[...]
    ar_ref[...] = abar_r
    ai_ref[...] = abar_i
    bbr_ref[...] = k_r * br - k_i * bi
    bbi_ref[...] = k_r * bi + k_i * br


def _s5prep(lam_re, lam_im, log_dt, b_re, b_im):
    rows = S5_GROUPS * S5_GROUP
    rep = lambda a: jnp.broadcast_to(a[:, None, :], (S5_GROUPS, S5_GROUP, S5_STATE)).reshape(rows, S5_STATE)
    lr = rep(lam_re)
    li = rep(lam_im)
    ldt = rep(jnp.broadcast_to(log_dt[:, None], (S5_GROUPS, S5_STATE)))
    br = jnp.transpose(b_re, (0, 2, 1)).reshape(rows, S5_STATE)
    bi = jnp.transpose(b_im, (0, 2, 1)).reshape(rows, S5_STATE)
    shp = jax.ShapeDtypeStruct((rows, S5_STATE), F32)
    return pl.pallas_call(_s5prep_kernel, out_shape=(shp, shp, shp, shp), name="s5prep")(lr, li, ldt, br, bi)


def _block_diag(blocks):
    g, r, c = blocks.shape
    eye = jnp.eye(g, dtype=blocks.dtype)
    return jnp.einsum("grc,gk->grkc", blocks, eye).reshape(g * r, g * c)


def _proj_kernel(x_ref, sh_ref, sc_ref, g_ref, wu_ref, wv_ref, wg_ref, wa_ref, wb_ref,
                 u_ref, zc_ref, ga_ref, gb_ref):
    x = x_ref[...]
    ms = jnp.mean(x * x, axis=-1, keepdims=True)
    h = x * lax.rsqrt(ms + EPS) * g_ref[...]
    h = h * (1.0 + sc_ref[...]) + sh_ref[...]
    h = _time_major(h).astype(BF16)
    u_ref[...] = jnp.dot(h, wu_ref[...], preferred_element_type=F32).astype(BF16)
    v = jnp.dot(h, wv_ref[...], preferred_element_type=F32)
    g = jnp.dot(h, wg_ref[...], preferred_element_type=F32)
    zc_ref[...] = (v * _sigmoid(g)).astype(BF16)
    ga_ref[...] = _sigmoid(jnp.dot(h, wa_ref[...], preferred_element_type=F32)).astype(BF16)
    gb_ref[...] = _sigmoid(jnp.dot(h, wb_ref[...], preferred_element_type=F32)).astype(BF16)


def _proj(x, sh1, sc1, g1, w_in):
    o1 = S5_WIDTH
    o2 = o1 + CONV_WIDTH
    o3 = o2 + CONV_WIDTH
    o4 = o3 + D_MODEL
    w = w_in.astype(BF16)
    ws = (w[:, :o1], w[:, o1:o2], w[:, o2:o3], w[:, o3:o4], w[:, o4:])
    row_spec = lambda c: pl.BlockSpec((TILE_ROWS, c), lambda i: (i, 0))
    widths = (S5_WIDTH, CONV_WIDTH, D_MODEL, D_MODEL)
    return pl.pallas_call(
        _proj_kernel,
        out_shape=tuple(jax.ShapeDtypeStruct((N_TOK, c), BF16) for c in widths),
        grid=(SEQ // TILE_STEPS,),
        in_specs=[_X_SPEC, _const_spec(sh1.shape), _const_spec(sc1.shape),
                  _const_spec((1, D_MODEL))] + [_const_spec(wi.shape) for wi in ws],
        out_specs=tuple(row_spec(c) for c in widths),
        compiler_params=pltpu.CompilerParams(
            dimension_semantics=("parallel",), vmem_limit_bytes=VMEM_LIMIT),
        name="proj",
    )(x, sh1, sc1, g1, *ws)


def _gelu_tanh(x):
    sqrt_2_over_pi = 0.7978845608028654
    cdf = 0.5 * (1.0 + jnp.tanh(sqrt_2_over_pi * (x + 0.044715 * (x * x * x))))
    return x * cdf


def _mix_kernel(u_ref, zc_ref, ga_ref, gb_ref, bmat_ref, cr_ref, ci_ref, ar_ref, ai_ref, d_ref,
                wv_ref, wg_ref, dww_ref, dwb_ref, lng_ref, lnb_ref, pw_ref, wout_ref,
                out_ref, bu_ref, sr_ref, si_ref, cbuf_ref, cv_ref):
    step = pl.program_id(0)

    @pl.when(step == 0)
    def _():
        sr_ref[...] = jnp.zeros_like(sr_ref)
        si_ref[...] = jnp.zeros_like(si_ref)
        cbuf_ref[0:CONV_HALO, :] = jnp.zeros((CONV_HALO, CONV_WIDTH), F32)

    u = u_ref[...]
    bu_ref[...] = jnp.dot(u, bmat_ref[...], preferred_element_type=F32)

    for lg in range(N_STATE // SCAN_LANES):
        lo = lg * SCAN_LANES
        re = slice(lo, lo + SCAN_LANES)
        im = slice(N_STATE + lo, N_STATE + lo + SCAN_LANES)
        a_r = jnp.broadcast_to(ar_ref[:, re], (SUBLANES, SCAN_LANES))
        a_i = jnp.broadcast_to(ai_ref[:, re], (SUBLANES, SCAN_LANES))

        def body(t, carry, re=re, im=im, a_r=a_r, a_i=a_i):
            s_r, s_i = carry
            rows = pl.ds(pl.multiple_of(t * SUBLANES, SUBLANES), SUBLANES)
            n_r = a_r * s_r - a_i * s_i + bu_ref[rows, re]
            n_i = a_r * s_i + a_i * s_r + bu_ref[rows, im]
            bu_ref[rows, re] = n_r
            bu_ref[rows, im] = n_i
            return n_r, n_i

        s_r, s_i = lax.fori_loop(0, MIX_STEPS, body, (sr_ref[:, re], si_ref[:, re]))
        sr_ref[:, re] = s_r
        si_ref[:, re] = s_i

    y = (jnp.dot(bu_ref[:, 0:N_STATE].astype(BF16), cr_ref[...], preferred_element_type=F32)
         - jnp.dot(bu_ref[:, N_STATE:2 * N_STATE].astype(BF16), ci_ref[...], preferred_element_type=F32)
         + d_ref[...] * u.astype(F32))
    z = _gelu_tanh(y).astype(BF16)
    y_a = (jnp.dot(z, wv_ref[...], preferred_element_type=F32)
           * _sigmoid(jnp.dot(z, wg_ref[...], preferred_element_type=F32)))

    cbuf_ref[CONV_HALO:CONV_HALO + MIX_ROWS, :] = zc_ref[...].astype(F32)
    bias = dwb_ref[...]

    def conv_chunk(ci, carry):
        r0 = pl.multiple_of(ci * CONV_CHUNK, CONV_CHUNK)
        acc = jnp.broadcast_to(bias, (CONV_CHUNK, CONV_WIDTH))
        for k in range(CONV_KERNEL):
            acc = acc + dww_ref[k:k + 1, :] * cbuf_ref[pl.ds(r0 + k * BATCH, CONV_CHUNK), :]
        cv_ref[pl.ds(r0, CONV_CHUNK), :] = acc
        return carry

    lax.fori_loop(0, MIX_ROWS // CONV_CHUNK, conv_chunk, 0)
    cbuf_ref[0:CONV_HALO, :] = cbuf_ref[MIX_ROWS:MIX_ROWS + CONV_HALO, :]

    cv = cv_ref[...]
    mu = jnp.mean(cv, axis=-1, keepdims=True)
    var = jnp.mean(jnp.square(cv - mu), axis=-1, keepdims=True)
    ln = (cv - mu) * lax.rsqrt(var + EPS) * lng_ref[...] + lnb_ref[...]
    zb = (ln * _sigmoid(ln)).astype(BF16)
    y_b = jnp.dot(zb, pw_ref[...], preferred_element_type=F32)

    m = ga_ref[...].astype(F32) * y_a + gb_ref[...].astype(F32) * y_b
    out_ref[...] = jnp.dot(m.astype(BF16), wout_ref[...], preferred_element_type=F32).astype(BF16)


def _mix(u2, zc2, ga2, gb2, bmat, c_r, c_i, a_r, a_i, d_row, wv, wg, dww, dwb, lng, lnb, pw, wout):
    rows = MIX_ROWS
    row_spec = lambda c: pl.BlockSpec((rows, c), lambda i: (i, 0))
    consts = (bmat, c_r, c_i, a_r, a_i, d_row, wv, wg, dww, dwb, lng, lnb, pw, wout)
    return pl.pallas_call(
        _mix_kernel,
        out_shape=jax.ShapeDtypeStruct((SEQ * BATCH, D_MODEL), BF16),
        grid=(SEQ // MIX_STEPS,),
        in_specs=[row_spec(S5_WIDTH), row_spec(CONV_WIDTH), row_spec(D_MODEL), row_spec(D_MODEL)]
                 + [_const_spec(a.shape) for a in consts],
        out_specs=row_spec(D_MODEL),
        scratch_shapes=[pltpu.VMEM((rows, 2 * N_STATE), F32),
                        pltpu.VMEM((SUBLANES, N_STATE), F32),
                        pltpu.VMEM((SUBLANES, N_STATE), F32),
                        pltpu.VMEM((CONV_HALO + rows, CONV_WIDTH), F32),
                        pltpu.VMEM((rows, CONV_WIDTH), F32)],
        compiler_params=pltpu.CompilerParams(
            dimension_semantics=("arbitrary",), vmem_limit_bytes=VMEM_LIMIT),
        name="mix",
    )(u2, zc2, ga2, gb2, *consts)


def _post_kernel(x_ref, mx_ref, gt1_ref, sh_ref, sc_ref, gt2_ref, g_ref, rw_ref, rb_ref,
                 sgu_ref, sd_ref, x1s_ref, h2_ref, kmap_ref, wc_ref, cnt_ref):
    x1 = _time_major(x_ref[...]) + _per_batch(mx_ref[...].astype(F32), gt1_ref[...], lambda r, m: r * m)
    ms = jnp.mean(x1 * x1, axis=-1, keepdims=True)
    h = x1 * lax.rsqrt(ms + EPS) * g_ref[...]
    h = _per_batch(h, (sc_ref[...], sh_ref[...]), lambda r, m: r * (1.0 + m[0]) + m[1])
    hb = h.astype(BF16)
    _store_slabs(h2_ref, 0, _pack_rows(hb.astype(F32)))

    scores = _sigmoid(jnp.dot(hb, rw_ref[...], preferred_element_type=F32))
    lane = lax.broadcasted_iota(I32, scores.shape, 1)
    neg_inf = jnp.float32(-jnp.inf)
    work = jnp.where(lane < N_EXPERTS, scores + rb_ref[...], neg_inf)
    kmap = jnp.zeros(scores.shape, I32)
    for k in range(TOP_K):
        m = jnp.max(work, axis=-1, keepdims=True)
        idx = jnp.min(jnp.where(work == m, lane, LANES), axis=-1, keepdims=True)
        hit = lane == idx
        kmap = jnp.where(hit, k + 1, kmap)
        work = jnp.where(hit, neg_inf, work)
    sel = kmap > 0
    w_sel = jnp.where(sel, scores, 0.0)
    kmap_ref[...] = kmap
    wc_ref[...] = w_sel / jnp.sum(w_sel, axis=-1, keepdims=True) * ROUTED_SCALE

    @pl.when(pl.program_id(0) % (MOE_BLOCK // TILE_ROWS) == 0)
    def _():
        cnt_ref[...] = jnp.zeros_like(cnt_ref)

    cnt_ref[...] += jnp.sum(sel.astype(F32), axis=0, keepdims=True)

    gu = jnp.dot(hb, sgu_ref[...], preferred_element_type=F32)
    g = gu[:, :SHARED_HIDDEN]
    act = (g * _sigmoid(g)) * gu[:, SHARED_HIDDEN:]
    shared = jnp.dot(act.astype(BF16), sd_ref[...], preferred_element_type=F32)
    x1s_ref[...] = x1 + _per_batch(shared, gt2_ref[...], lambda r, m: r * m)


def _post(x, mixed, gt1, sh2, sc2, gt2, g2, rw_pad, rb_pad, sgu, sd):
    ts = TILE_ROWS
    tiles_per_block = MOE_BLOCK // ts
    mod_spec = _const_spec((1, BATCH, D_MODEL))
    row_spec = lambda rows, c: pl.BlockSpec((rows, c), lambda i: (i, 0))
    return pl.pallas_call(
        _post_kernel,
        out_shape=(jax.ShapeDtypeStruct((N_TOK, D_MODEL), F32),
                   jax.ShapeDtypeStruct((N_TOK * SLAB, LANES), U32),
                   jax.ShapeDtypeStruct((N_TOK, LANES), I32),
                   jax.ShapeDtypeStruct((N_TOK, LANES), F32),
                   jax.ShapeDtypeStruct((MOE_BLOCKS, 1, LANES), F32)),
        grid=(SEQ // TILE_STEPS,),
        in_specs=[_X_SPEC, row_spec(ts, D_MODEL),
                  mod_spec, mod_spec, mod_spec, mod_spec,
                  _const_spec((1, D_MODEL)), _const_spec(rw_pad.shape), _const_spec(rb_pad.shape),
                  _const_spec(sgu.shape), _const_spec(sd.shape)],
        out_specs=(row_spec(ts, D_MODEL), row_spec(ts * SLAB, LANES),
                   row_spec(ts, LANES), row_spec(ts, LANES),
                   pl.BlockSpec((None, 1, LANES), lambda i: (i // tiles_per_block, 0, 0))),
        compiler_params=pltpu.CompilerParams(
            dimension_semantics=("arbitrary",), vmem_limit_bytes=VMEM_LIMIT),
        name="post",
    )(x, mixed, gt1, sh2, sc2, gt2, g2, rw_pad, rb_pad, sgu, sd)


def _slots_kernel(kmap_ref, wc_ref, off_ref, tri_ref, dst_ref, wk_ref, seen_ref):
    @pl.when(pl.program_id(0) % (MOE_BLOCK // TILE_ROWS) == 0)
    def _():
        seen_ref[...] = jnp.zeros_like(seen_ref)

    kmap = kmap_ref[...]
    wc = wc_ref[...]
    sel = (kmap > 0).astype(F32)
    seen = seen_ref[...]
    slot = jnp.dot(tri_ref[...], sel.astype(BF16), preferred_element_type=F32) + (seen + off_ref[...])
    seen_ref[...] = seen + jnp.sum(sel, axis=0, keepdims=True)
    lane = lax.broadcasted_iota(I32, kmap.shape, 1)
    dst = jnp.zeros(kmap.shape, F32)
    w_k = jnp.zeros(kmap.shape, F32)
    for k in range(TOP_K):
        hit = kmap == k + 1
        dst = jnp.where(lane == k, jnp.sum(jnp.where(hit, slot, 0.0), axis=-1, keepdims=True), dst)
        w_k = jnp.where(lane == k, jnp.sum(jnp.where(hit, wc, 0.0), axis=-1, keepdims=True), w_k)
    dst_ref[...] = dst.astype(I32)
    wk_ref[...] = w_k


def _slots(kmap, wc, off):
    ts = TILE_ROWS
    tiles_per_block = MOE_BLOCK // ts
    row_spec = pl.BlockSpec((ts, LANES), lambda i: (i, 0))
    tri = (lax.broadcasted_iota(I32, (ts, ts), 0) > lax.broadcasted_iota(I32, (ts, ts), 1)).astype(BF16)
    return pl.pallas_call(
        _slots_kernel,
        out_shape=(jax.ShapeDtypeStruct((N_TOK, LANES), I32), jax.ShapeDtypeStruct((N_TOK, LANES), F32)),
        grid=(N_TOK // ts,),
        in_specs=[row_spec, row_spec,
                  pl.BlockSpec((None, 1, LANES), lambda i: (i // tiles_per_block, 0, 0)),
                  _const_spec(tri.shape)],
        out_specs=(row_spec, row_spec),
        scratch_shapes=[pltpu.VMEM((1, LANES), F32)],
        compiler_params=pltpu.CompilerParams(
            dimension_semantics=("arbitrary",), vmem_limit_bytes=VMEM_LIMIT),
        name="slots",
    )(kmap, wc, off, tri)


def _moe_kernel(cnt_ref, off_ref, dst_ref, w_ref, h_ref, wgu_ref, wd_ref, o_ref, stage_ref):
    blk = pl.program_id(0)
    step = pl.program_id(1)

    @pl.when(jnp.logical_and(blk == 0, step == 0))
    def _():
        stage_ref[TOP_K * MOE_BLOCK * SLAB:, :] = jnp.zeros((MOE_CHUNK * SLAB, LANES), U32)

    @pl.when(step == 0)
    def _():
        def dispatch(t, carry):
            slab = h_ref[pl.ds(pl.multiple_of(t * SLAB, SLAB), SLAB), :]
            for k in range(TOP_K):
                d = dst_ref[0, t * TOP_K + k]
                stage_ref[pl.ds(pl.multiple_of(d * SLAB, SLAB), SLAB), :] = slab
            return carry

        lax.fori_loop(0, MOE_BLOCK, dispatch, 0)

    @pl.when(jnp.logical_and(step >= 1, step <= N_EXPERTS))
    def _():
        e = step - 1
        n_rows = cnt_ref[blk * N_EXPERTS + e]
        first = off_ref[blk * N_EXPERTS + e]
        row = lax.broadcasted_iota(I32, (MOE_CHUNK, LANES), 0)

        def chunk(c, carry):
            base = (first + c * MOE_CHUNK) * SLAB
            words = _load_slabs(stage_ref, base, MOE_CHUNK)
            halves = [_unpack_words(w) for w in words]
            xs = jnp.concatenate([lo for lo, _ in halves] + [hi for _, hi in halves], axis=-1)
            gu = jnp.dot(xs.astype(BF16), wgu_ref[...], preferred_element_type=F32)
            g = gu[:, :EXPERT_HIDDEN]
            act = (g * _sigmoid(g)) * gu[:, EXPERT_HIDDEN:]
            out = _pack_rows(jnp.dot(act.astype(BF16), wd_ref[...], preferred_element_type=F32))
            live = row < (n_rows - c * MOE_CHUNK)
            for s in range(SLAB):
                merged = jnp.where(live, out[:, s * LANES:(s + 1) * LANES], words[s])
                stage_ref[pl.ds(base + s, MOE_CHUNK, stride=SLAB), :] = merged
            return carry

        lax.fori_loop(0, (n_rows + MOE_CHUNK - 1) // MOE_CHUNK, chunk, 0)

    @pl.when(step == N_EXPERTS + 1)
    def _():
        def combine(t, carry):
            acc_lo = jnp.zeros((SLAB, LANES), F32)
            acc_hi = jnp.zeros((SLAB, LANES), F32)
            for k in range(TOP_K):
                d = dst_ref[0, t * TOP_K + k]
                w = w_ref[0, t * TOP_K + k]
                lo, hi = _unpack_words(stage_ref[pl.ds(pl.multiple_of(d * SLAB, SLAB), SLAB), :])
                acc_lo = acc_lo + lo * w
                acc_hi = acc_hi + hi * w
            o_ref[pl.ds(pl.multiple_of(t * SLAB, SLAB), SLAB), :] = pltpu.pack_elementwise(
                [acc_lo, acc_hi], packed_dtype=BF16)
            return carry

        lax.fori_loop(0, MOE_BLOCK, combine, 0)


def _moe(cnt, off, dst, w, h2s, wgu, wd):
    expert = lambda b, s, *_: (jnp.clip(s - 1, 0, N_EXPERTS - 1), 0, 0)
    smem_spec = pl.BlockSpec((None, 1, TOP_K * MOE_BLOCK), lambda b, s, *_: (b, 0, 0), memory_space=pltpu.SMEM)
    return pl.pallas_call(
        _moe_kernel,
        out_shape=jax.ShapeDtypeStruct((N_TOK * SLAB, LANES), U32),
        grid_spec=pltpu.PrefetchScalarGridSpec(
            num_scalar_prefetch=2,
            grid=(MOE_BLOCKS, N_EXPERTS + 2),
            in_specs=[smem_spec, smem_spec,
                      pl.BlockSpec((MOE_BLOCK * SLAB, LANES), lambda b, s, *_: (b, 0),
                                   pipeline_mode=pl.Buffered(1)),
                      pl.BlockSpec((None, D_MODEL, 2 * EXPERT_HIDDEN), expert),
                      pl.BlockSpec((None, EXPERT_HIDDEN, D_MODEL), expert)],
            out_specs=pl.BlockSpec((MOE_BLOCK * SLAB, LANES), lambda b, s, *_: (b, 0)),
            scratch_shapes=[pltpu.VMEM((STAGE_ROWS * SLAB, LANES), U32)]),
        compiler_params=pltpu.CompilerParams(
            dimension_semantics=("arbitrary", "arbitrary"), vmem_limit_bytes=VMEM_LIMIT),
        name="moe",
    )(cnt, off, dst, w, h2s, wgu, wd)


def _fin_kernel(x1s_ref, r_ref, gt2_ref, fg_ref, o_ref):
    halves = [_unpack_words(w) for w in _load_slabs(r_ref, 0, TILE_ROWS)]
    routed = jnp.concatenate([lo for lo, _ in halves] + [hi for _, hi in halves], axis=-1)
    x2 = x1s_ref[...] + _per_batch(routed, gt2_ref[...], lambda r, m: r * m)
    ms = jnp.mean(x2 * x2, axis=-1, keepdims=True)
    o_ref[...] = _batch_major(x2 * lax.rsqrt(ms + EPS) * fg_ref[...])


def _fin(x1s, routed, gt2, fg):
    ts = TILE_ROWS
    return pl.pallas_call(
        _fin_kernel,
        out_shape=jax.ShapeDtypeStruct((BATCH, SEQ, D_MODEL), F32),
        grid=(SEQ // TILE_STEPS,),
        in_specs=[pl.BlockSpec((ts, D_MODEL), lambda i: (i, 0)),
                  pl.BlockSpec((ts * SLAB, LANES), lambda i: (i, 0)),
                  _const_spec((1, BATCH, D_MODEL)),
                  _const_spec((1, D_MODEL))],
        out_specs=_X_SPEC,
        compiler_params=pltpu.CompilerParams(
            dimension_semantics=("parallel",), vmem_limit_bytes=VMEM_LIMIT),
        name="fin",
    )(x1s, routed, gt2, fg)


def kernel(x, c, w_ada, b_ada, norm1_g, w_in, s5_lambda_re, s5_lambda_im, s5_log_dt, s5_b_re, s5_b_im, s5_c_re, s5_c_im, s5_d, s5_glu_wv, s5_glu_wg, conv_dw_w, conv_dw_b, conv_ln_g, conv_ln_b, conv_pw_w, w_out, norm2_g, router_w, router_bias, exp_w_gate, exp_w_up, exp_w_down, shared_w_gate, shared_w_up, shared_w_down, final_norm_g):
    l = 0
    row = lambda a: a.reshape(1, -1)

    mod = _ada(c, w_ada[l], b_ada[l])
    sh1, sc1, gt1, sh2, sc2, gt2 = jnp.split(mod, N_MOD, axis=-1)
    sh1, sc1 = [m.reshape(BATCH, 1, D_MODEL) for m in (sh1, sc1)]
    gt1, sh2, sc2, gt2 = [m.reshape(1, BATCH, D_MODEL) for m in (gt1, sh2, sc2, gt2)]

    abar_r, abar_i, bbar_r, bbar_i = _s5prep(s5_lambda_re[l], s5_lambda_im[l], s5_log_dt[l],
                                             s5_b_re[l], s5_b_im[l])
    per_group = lambda a: a.reshape(S5_GROUPS, S5_GROUP, S5_STATE)
    a_r = per_group(abar_r)[:, 0, :].reshape(1, N_STATE)
    a_i = per_group(abar_i)[:, 0, :].reshape(1, N_STATE)
    bmat = jnp.concatenate([_block_diag(per_group(bbar_r)), _block_diag(per_group(bbar_i))],
                           axis=1).astype(BF16)
    c_r = _block_diag(jnp.transpose(s5_c_re[l], (0, 2, 1))).astype(BF16)
    c_i = _block_diag(jnp.transpose(s5_c_im[l], (0, 2, 1))).astype(BF16)

    u2, zc2, ga2, gb2 = _proj(x, sh1, sc1, row(norm1_g[l]), w_in[l])
    mixed = _mix(u2, zc2, ga2, gb2,
                 bmat, c_r, c_i, a_r, a_i, row(s5_d[l]),
                 s5_glu_wv[l].astype(BF16), s5_glu_wg[l].astype(BF16),
                 conv_dw_w[l], row(conv_dw_b[l]), row(conv_ln_g[l]), row(conv_ln_b[l]),
                 conv_pw_w[l].astype(BF16), w_out[l].astype(BF16))

    pad = LANES - N_EXPERTS
    rw_pad = jnp.pad(router_w[l], ((0, 0), (0, pad))).astype(BF16)
    rb_pad = jnp.pad(row(router_bias[l]), ((0, 0), (0, pad)))
    sgu = jnp.concatenate([shared_w_gate[l], shared_w_up[l]], axis=1).astype(BF16)
    x1s, h2s, kmap, wc, cnt = _post(x, mixed, gt1, sh2, sc2, gt2, row(norm2_g[l]), rw_pad, rb_pad,
                                    sgu, shared_w_down[l].astype(BF16))

    off = jnp.cumsum(cnt, axis=-1) - cnt
    dst, w_k = _slots(kmap, wc, off)
    as_ints = lambda a: a[:, 0, :N_EXPERTS].astype(I32).reshape(-1)
    per_block = lambda a: a[:, :TOP_K].reshape(MOE_BLOCKS, 1, MOE_BLOCK * TOP_K)

    wgu = jnp.concatenate([exp_w_gate[l], exp_w_up[l]], axis=2).astype(BF16)
    routed = _moe(as_ints(cnt), as_ints(off), per_block(dst), per_block(w_k), h2s, wgu,
                  exp_w_down[l].astype(BF16))
    return _fin(x1s, routed, gt2, row(final_norm_g))
```

```python
import functools

import jax
import jax.numpy as jnp
from jax import lax
from jax.experimental import pallas as pl
from jax.experimental.pallas import tpu as pltpu

D_MODEL = 1024
BATCH = 8
SEQ = 4096
N_TOK = BATCH * SEQ
S5_WIDTH = 512
S5_GROUP = 16
S5_GROUPS = 32
S5_STATE = 64
N_STATE = S5_GROUPS * S5_STATE
CONV_WIDTH = 512
CONV_KERNEL = 31
N_EXPERTS = 64
TOP_K = 8
EXPERT_HIDDEN = 256
SHARED_HIDDEN = 256
ROUTED_SCALE = 2.5
N_MOD = 6
EPS = 1e-6

LANES = 128
SUBLANES = 8
VMEM_LIMIT = 56 * 1024 * 1024

TILE_STEPS = 64
TILE_ROWS = TILE_STEPS * BATCH
MIX_STEPS = TILE_STEPS
MIX_ROWS = TILE_ROWS
CONV_HALO = (CONV_KERNEL - 1) * BATCH
CONV_CHUNK = 64
SCAN_LANES = 512
MOE_BLOCK = 2048
MOE_BLOCKS = N_TOK // MOE_BLOCK
MOE_CHUNK = 320
MOE_EXPERTS_PER_STEP = 2
MOE_EXPERT_STEPS = N_EXPERTS // MOE_EXPERTS_PER_STEP
MOE_UNROLL = 4
SLAB = 4
STAGE_ROWS = TOP_K * MOE_BLOCK + MOE_CHUNK

F32 = jnp.float32
BF16 = jnp.bfloat16
U32 = jnp.uint32
I32 = jnp.int32


_X_SPEC = pl.BlockSpec((BATCH, TILE_STEPS, D_MODEL), lambda i: (0, i, 0))


def _time_major(a):
    return jnp.swapaxes(a, 0, 1).reshape(a.shape[0] * a.shape[1], a.shape[2])


def _batch_major(a):
    return jnp.swapaxes(a.reshape(a.shape[0] // BATCH, BATCH, a.shape[1]), 0, 1)


def _per_batch(rows, mod, op):
    r3 = rows.reshape(rows.shape[0] // BATCH, BATCH, rows.shape[1])
    return op(r3, mod).reshape(rows.shape)


def _pack_rows(x):
    half = D_MODEL // 2
    return pltpu.pack_elementwise([x[:, :half], x[:, half:]], packed_dtype=BF16)


def _unpack_words(w):
    lo = pltpu.unpack_elementwise(w, index=0, packed_dtype=BF16, unpacked_dtype=F32)
    hi = pltpu.unpack_elementwise(w, index=1, packed_dtype=BF16, unpacked_dtype=F32)
    return lo, hi


def _store_slabs(ref, base, words):
    rows = words.shape[0]
    for s in range(SLAB):
        ref[pl.ds(base + s, rows, stride=SLAB), :] = words[:, s * LANES:(s + 1) * LANES]


def _load_slabs(ref, base, rows):
    return [ref[pl.ds(base + s, rows, stride=SLAB), :] for s in range(SLAB)]


def _bdot(a, b):
    return jnp.dot(a.astype(BF16), b.astype(BF16), preferred_element_type=F32)


def _sigmoid(x):
    return jax.nn.sigmoid(x)


def _const_spec(shape):
    zeros = (0,) * len(shape)
    return pl.BlockSpec(shape, lambda *_: zeros, pipeline_mode=pl.Buffered(1))


def _ada_kernel(c_ref, w_ref, b_ref, o_ref):
    c = c_ref[...]
    c_act = c * _sigmoid(c)
    o_ref[...] = jnp.dot(c_act, w_ref[...], preferred_element_type=F32,
                         precision=lax.Precision.HIGHEST) + b_ref[...]


def _ada(c, w_ada, b_ada):
    n_out = N_MOD * D_MODEL
    blk = 1536
    return pl.pallas_call(
        _ada_kernel,
        out_shape=jax.ShapeDtypeStruct((BATCH, n_out), F32),
        grid=(n_out // blk,),
        in_specs=[pl.BlockSpec((BATCH, D_MODEL), lambda j: (0, 0)),
                  pl.BlockSpec((D_MODEL, blk), lambda j: (0, j)),
                  pl.BlockSpec((1, blk), lambda j: (0, j))],
        out_specs=pl.BlockSpec((BATCH, blk), lambda j: (0, j)),
        compiler_params=pltpu.CompilerParams(vmem_limit_bytes=VMEM_LIMIT),
        name="ada",
    )(c, w_ada, b_ada.reshape(1, n_out))


def _s5prep_kernel(lr_ref, li_ref, ldt_ref, br_ref, bi_ref, ar_ref, ai_ref, bbr_ref, bbi_ref):
    lr = lr_ref[...]
    li = li_ref[...]
    dt = jnp.exp(ldt_ref[...])
    mag = jnp.exp(lr * dt)
    abar_r = mag * jnp.cos(li * dt)
    abar_i = mag * jnp.sin(li * dt)
    den = lr * lr + li * li
    nr = abar_r - 1.0
    ni = abar_i
    k_r = (nr * lr + ni * li) / den
    k_i = (ni * lr - nr * li) / den
    br = br_ref[...]
    bi = bi_ref[...]
    ar_ref[...] = abar_r
    ai_ref[...] = abar_i
    bbr_ref[...] = k_r * br - k_i * bi
    bbi_ref[...] = k_r * bi + k_i * br


def _s5prep(lam_re, lam_im, log_dt, b_re, b_im):
    rows = S5_GROUPS * S5_GROUP
    rep = lambda a: jnp.broadcast_to(a[:, None, :], (S5_GROUPS, S5_GROUP, S5_STATE)).reshape(rows, S5_STATE)
    lr = rep(lam_re)
    li = rep(lam_im)
    ldt = rep(jnp.broadcast_to(log_dt[:, None], (S5_GROUPS, S5_STATE)))
    br = jnp.transpose(b_re, (0, 2, 1)).reshape(rows, S5_STATE)
    bi = jnp.transpose(b_im, (0, 2, 1)).reshape(rows, S5_STATE)
    shp = jax.ShapeDtypeStruct((rows, S5_STATE), F32)
    return pl.pallas_call(_s5prep_kernel, out_shape=(shp, shp, shp, shp), name="s5prep")(lr, li, ldt, br, bi)


def _block_diag(blocks):
    g, r, c = blocks.shape
    eye = jnp.eye(g, dtype=blocks.dtype)
    return jnp.einsum("grc,gk->grkc", blocks, eye).reshape(g * r, g * c)


def _proj_kernel(x_ref, sh_ref, sc_ref, g_ref, wu_ref, wv_ref, wg_ref, wa_ref, wb_ref,
                 u_ref, zc_ref, ga_ref, gb_ref):
    x = x_ref[...]
    ms = jnp.mean(x * x, axis=-1, keepdims=True)
    h = x * lax.rsqrt(ms + EPS) * g_ref[...]
    h = h * (1.0 + sc_ref[...]) + sh_ref[...]
    h = _time_major(h).astype(BF16)
    u_ref[...] = jnp.dot(h, wu_ref[...], preferred_element_type=F32).astype(BF16)
    v = jnp.dot(h, wv_ref[...], preferred_element_type=F32)
    g = jnp.dot(h, wg_ref[...], preferred_element_type=F32)
    zc_ref[...] = (v * _sigmoid(g)).astype(BF16)
    ga_ref[...] = _sigmoid(jnp.dot(h, wa_ref[...], preferred_element_type=F32)).astype(BF16)
    gb_ref[...] = _sigmoid(jnp.dot(h, wb_ref[...], preferred_element_type=F32)).astype(BF16)


def _proj(x, sh1, sc1, g1, w_in):
    o1 = S5_WIDTH
    o2 = o1 + CONV_WIDTH
    o3 = o2 + CONV_WIDTH
    o4 = o3 + D_MODEL
    w = w_in.astype(BF16)
    ws = (w[:, :o1], w[:, o1:o2], w[:, o2:o3], w[:, o3:o4], w[:, o4:])
    row_spec = lambda c: pl.BlockSpec((TILE_ROWS, c), lambda i: (i, 0))
    widths = (S5_WIDTH, CONV_WIDTH, D_MODEL, D_MODEL)
    return pl.pallas_call(
        _proj_kernel,
        out_shape=tuple(jax.ShapeDtypeStruct((N_TOK, c), BF16) for c in widths),
        grid=(SEQ // TILE_STEPS,),
        in_specs=[_X_SPEC, _const_spec(sh1.shape), _const_spec(sc1.shape),
                  _const_spec((1, D_MODEL))] + [_const_spec(wi.shape) for wi in ws],
        out_specs=tuple(row_spec(c) for c in widths),
        compiler_params=pltpu.CompilerParams(
            dimension_semantics=("parallel",), vmem_limit_bytes=VMEM_LIMIT),
        name="proj",
    )(x, sh1, sc1, g1, *ws)


def _gelu_tanh(x):
    sqrt_2_over_pi = 0.7978845608028654
    cdf = 0.5 * (1.0 + jnp.tanh(sqrt_2_over_pi * (x + 0.044715 * (x * x * x))))
    return x * cdf


def _mix_kernel(u_ref, zc_ref, ga_ref, gb_ref, bsp_ref, csr_ref, csi_ref, ar_ref, ai_ref, d_ref,
                wv_ref, wg_ref, dww_ref, dwb_ref, lng_ref, lnb_ref, pw_ref, wout_ref,
                out_ref, bu_ref, sr_ref, si_ref, cbuf_ref, cv_ref):
    step = pl.program_id(0)

    @pl.when(step == 0)
    def _():
        sr_ref[...] = jnp.zeros_like(sr_ref)
        si_ref[...] = jnp.zeros_like(si_ref)
        cbuf_ref[0:CONV_HALO, :] = jnp.zeros((CONV_HALO, CONV_WIDTH), F32)

    u = u_ref[...]
    for j in range(N_STATE // LANES):
        kb = (j * LANES // S5_STATE * S5_GROUP) // LANES
        res = jnp.dot(u[:, kb * LANES:(kb + 1) * LANES], bsp_ref[j], preferred_element_type=F32)
        bu_ref[:, j * LANES:(j + 1) * LANES] = res[:, :LANES]
        bu_ref[:, N_STATE + j * LANES:N_STATE + (j + 1) * LANES] = res[:, LANES:]

    for lg in range(N_STATE // SCAN_LANES):
        lo = lg * SCAN_LANES
        re = slice(lo, lo + SCAN_LANES)
        im = slice(N_STATE + lo, N_STATE + lo + SCAN_LANES)
        a_r = jnp.broadcast_to(ar_ref[:, re], (SUBLANES, SCAN_LANES))
        a_i = jnp.broadcast_to(ai_ref[:, re], (SUBLANES, SCAN_LANES))
        s_r = sr_ref[:, re]
        s_i = si_ref[:, re]
        for t in range(MIX_STEPS):
            rows = slice(t * SUBLANES, (t + 1) * SUBLANES)
            s_r, s_i = (a_r * s_r - a_i * s_i + bu_ref[rows, re],
                        a_r * s_i + a_i * s_r + bu_ref[rows, im])
            bu_ref[rows, re] = s_r
            bu_ref[rows, im] = s_i
        sr_ref[:, re] = s_r
        si_ref[:, re] = s_i

    half_c = S5_WIDTH // 2
    half_s = N_STATE // 2
    ys = []
    for hf in range(2):
        s_re = bu_ref[:, hf * half_s:(hf + 1) * half_s].astype(BF16)
        s_im = bu_ref[:, N_STATE + hf * half_s:N_STATE + (hf + 1) * half_s].astype(BF16)
        ys.append(jnp.dot(s_re, csr_ref[hf], preferred_element_type=F32)
                  - jnp.dot(s_im, csi_ref[hf], preferred_element_type=F32))
    y = jnp.concatenate(ys, axis=-1) + d_ref[...] * u.astype(F32)
    z = _gelu_tanh(y).astype(BF16)
    y_a = (jnp.dot(z, wv_ref[...], preferred_element_type=F32)
           * _sigmoid(jnp.dot(z, wg_ref[...], preferred_element_type=F32)))

    cbuf_ref[CONV_HALO:CONV_HALO + MIX_ROWS, :] = zc_ref[...].astype(F32)
    for ci in range(MIX_ROWS // CONV_CHUNK):
        r0 = ci * CONV_CHUNK
        for lt in range(CONV_WIDTH // LANES):
            ls = slice(lt * LANES, (lt + 1) * LANES)
            win = cbuf_ref[r0:r0 + CONV_CHUNK + CONV_HALO, ls]
            acc = jnp.broadcast_to(dwb_ref[:, ls], (CONV_CHUNK, LANES))
            for k in range(CONV_KERNEL):
                acc = acc + dww_ref[k:k + 1, ls] * win[k * BATCH:k * BATCH + CONV_CHUNK]
            cv_ref[r0:r0 + CONV_CHUNK, ls] = acc
    cbuf_ref[0:CONV_HALO, :] = cbuf_ref[MIX_ROWS:MIX_ROWS + CONV_HALO, :]

    cv = cv_ref[...]
    mu = jnp.mean(cv, axis=-1, keepdims=True)
    var = jnp.mean(jnp.square(cv - mu), axis=-1, keepdims=True)
    ln = (cv - mu) * lax.rsqrt(var + EPS) * lng_ref[...] + lnb_ref[...]
    zb = (ln * _sigmoid(ln)).astype(BF16)
    y_b = jnp.dot(zb, pw_ref[...], preferred_element_type=F32)

    m = ga_ref[...].astype(F32) * y_a + gb_ref[...].astype(F32) * y_b
    out_ref[...] = jnp.dot(m.astype(BF16), wout_ref[...], preferred_element_type=F32).astype(BF16)


def _mix(u2, zc2, ga2, gb2, bmat, c_r, c_i, a_r, a_i, d_row, wv, wg, dww, dwb, lng, lnb, pw, wout):
    rows = MIX_ROWS
    row_spec = lambda c: pl.BlockSpec((rows, c), lambda i: (i, 0))
    tiles = []
    for j in range(N_STATE // LANES):
        kb = (j * LANES // S5_STATE * S5_GROUP) // LANES
        k_rows = slice(kb * LANES, (kb + 1) * LANES)
        tiles.append(jnp.concatenate([bmat[k_rows, j * LANES:(j + 1) * LANES],
                                      bmat[k_rows, N_STATE + j * LANES:N_STATE + (j + 1) * LANES]], axis=1))
    bsp = jnp.stack(tiles)
    half_c, half_s = S5_WIDTH // 2, N_STATE // 2
    halves = lambda m: jnp.stack([m[hf * half_s:(hf + 1) * half_s, hf * half_c:(hf + 1) * half_c]
                                  for hf in range(2)])
    consts = (bsp, halves(c_r), halves(c_i), a_r, a_i, d_row, wv, wg, dww, dwb, lng, lnb, pw, wout)
    return pl.pallas_call(
        _mix_kernel,
        out_shape=jax.ShapeDtypeStruct((SEQ * BATCH, D_MODEL), BF16),
        grid=(SEQ // MIX_STEPS,),
        in_specs=[row_spec(S5_WIDTH), row_spec(CONV_WIDTH), row_spec(D_MODEL), row_spec(D_MODEL)]
                 + [_const_spec(a.shape) for a in consts],
        out_specs=row_spec(D_MODEL),
        scratch_shapes=[pltpu.VMEM((rows, 2 * N_STATE), F32),
                        pltpu.VMEM((SUBLANES, N_STATE), F32),
                        pltpu.VMEM((SUBLANES, N_STATE), F32),
                        pltpu.VMEM((CONV_HALO + rows, CONV_WIDTH), F32),
                        pltpu.VMEM((rows, CONV_WIDTH), F32)],
        compiler_params=pltpu.CompilerParams(
            dimension_semantics=("arbitrary",), vmem_limit_bytes=VMEM_LIMIT),
        name="mix",
    )(u2, zc2, ga2, gb2, *consts)


def _post_kernel(x_ref, mx_ref, gt1_ref, sh_ref, sc_ref, gt2_ref, g_ref, rw_ref, rb_ref,
                 sgu_ref, sd_ref, x1s_ref, h2_ref, kmap_ref, wc_ref, cnt_ref):
    x1 = _time_major(x_ref[...]) + _per_batch(mx_ref[...].astype(F32), gt1_ref[...], lambda r, m: r * m)
    ms = jnp.mean(x1 * x1, axis=-1, keepdims=True)
    h = x1 * lax.rsqrt(ms + EPS) * g_ref[...]
    h = _per_batch(h, (sc_ref[...], sh_ref[...]), lambda r, m: r * (1.0 + m[0]) + m[1])
    hb = h.astype(BF16)
    _store_slabs(h2_ref, 0, _pack_rows(hb.astype(F32)))

    scores = _sigmoid(jnp.dot(hb, rw_ref[...], preferred_element_type=F32))
    lane = lax.broadcasted_iota(I32, scores.shape, 1)
    neg_inf = jnp.float32(-jnp.inf)
    work = jnp.where(lane < N_EXPERTS, scores + rb_ref[...], neg_inf)
    kmap = jnp.zeros(scores.shape, I32)
    for k in range(TOP_K):
        m = jnp.max(work, axis=-1, keepdims=True)
        idx = jnp.min(jnp.where(work == m, lane, LANES), axis=-1, keepdims=True)
        hit = lane == idx
        kmap = jnp.where(hit, k + 1, kmap)
        work = jnp.where(hit, neg_inf, work)
    sel = kmap > 0
    w_sel = jnp.where(sel, scores, 0.0)
    kmap_ref[...] = kmap
    wc_ref[...] = w_sel / jnp.sum(w_sel, axis=-1, keepdims=True) * ROUTED_SCALE

    @pl.when(pl.program_id(0) % (MOE_BLOCK // TILE_ROWS) == 0)
    def _():
        cnt_ref[...] = jnp.zeros_like(cnt_ref)

    cnt_ref[...] += jnp.sum(sel.astype(F32), axis=0, keepdims=True)

    gu = jnp.dot(hb, sgu_ref[...], preferred_element_type=F32)
    g = gu[:, :SHARED_HIDDEN]
    act = (g * _sigmoid(g)) * gu[:, SHARED_HIDDEN:]
    shared = jnp.dot(act.astype(BF16), sd_ref[...], preferred_element_type=F32)
    x1s_ref[...] = x1 + _per_batch(shared, gt2_ref[...], lambda r, m: r * m)


def _post(x, mixed, gt1, sh2, sc2, gt2, g2, rw_pad, rb_pad, sgu, sd):
    ts = TILE_ROWS
    tiles_per_block = MOE_BLOCK // ts
    mod_spec = _const_spec((1, BATCH, D_MODEL))
    row_spec = lambda rows, c: pl.BlockSpec((rows, c), lambda i: (i, 0))
    return pl.pallas_call(
        _post_kernel,
        out_shape=(jax.ShapeDtypeStruct((N_TOK, D_MODEL), F32),
                   jax.ShapeDtypeStruct((N_TOK * SLAB, LANES), U32),
                   jax.ShapeDtypeStruct((N_TOK, LANES), I32),
                   jax.ShapeDtypeStruct((N_TOK, LANES), F32),
                   jax.ShapeDtypeStruct((MOE_BLOCKS, 1, LANES), F32)),
        grid=(SEQ // TILE_STEPS,),
        in_specs=[_X_SPEC, row_spec(ts, D_MODEL),
                  mod_spec, mod_spec, mod_spec, mod_spec,
                  _const_spec((1, D_MODEL)), _const_spec(rw_pad.shape), _const_spec(rb_pad.shape),
                  _const_spec(sgu.shape), _const_spec(sd.shape)],
        out_specs=(row_spec(ts, D_MODEL), row_spec(ts * SLAB, LANES),
                   row_spec(ts, LANES), row_spec(ts, LANES),
                   pl.BlockSpec((None, 1, LANES), lambda i: (i // tiles_per_block, 0, 0))),
        compiler_params=pltpu.CompilerParams(
            dimension_semantics=("arbitrary",), vmem_limit_bytes=VMEM_LIMIT),
        name="post",
    )(x, mixed, gt1, sh2, sc2, gt2, g2, rw_pad, rb_pad, sgu, sd)


def _slots_kernel(kmap_ref, wc_ref, off_ref, tri_ref, dst_ref, wk_ref, seen_ref):
    @pl.when(pl.program_id(0) % (MOE_BLOCK // TILE_ROWS) == 0)
    def _():
        seen_ref[...] = jnp.zeros_like(seen_ref)

    kmap = kmap_ref[...]
    wc = wc_ref[...]
    sel = (kmap > 0).astype(F32)
    seen = seen_ref[...]
    slot = jnp.dot(tri_ref[...], sel.astype(BF16), preferred_element_type=F32) + (seen + off_ref[...])
    seen_ref[...] = seen + jnp.sum(sel, axis=0, keepdims=True)
    lane = lax.broadcasted_iota(I32, kmap.shape, 1)
    dst = jnp.zeros(kmap.shape, F32)
    w_k = jnp.zeros(kmap.shape, F32)
    for k in range(TOP_K):
        hit = kmap == k + 1
        dst = jnp.where(lane == k, jnp.sum(jnp.where(hit, slot, 0.0), axis=-1, keepdims=True), dst)
        w_k = jnp.where(lane == k, jnp.sum(jnp.where(hit, wc, 0.0), axis=-1, keepdims=True), w_k)
    dst_ref[...] = dst.astype(I32) * SLAB
    wk_ref[...] = w_k


def _slots(kmap, wc, off):
    ts = TILE_ROWS
    tiles_per_block = MOE_BLOCK // ts
    row_spec = pl.BlockSpec((ts, LANES), lambda i: (i, 0))
    tri = (lax.broadcasted_iota(I32, (ts, ts), 0) > lax.broadcasted_iota(I32, (ts, ts), 1)).astype(BF16)
    return pl.pallas_call(
        _slots_kernel,
        out_shape=(jax.ShapeDtypeStruct((N_TOK, LANES), I32), jax.ShapeDtypeStruct((N_TOK, LANES), F32)),
        grid=(N_TOK // ts,),
        in_specs=[row_spec, row_spec,
                  pl.BlockSpec((None, 1, LANES), lambda i: (i // tiles_per_block, 0, 0)),
                  _const_spec(tri.shape)],
        out_specs=(row_spec, row_spec),
        scratch_shapes=[pltpu.VMEM((1, LANES), F32)],
        compiler_params=pltpu.CompilerParams(
            dimension_semantics=("arbitrary",), vmem_limit_bytes=VMEM_LIMIT),
        name="slots",
    )(kmap, wc, off, tri)


def _moe_kernel(cnt_ref, off_ref, dst_ref, w_ref, h_ref, wgu_ref, wd_ref, o_ref, stage_ref):
    blk = pl.program_id(0)
    step = pl.program_id(1)

    @pl.when(jnp.logical_and(blk == 0, step == 0))
    def _():
        stage_ref[TOP_K * MOE_BLOCK * SLAB:, :] = jnp.zeros((MOE_CHUNK * SLAB, LANES), U32)

    def slab_at(ref, sublane_row):
        return ref.at[pl.ds(pl.multiple_of(sublane_row, SLAB), SLAB), :]

    @pl.when(step == 0)
    def _():
        def dispatch(i, carry):
            for j in range(MOE_UNROLL):
                t = i * MOE_UNROLL + j
                slab = slab_at(h_ref, t * SLAB)[...]
                for k in range(TOP_K):
                    slab_at(stage_ref, dst_ref[0, t * TOP_K + k])[...] = slab
            return carry

        lax.fori_loop(0, MOE_BLOCK // MOE_UNROLL, dispatch, 0)

    def run_expert(j):
        e = (step - 1) * MOE_EXPERTS_PER_STEP + j
        n_rows = cnt_ref[blk * N_EXPERTS + e]
        first = off_ref[blk * N_EXPERTS + e]
        row = lax.broadcasted_iota(I32, (MOE_CHUNK, LANES), 0)

        def chunk(c, carry):
            base = (first + c * MOE_CHUNK) * SLAB
            words = _load_slabs(stage_ref, base, MOE_CHUNK)
            halves = [_unpack_words(w) for w in words]
            xs = jnp.concatenate([lo for lo, _ in halves] + [hi for _, hi in halves], axis=-1)
            gu = jnp.dot(xs.astype(BF16), wgu_ref[j], preferred_element_type=F32)
            g = gu[:, :EXPERT_HIDDEN]
            act = (g * _sigmoid(g)) * gu[:, EXPERT_HIDDEN:]
            out = _pack_rows(jnp.dot(act.astype(BF16), wd_ref[j], preferred_element_type=F32))
            live = row < (n_rows - c * MOE_CHUNK)
            for s in range(SLAB):
                merged = jnp.where(live, out[:, s * LANES:(s + 1) * LANES], words[s])
                stage_ref[pl.ds(base + s, MOE_CHUNK, stride=SLAB), :] = merged
            return carry

        lax.fori_loop(0, (n_rows + MOE_CHUNK - 1) // MOE_CHUNK, chunk, 0)

    @pl.when(jnp.logical_and(step >= 1, step <= MOE_EXPERT_STEPS))
    def _():
        for j in range(MOE_EXPERTS_PER_STEP):
            run_expert(j)

    @pl.when(step == MOE_EXPERT_STEPS + 1)
    def _():
        def combine(i, carry):
            for j in range(MOE_UNROLL):
                t = i * MOE_UNROLL + j
                terms = []
                for k in range(TOP_K):
                    w = w_ref[0, t * TOP_K + k]
                    lo, hi = _unpack_words(slab_at(stage_ref, dst_ref[0, t * TOP_K + k])[...])
                    terms.append((lo * w, hi * w))
                while len(terms) > 1:
                    terms = [(a[0] + b[0], a[1] + b[1]) for a, b in zip(terms[0::2], terms[1::2])]
                slab_at(o_ref, t * SLAB)[...] = pltpu.pack_elementwise(list(terms[0]), packed_dtype=BF16)
            return carry

        lax.fori_loop(0, MOE_BLOCK // MOE_UNROLL, combine, 0)


def _moe(cnt, off, dst, w, h2s, wgu, wd):
    expert = lambda b, s, *_: (jnp.clip(s - 1, 0, MOE_EXPERT_STEPS - 1), 0, 0)
    smem_spec = pl.BlockSpec((None, 1, TOP_K * MOE_BLOCK), lambda b, s, *_: (b, 0, 0), memory_space=pltpu.SMEM)
    return pl.pallas_call(
        _moe_kernel,
        out_shape=jax.ShapeDtypeStruct((N_TOK * SLAB, LANES), U32),
        grid_spec=pltpu.PrefetchScalarGridSpec(
            num_scalar_prefetch=2,
            grid=(MOE_BLOCKS, MOE_EXPERT_STEPS + 2),
            in_specs=[smem_spec, smem_spec,
                      pl.BlockSpec((MOE_BLOCK * SLAB, LANES), lambda b, s, *_: (b, 0),
                                   pipeline_mode=pl.Buffered(1)),
                      pl.BlockSpec((MOE_EXPERTS_PER_STEP, D_MODEL, 2 * EXPERT_HIDDEN), expert),
                      pl.BlockSpec((MOE_EXPERTS_PER_STEP, EXPERT_HIDDEN, D_MODEL), expert)],
            out_specs=pl.BlockSpec((MOE_BLOCK * SLAB, LANES), lambda b, s, *_: (b, 0)),
            scratch_shapes=[pltpu.VMEM((STAGE_ROWS * SLAB, LANES), U32)]),
        compiler_params=pltpu.CompilerParams(
            dimension_semantics=("arbitrary", "arbitrary"), vmem_limit_bytes=VMEM_LIMIT),
        name="moe",
    )(cnt, off, dst, w, h2s, wgu, wd)


def _fin_kernel(x1s_ref, r_ref, gt2_ref, fg_ref, o_ref):
    halves = [_unpack_words(w) for w in _load_slabs(r_ref, 0, TILE_ROWS)]
    routed = jnp.concatenate([lo for lo, _ in halves] + [hi for _, hi in halves], axis=-1)
    x2 = x1s_ref[...] + _per_batch(routed, gt2_ref[...], lambda r, m: r * m)
    ms = jnp.mean(x2 * x2, axis=-1, keepdims=True)
    o_ref[...] = _batch_major(x2 * lax.rsqrt(ms + EPS) * fg_ref[...])


def _fin(x1s, routed, gt2, fg):
    ts = TILE_ROWS
    return pl.pallas_call(
        _fin_kernel,
        out_shape=jax.ShapeDtypeStruct((BATCH, SEQ, D_MODEL), F32),
        grid=(SEQ // TILE_STEPS,),
        in_specs=[pl.BlockSpec((ts, D_MODEL), lambda i: (i, 0)),
                  pl.BlockSpec((ts * SLAB, LANES), lambda i: (i, 0)),
                  _const_spec((1, BATCH, D_MODEL)),
                  _const_spec((1, D_MODEL))],
        out_specs=_X_SPEC,
        compiler_params=pltpu.CompilerParams(
            dimension_semantics=("parallel",), vmem_limit_bytes=VMEM_LIMIT),
        name="fin",
    )(x1s, routed, gt2, fg)


def kernel(x, c, w_ada, b_ada, norm1_g, w_in, s5_lambda_re, s5_lambda_im, s5_log_dt, s5_b_re, s5_b_im, s5_c_re, s5_c_im, s5_d, s5_glu_wv, s5_glu_wg, conv_dw_w, conv_dw_b, conv_ln_g, conv_ln_b, conv_pw_w, w_out, norm2_g, router_w, router_bias, exp_w_gate, exp_w_up, exp_w_down, shared_w_gate, shared_w_up, shared_w_down, final_norm_g):
    l = 0
    row = lambda a: a.reshape(1, -1)

    mod = _ada(c, w_ada[l], b_ada[l])
    sh1, sc1, gt1, sh2, sc2, gt2 = jnp.split(mod, N_MOD, axis=-1)
    sh1, sc1 = [m.reshape(BATCH, 1, D_MODEL) for m in (sh1, sc1)]
    gt1, sh2, sc2, gt2 = [m.reshape(1, BATCH, D_MODEL) for m in (gt1, sh2, sc2, gt2)]

    abar_r, abar_i, bbar_r, bbar_i = _s5prep(s5_lambda_re[l], s5_lambda_im[l], s5_log_dt[l],
                                             s5_b_re[l], s5_b_im[l])
    per_group = lambda a: a.reshape(S5_GROUPS, S5_GROUP, S5_STATE)
    a_r = per_group(abar_r)[:, 0, :].reshape(1, N_STATE)
    a_i = per_group(abar_i)[:, 0, :].reshape(1, N_STATE)
    bmat = jnp.concatenate([_block_diag(per_group(bbar_r)), _block_diag(per_group(bbar_i))],
                           axis=1).astype(BF16)
    c_r = _block_diag(jnp.transpose(s5_c_re[l], (0, 2, 1))).astype(BF16)
    c_i = _block_diag(jnp.transpose(s5_c_im[l], (0, 2, 1))).astype(BF16)

    u2, zc2, ga2, gb2 = _proj(x, sh1, sc1, row(norm1_g[l]), w_in[l])
    mixed = _mix(u2, zc2, ga2, gb2,
                 bmat, c_r, c_i, a_r, a_i, row(s5_d[l]),
                 s5_glu_wv[l].astype(BF16), s5_glu_wg[l].astype(BF16),
                 conv_dw_w[l], row(conv_dw_b[l]), row(conv_ln_g[l]), row(conv_ln_b[l]),
                 conv_pw_w[l].astype(BF16), w_out[l].astype(BF16))

    pad = LANES - N_EXPERTS
    rw_pad = jnp.pad(router_w[l], ((0, 0), (0, pad))).astype(BF16)
    rb_pad = jnp.pad(row(router_bias[l]), ((0, 0), (0, pad)))
    sgu = jnp.concatenate([shared_w_gate[l], shared_w_up[l]], axis=1).astype(BF16)
    x1s, h2s, kmap, wc, cnt = _post(x, mixed, gt1, sh2, sc2, gt2, row(norm2_g[l]), rw_pad, rb_pad,
                                    sgu, shared_w_down[l].astype(BF16))

    off = jnp.cumsum(cnt, axis=-1) - cnt
    dst, w_k = _slots(kmap, wc, off)
    as_ints = lambda a: a[:, 0, :N_EXPERTS].astype(I32).reshape(-1)
    per_block = lambda a: a[:, :TOP_K].reshape(MOE_BLOCKS, 1, MOE_BLOCK * TOP_K)

    wgu = jnp.concatenate([exp_w_gate[l], exp_w_up[l]], axis=2).astype(BF16)
    routed = _moe(as_ints(cnt), as_ints(off), per_block(dst), per_block(w_k), h2s, wgu,
                  exp_w_down[l].astype(BF16))
    return _fin(x1s, routed, gt2, row(final_norm_g))
```

```python
import functools

import jax
import jax.numpy as jnp
from jax import lax
from jax.experimental import pallas as pl
from jax.experimental.pallas import tpu as pltpu

D_MODEL = 1024
BATCH = 8
SEQ = 4096
N_TOK = BATCH * SEQ
S5_WIDTH = 512
S5_GROUP = 16
S5_GROUPS = 32
S5_STATE = 64
N_STATE = S5_GROUPS * S5_STATE
CONV_WIDTH = 512
CONV_KERNEL = 31
N_EXPERTS = 64
TOP_K = 8
EXPERT_HIDDEN = 256
SHARED_HIDDEN = 256
ROUTED_SCALE = 2.5
N_MOD = 6
EPS = 1e-6

LANES = 128
SUBLANES = 8
VMEM_LIMIT = 56 * 1024 * 1024

TILE_STEPS = 64
TILE_ROWS = TILE_STEPS * BATCH
MIX_STEPS = TILE_STEPS
MIX_ROWS = TILE_ROWS
CONV_HALO = (CONV_KERNEL - 1) * BATCH
CONV_CHUNK = 64
SCAN_LANES = 512
MOE_BLOCK = 2048
MOE_BLOCKS = N_TOK // MOE_BLOCK
MOE_CHUNK = 320
MOE_EXPERTS_PER_STEP = 2
MOE_EXPERT_STEPS = N_EXPERTS // MOE_EXPERTS_PER_STEP
MOE_UNROLL = 4
SLAB = 4
STAGE_ROWS = TOP_K * MOE_BLOCK + MOE_CHUNK

F32 = jnp.float32
BF16 = jnp.bfloat16
U32 = jnp.uint32
I32 = jnp.int32


_X_SPEC = pl.BlockSpec((BATCH, TILE_STEPS, D_MODEL), lambda i: (0, i, 0))


def _time_major(a):
    return jnp.swapaxes(a, 0, 1).reshape(a.shape[0] * a.shape[1], a.shape[2])


def _batch_major(a):
    return jnp.swapaxes(a.reshape(a.shape[0] // BATCH, BATCH, a.shape[1]), 0, 1)


def _per_batch(rows, mod, op):
    r3 = rows.reshape(rows.shape[0] // BATCH, BATCH, rows.shape[1])
    return op(r3, mod).reshape(rows.shape)


def _pack_rows(x):
    half = D_MODEL // 2
    return pltpu.pack_elementwise([x[:, :half], x[:, half:]], packed_dtype=BF16)


def _unpack_words(w):
    lo = pltpu.unpack_elementwise(w, index=0, packed_dtype=BF16, unpacked_dtype=F32)
    hi = pltpu.unpack_elementwise(w, index=1, packed_dtype=BF16, unpacked_dtype=F32)
    return lo, hi


def _store_slabs(ref, base, words):
    rows = words.shape[0]
    for s in range(SLAB):
        ref[pl.ds(base + s, rows, stride=SLAB), :] = words[:, s * LANES:(s + 1) * LANES]


def _load_slabs(ref, base, rows):
    return [ref[pl.ds(base + s, rows, stride=SLAB), :] for s in range(SLAB)]


def _bdot(a, b):
    return jnp.dot(a.astype(BF16), b.astype(BF16), preferred_element_type=F32)


def _sigmoid(x):
    return jax.nn.sigmoid(x)


def _const_spec(shape):
    zeros = (0,) * len(shape)
    return pl.BlockSpec(shape, lambda *_: zeros, pipeline_mode=pl.Buffered(1))


def _ada_kernel(c_ref, w_ref, b_ref, o_ref):
    c = c_ref[...]
    c_act = c * _sigmoid(c)
    o_ref[...] = jnp.dot(c_act, w_ref[...], preferred_element_type=F32,
                         precision=lax.Precision.HIGHEST) + b_ref[...]


def _ada(c, w_ada, b_ada):
    n_out = N_MOD * D_MODEL
    blk = 1536
    return pl.pallas_call(
        _ada_kernel,
        out_shape=jax.ShapeDtypeStruct((BATCH, n_out), F32),
        grid=(n_out // blk,),
        in_specs=[pl.BlockSpec((BATCH, D_MODEL), lambda j: (0, 0)),
                  pl.BlockSpec((D_MODEL, blk), lambda j: (0, j)),
                  pl.BlockSpec((1, blk), lambda j: (0, j))],
        out_specs=pl.BlockSpec((BATCH, blk), lambda j: (0, j)),
        compiler_params=pltpu.CompilerParams(vmem_limit_bytes=VMEM_LIMIT),
        name="ada",
    )(c, w_ada, b_ada.reshape(1, n_out))


def _s5prep_kernel(lr_ref, li_ref, ldt_ref, br_ref, bi_ref, ar_ref, ai_ref, bbr_ref, bbi_ref):
    lr = lr_ref[...]
    li = li_ref[...]
    dt = jnp.exp(ldt_ref[...])
    mag = jnp.exp(lr * dt)
    abar_r = mag * jnp.cos(li * dt)
    abar_i = mag * jnp.sin(li * dt)
    den = lr * lr + li * li
    nr = abar_r - 1.0
    ni = abar_i
    k_r = (nr * lr + ni * li) / den
    k_i = (ni * lr - nr * li) / den
    br = br_ref[...]
    bi = bi_ref[...]
    ar_ref[...] = abar_r
    ai_ref[...] = abar_i
    bbr_ref[...] = k_r * br - k_i * bi
    bbi_ref[...] = k_r * bi + k_i * br


def _s5prep(lam_re, lam_im, log_dt, b_re, b_im):
    rows = S5_GROUPS * S5_GROUP
    rep = lambda a: jnp.broadcast_to(a[:, None, :], (S5_GROUPS, S5_GROUP, S5_STATE)).reshape(rows, S5_STATE)
    lr = rep(lam_re)
    li = rep(lam_im)
    ldt = rep(jnp.broadcast_to(log_dt[:, None], (S5_GROUPS, S5_STATE)))
    br = jnp.transpose(b_re, (0, 2, 1)).reshape(rows, S5_STATE)
    bi = jnp.transpose(b_im, (0, 2, 1)).reshape(rows, S5_STATE)
    shp = jax.ShapeDtypeStruct((rows, S5_STATE), F32)
    return pl.pallas_call(_s5prep_kernel, out_shape=(shp, shp, shp, shp), name="s5prep")(lr, li, ldt, br, bi)


def _block_diag(blocks):
    g, r, c = blocks.shape
    eye = jnp.eye(g, dtype=blocks.dtype)
    return jnp.einsum("grc,gk->grkc", blocks, eye).reshape(g * r, g * c)


def _proj_kernel(x_ref, sh_ref, sc_ref, g_ref, wu_ref, wv_ref, wg_ref, wa_ref, wb_ref,
                 u_ref, zc_ref, ga_ref, gb_ref):
    x = x_ref[...]
    ms = jnp.mean(x * x, axis=-1, keepdims=True)
    h = x * lax.rsqrt(ms + EPS) * g_ref[...]
    h = h * (1.0 + sc_ref[...]) + sh_ref[...]
    h = _time_major(h).astype(BF16)
    u_ref[...] = jnp.dot(h, wu_ref[...], preferred_element_type=F32).astype(BF16)
    v = jnp.dot(h, wv_ref[...], preferred_element_type=F32)
    g = jnp.dot(h, wg_ref[...], preferred_element_type=F32)
    zc_ref[...] = (v * _sigmoid(g)).astype(BF16)
    ga_ref[...] = _sigmoid(jnp.dot(h, wa_ref[...], preferred_element_type=F32)).astype(BF16)
    gb_ref[...] = _sigmoid(jnp.dot(h, wb_ref[...], preferred_element_type=F32)).astype(BF16)


def _proj(x, sh1, sc1, g1, w_in):
    o1 = S5_WIDTH
    o2 = o1 + CONV_WIDTH
    o3 = o2 + CONV_WIDTH
    o4 = o3 + D_MODEL
    w = w_in.astype(BF16)
    ws = (w[:, :o1], w[:, o1:o2], w[:, o2:o3], w[:, o3:o4], w[:, o4:])
    row_spec = lambda c: pl.BlockSpec((TILE_ROWS, c), lambda i: (i, 0))
    widths = (S5_WIDTH, CONV_WIDTH, D_MODEL, D_MODEL)
    return pl.pallas_call(
        _proj_kernel,
        out_shape=tuple(jax.ShapeDtypeStruct((N_TOK, c), BF16) for c in widths),
        grid=(SEQ // TILE_STEPS,),
        in_specs=[_X_SPEC, _const_spec(sh1.shape), _const_spec(sc1.shape),
                  _const_spec((1, D_MODEL))] + [_const_spec(wi.shape) for wi in ws],
        out_specs=tuple(row_spec(c) for c in widths),
        compiler_params=pltpu.CompilerParams(
            dimension_semantics=("parallel",), vmem_limit_bytes=VMEM_LIMIT),
        name="proj",
    )(x, sh1, sc1, g1, *ws)


def _gelu_tanh(x):
    sqrt_2_over_pi = 0.7978845608028654
    cdf = 0.5 * (1.0 + jnp.tanh(sqrt_2_over_pi * (x + 0.044715 * (x * x * x))))
    return x * cdf


def _mix_kernel(u_ref, zc_ref, ga_ref, gb_ref, bsp_ref, csr_ref, csi_ref, ar_ref, ai_ref, d_ref,
                wv_ref, wg_ref, dww_ref, dwb_ref, lng_ref, lnb_ref, pw_ref, wout_ref,
                out_ref, bu_ref, sr_ref, si_ref, cbuf_ref, cv_ref):
    step = pl.program_id(0)

    @pl.when(step == 0)
    def _():
        sr_ref[...] = jnp.zeros_like(sr_ref)
        si_ref[...] = jnp.zeros_like(si_ref)
        cbuf_ref[0:CONV_HALO, :] = jnp.zeros((CONV_HALO, CONV_WIDTH), F32)

    u = u_ref[...]
    for j in range(N_STATE // LANES):
        kb = (j * LANES // S5_STATE * S5_GROUP) // LANES
        res = jnp.dot(u[:, kb * LANES:(kb + 1) * LANES], bsp_ref[j], preferred_element_type=F32)
        bu_ref[:, j * LANES:(j + 1) * LANES] = res[:, :LANES]
        bu_ref[:, N_STATE + j * LANES:N_STATE + (j + 1) * LANES] = res[:, LANES:]

    for lg in range(N_STATE // SCAN_LANES):
        lo = lg * SCAN_LANES
        re = slice(lo, lo + SCAN_LANES)
        im = slice(N_STATE + lo, N_STATE + lo + SCAN_LANES)
        a_r = jnp.broadcast_to(ar_ref[:, re], (SUBLANES, SCAN_LANES))
        a_i = jnp.broadcast_to(ai_ref[:, re], (SUBLANES, SCAN_LANES))
        s_r = sr_ref[:, re]
        s_i = si_ref[:, re]
        for t in range(MIX_STEPS):
            rows = slice(t * SUBLANES, (t + 1) * SUBLANES)
            s_r, s_i = (a_r * s_r - a_i * s_i + bu_ref[rows, re],
                        a_r * s_i + a_i * s_r + bu_ref[rows, im])
            bu_ref[rows, re] = s_r
            bu_ref[rows, im] = s_i
        sr_ref[:, re] = s_r
        si_ref[:, re] = s_i

    half_c = S5_WIDTH // 2
    half_s = N_STATE // 2
    ys = []
    for hf in range(2):
        s_re = bu_ref[:, hf * half_s:(hf + 1) * half_s].astype(BF16)
        s_im = bu_ref[:, N_STATE + hf * half_s:N_STATE + (hf + 1) * half_s].astype(BF16)
        ys.append(jnp.dot(s_re, csr_ref[hf], preferred_element_type=F32)
                  - jnp.dot(s_im, csi_ref[hf], preferred_element_type=F32))
    y = jnp.concatenate(ys, axis=-1) + d_ref[...] * u.astype(F32)
    z = _gelu_tanh(y).astype(BF16)
    y_a = (jnp.dot(z, wv_ref[...], preferred_element_type=F32)
           * _sigmoid(jnp.dot(z, wg_ref[...], preferred_element_type=F32)))

    cbuf_ref[CONV_HALO:CONV_HALO + MIX_ROWS, :] = zc_ref[...].astype(F32)
    for ci in range(MIX_ROWS // CONV_CHUNK):
        r0 = ci * CONV_CHUNK
        for lt in range(CONV_WIDTH // LANES):
            ls = slice(lt * LANES, (lt + 1) * LANES)
            win = cbuf_ref[r0:r0 + CONV_CHUNK + CONV_HALO, ls]
            acc = jnp.broadcast_to(dwb_ref[:, ls], (CONV_CHUNK, LANES))
            for k in range(CONV_KERNEL):
                acc = acc + dww_ref[k:k + 1, ls] * win[k * BATCH:k * BATCH + CONV_CHUNK]
            cv_ref[r0:r0 + CONV_CHUNK, ls] = acc
    cbuf_ref[0:CONV_HALO, :] = cbuf_ref[MIX_ROWS:MIX_ROWS + CONV_HALO, :]

    cv = cv_ref[...]
    mu = jnp.mean(cv, axis=-1, keepdims=True)
    var = jnp.mean(jnp.square(cv - mu), axis=-1, keepdims=True)
    ln = (cv - mu) * lax.rsqrt(var + EPS) * lng_ref[...] + lnb_ref[...]
    zb = (ln * _sigmoid(ln)).astype(BF16)
    y_b = jnp.dot(zb, pw_ref[...], preferred_element_type=F32)

    m = ga_ref[...].astype(F32) * y_a + gb_ref[...].astype(F32) * y_b
    out_ref[...] = jnp.dot(m.astype(BF16), wout_ref[...], preferred_element_type=F32).astype(BF16)


def _mix(u2, zc2, ga2, gb2, bmat, c_r, c_i, a_r, a_i, d_row, wv, wg, dww, dwb, lng, lnb, pw, wout):
    rows = MIX_ROWS
    row_spec = lambda c: pl.BlockSpec((rows, c), lambda i: (i, 0))
    tiles = []
    for j in range(N_STATE // LANES):
        kb = (j * LANES // S5_STATE * S5_GROUP) // LANES
        k_rows = slice(kb * LANES, (kb + 1) * LANES)
        tiles.append(jnp.concatenate([bmat[k_rows, j * LANES:(j + 1) * LANES],
                                      bmat[k_rows, N_STATE + j * LANES:N_STATE + (j + 1) * LANES]], axis=1))
    bsp = jnp.stack(tiles)
    half_c, half_s = S5_WIDTH // 2, N_STATE // 2
    halves = lambda m: jnp.stack([m[hf * half_s:(hf + 1) * half_s, hf * half_c:(hf + 1) * half_c]
                                  for hf in range(2)])
    consts = (bsp, halves(c_r), halves(c_i), a_r, a_i, d_row, wv, wg, dww, dwb, lng, lnb, pw, wout)
    return pl.pallas_call(
        _mix_kernel,
        out_shape=jax.ShapeDtypeStruct((SEQ * BATCH, D_MODEL), BF16),
        grid=(SEQ // MIX_STEPS,),
        in_specs=[row_spec(S5_WIDTH), row_spec(CONV_WIDTH), row_spec(D_MODEL), row_spec(D_MODEL)]
                 + [_const_spec(a.shape) for a in consts],
        out_specs=row_spec(D_MODEL),
        scratch_shapes=[pltpu.VMEM((rows, 2 * N_STATE), F32),
                        pltpu.VMEM((SUBLANES, N_STATE), F32),
                        pltpu.VMEM((SUBLANES, N_STATE), F32),
                        pltpu.VMEM((CONV_HALO + rows, CONV_WIDTH), F32),
                        pltpu.VMEM((rows, CONV_WIDTH), F32)],
        compiler_params=pltpu.CompilerParams(
            dimension_semantics=("arbitrary",), vmem_limit_bytes=VMEM_LIMIT),
        name="mix",
    )(u2, zc2, ga2, gb2, *consts)


def _post_kernel(x_ref, mx_ref, gt1_ref, sh_ref, sc_ref, gt2_ref, g_ref, rw_ref, rb_ref,
                 sgu_ref, sd_ref, x1s_ref, h2_ref, kmap_ref, wc_ref, cnt_ref):
    x1 = _time_major(x_ref[...]) + _per_batch(mx_ref[...].astype(F32), gt1_ref[...], lambda r, m: r * m)
    ms = jnp.mean(x1 * x1, axis=-1, keepdims=True)
    h = x1 * lax.rsqrt(ms + EPS) * g_ref[...]
    h = _per_batch(h, (sc_ref[...], sh_ref[...]), lambda r, m: r * (1.0 + m[0]) + m[1])
    hb = h.astype(BF16)
    _store_slabs(h2_ref, 0, _pack_rows(hb.astype(F32)))

    scores = _sigmoid(lax.dot_general(rw_ref[...], hb, (((1,), (1,)), ((), ())),
                                      preferred_element_type=F32))
    expert = lax.broadcasted_iota(I32, scores.shape, 0)
    neg_inf = jnp.float32(-jnp.inf)
    work = scores + rb_ref[...]
    kmap = jnp.zeros(scores.shape, I32)
    for k in range(TOP_K):
        m = jnp.max(work, axis=0, keepdims=True)
        idx = jnp.min(jnp.where(work == m, expert, N_EXPERTS), axis=0, keepdims=True)
        hit = expert == idx
        kmap = jnp.where(hit, k + 1, kmap)
        work = jnp.where(hit, neg_inf, work)
    sel = kmap > 0
    w_sel = jnp.where(sel, scores, 0.0)
    kmap_ref[...] = kmap
    wc_ref[...] = w_sel / jnp.sum(w_sel, axis=0, keepdims=True) * ROUTED_SCALE

    @pl.when(pl.program_id(0) % (MOE_BLOCK // TILE_ROWS) == 0)
    def _():
        cnt_ref[...] = jnp.zeros_like(cnt_ref)

    cnt_ref[...] += jnp.sum(sel.astype(F32), axis=1, keepdims=True)

    gu = jnp.dot(hb, sgu_ref[...], preferred_element_type=F32)
    g = gu[:, :SHARED_HIDDEN]
    act = (g * _sigmoid(g)) * gu[:, SHARED_HIDDEN:]
    shared = jnp.dot(act.astype(BF16), sd_ref[...], preferred_element_type=F32)
    x1s_ref[...] = x1 + _per_batch(shared, gt2_ref[...], lambda r, m: r * m)


def _post(x, mixed, gt1, sh2, sc2, gt2, g2, rw_t, rb_col, sgu, sd):
    ts = TILE_ROWS
    tiles_per_block = MOE_BLOCK // ts
    mod_spec = _const_spec((1, BATCH, D_MODEL))
    row_spec = lambda rows, c: pl.BlockSpec((rows, c), lambda i: (i, 0))
    route_spec = pl.BlockSpec((N_EXPERTS, ts), lambda i: (0, i))
    return pl.pallas_call(
        _post_kernel,
        out_shape=(jax.ShapeDtypeStruct((N_TOK, D_MODEL), F32),
                   jax.ShapeDtypeStruct((N_TOK * SLAB, LANES), U32),
                   jax.ShapeDtypeStruct((N_EXPERTS, N_TOK), I32),
                   jax.ShapeDtypeStruct((N_EXPERTS, N_TOK), F32),
                   jax.ShapeDtypeStruct((MOE_BLOCKS, N_EXPERTS, 1), F32)),
        grid=(SEQ // TILE_STEPS,),
        in_specs=[_X_SPEC, row_spec(ts, D_MODEL),
                  mod_spec, mod_spec, mod_spec, mod_spec,
                  _const_spec((1, D_MODEL)), _const_spec(rw_t.shape), _const_spec(rb_col.shape),
                  _const_spec(sgu.shape), _const_spec(sd.shape)],
        out_specs=(row_spec(ts, D_MODEL), row_spec(ts * SLAB, LANES), route_spec, route_spec,
                   pl.BlockSpec((None, N_EXPERTS, 1), lambda i: (i // tiles_per_block, 0, 0))),
        compiler_params=pltpu.CompilerParams(
            dimension_semantics=("arbitrary",), vmem_limit_bytes=VMEM_LIMIT),
        name="post",
    )(x, mixed, gt1, sh2, sc2, gt2, g2, rw_t, rb_col, sgu, sd)


def _slots_kernel(kmap_ref, wc_ref, off_ref, tri_ref, dst_ref, wk_ref, seen_ref):
    @pl.when(pl.program_id(0) % (MOE_BLOCK // TILE_ROWS) == 0)
    def _():
        seen_ref[...] = jnp.zeros_like(seen_ref)

    kmap = kmap_ref[...]
    wc = wc_ref[...]
    sel = (kmap > 0).astype(F32)
    seen = seen_ref[...]
    slot = jnp.dot(sel.astype(BF16), tri_ref[...], preferred_element_type=F32) + (seen + off_ref[...])
    seen_ref[...] = seen + jnp.sum(sel, axis=1, keepdims=True)
    dst, w_k = [], []
    for k in range(TOP_K):
        hit = kmap == k + 1
        dst.append(jnp.sum(jnp.where(hit, slot, 0.0), axis=0, keepdims=True))
        w_k.append(jnp.sum(jnp.where(hit, wc, 0.0), axis=0, keepdims=True))
    dst_ref[...] = jnp.concatenate(dst, axis=0).astype(I32) * SLAB
    wk_ref[...] = jnp.concatenate(w_k, axis=0)


def _slots(kmap, wc, off):
    ts = TILE_ROWS
    tiles_per_block = MOE_BLOCK // ts
    route_spec = pl.BlockSpec((N_EXPERTS, ts), lambda i: (0, i))
    slot_spec = pl.BlockSpec((TOP_K, ts), lambda i: (0, i))
    tri = (lax.broadcasted_iota(I32, (ts, ts), 0) < lax.broadcasted_iota(I32, (ts, ts), 1)).astype(BF16)
    return pl.pallas_call(
        _slots_kernel,
        out_shape=(jax.ShapeDtypeStruct((TOP_K, N_TOK), I32), jax.ShapeDtypeStruct((TOP_K, N_TOK), F32)),
        grid=(N_TOK // ts,),
        in_specs=[route_spec, route_spec,
                  pl.BlockSpec((None, N_EXPERTS, 1), lambda i: (i // tiles_per_block, 0, 0)),
                  _const_spec(tri.shape)],
        out_specs=(slot_spec, slot_spec),
        scratch_shapes=[pltpu.VMEM((N_EXPERTS, 1), F32)],
        compiler_params=pltpu.CompilerParams(
            dimension_semantics=("arbitrary",), vmem_limit_bytes=VMEM_LIMIT),
        name="slots",
    )(kmap, wc, off, tri)


def _moe_kernel(cnt_ref, off_ref, dst_ref, w_ref, h_ref, wgu_ref, wd_ref, o_ref, stage_ref):
    blk = pl.program_id(0)
    step = pl.program_id(1)

    @pl.when(jnp.logical_and(blk == 0, step == 0))
    def _():
        stage_ref[TOP_K * MOE_BLOCK * SLAB:, :] = jnp.zeros((MOE_CHUNK * SLAB, LANES), U32)

    def slab_at(ref, sublane_row):
        return ref.at[pl.ds(pl.multiple_of(sublane_row, SLAB), SLAB), :]

    group = MOE_UNROLL * TOP_K

    def selections(ref, i):
        return ref.at[0, pl.ds(pl.multiple_of(i * group, group), group)]

    @pl.when(step == 0)
    def _():
        def dispatch(i, carry):
            dst = selections(dst_ref, i)
            for j in range(MOE_UNROLL):
                slab = slab_at(h_ref, (i * MOE_UNROLL + j) * SLAB)[...]
                for k in range(TOP_K):
                    slab_at(stage_ref, dst[j * TOP_K + k])[...] = slab
            return carry

        lax.fori_loop(0, MOE_BLOCK // MOE_UNROLL, dispatch, 0)

    row = lax.broadcasted_iota(I32, (MOE_CHUNK, LANES), 0)

    def load_chunk(first, c):
        return _load_slabs(stage_ref, (first + c * MOE_CHUNK) * SLAB, MOE_CHUNK)

    def expert_mlp(j, words):
        halves = [_unpack_words(w) for w in words]
        xs = jnp.concatenate([lo for lo, _ in halves] + [hi for _, hi in halves], axis=-1)
        gu = jnp.dot(xs.astype(BF16), wgu_ref[j], preferred_element_type=F32)
        g = gu[:, :EXPERT_HIDDEN]
        act = (g * _sigmoid(g)) * gu[:, EXPERT_HIDDEN:]
        return _pack_rows(jnp.dot(act.astype(BF16), wd_ref[j], preferred_element_type=F32))

    def store_chunk(first, n_rows, c, words, out):
        base = (first + c * MOE_CHUNK) * SLAB
        live = row < (n_rows - c * MOE_CHUNK)
        for s in range(SLAB):
            merged = jnp.where(live, out[:, s * LANES:(s + 1) * LANES], words[s])
            stage_ref[pl.ds(base + s, MOE_CHUNK, stride=SLAB), :] = merged

    @pl.when(jnp.logical_and(step >= 1, step <= MOE_EXPERT_STEPS))
    def _():
        experts = [(step - 1) * MOE_EXPERTS_PER_STEP + j for j in range(MOE_EXPERTS_PER_STEP)]
        counts = [cnt_ref[blk * N_EXPERTS + e] for e in experts]
        firsts = [off_ref[blk * N_EXPERTS + e] for e in experts]
        words = [load_chunk(firsts[j], 0) for j in range(MOE_EXPERTS_PER_STEP)]
        outs = [expert_mlp(j, words[j]) for j in range(MOE_EXPERTS_PER_STEP)]
        for j in range(MOE_EXPERTS_PER_STEP):
            store_chunk(firsts[j], counts[j], 0, words[j], outs[j])
        for j in range(MOE_EXPERTS_PER_STEP):
            def more(c, carry, j=j):
                w_c = load_chunk(firsts[j], c)
                store_chunk(firsts[j], counts[j], c, w_c, expert_mlp(j, w_c))
                return carry

            lax.fori_loop(1, (counts[j] + MOE_CHUNK - 1) // MOE_CHUNK, more, 0)

    @pl.when(step == MOE_EXPERT_STEPS + 1)
    def _():
        def combine(i, carry):
            dst = selections(dst_ref, i)
            wts = selections(w_ref, i)
            for j in range(MOE_UNROLL):
                t = i * MOE_UNROLL + j
                terms = []
                for k in range(TOP_K):
                    w = wts[j * TOP_K + k]
                    lo, hi = _unpack_words(slab_at(stage_ref, dst[j * TOP_K + k])[...])
                    terms.append((lo * w, hi * w))
                while len(terms) > 1:
                    terms = [(a[0] + b[0], a[1] + b[1]) for a, b in zip(terms[0::2], terms[1::2])]
                slab_at(o_ref, t * SLAB)[...] = pltpu.pack_elementwise(list(terms[0]), packed_dtype=BF16)
            return carry

        lax.fori_loop(0, MOE_BLOCK // MOE_UNROLL, combine, 0)


def _moe(cnt, off, dst, w, h2s, wgu, wd):
    expert = lambda b, s, *_: (jnp.clip(s - 1, 0, MOE_EXPERT_STEPS - 1), 0, 0)
    smem_spec = pl.BlockSpec((None, 1, TOP_K * MOE_BLOCK), lambda b, s, *_: (b, 0, 0), memory_space=pltpu.SMEM)
    return pl.pallas_call(
        _moe_kernel,
        out_shape=jax.ShapeDtypeStruct((N_TOK * SLAB, LANES), U32),
        grid_spec=pltpu.PrefetchScalarGridSpec(
            num_scalar_prefetch=2,
            grid=(MOE_BLOCKS, MOE_EXPERT_STEPS + 2),
            in_specs=[smem_spec, smem_spec,
                      pl.BlockSpec((MOE_BLOCK * SLAB, LANES), lambda b, s, *_: (b, 0),
                                   pipeline_mode=pl.Buffered(1)),
                      pl.BlockSpec((MOE_EXPERTS_PER_STEP, D_MODEL, 2 * EXPERT_HIDDEN), expert),
                      pl.BlockSpec((MOE_EXPERTS_PER_STEP, EXPERT_HIDDEN, D_MODEL), expert)],
            out_specs=pl.BlockSpec((MOE_BLOCK * SLAB, LANES), lambda b, s, *_: (b, 0)),
            scratch_shapes=[pltpu.VMEM((STAGE_ROWS * SLAB, LANES), U32)]),
        compiler_params=pltpu.CompilerParams(
            dimension_semantics=("arbitrary", "arbitrary"), vmem_limit_bytes=VMEM_LIMIT),
        name="moe",
    )(cnt, off, dst, w, h2s, wgu, wd)


def _fin_kernel(x1s_ref, r_ref, gt2_ref, fg_ref, o_ref):
    halves = [_unpack_words(w) for w in _load_slabs(r_ref, 0, TILE_ROWS)]
    routed = jnp.concatenate([lo for lo, _ in halves] + [hi for _, hi in halves], axis=-1)
    x2 = x1s_ref[...] + _per_batch(routed, gt2_ref[...], lambda r, m: r * m)
    ms = jnp.mean(x2 * x2, axis=-1, keepdims=True)
    o_ref[...] = _batch_major(x2 * lax.rsqrt(ms + EPS) * fg_ref[...])


def _fin(x1s, routed, gt2, fg):
    ts = TILE_ROWS
    return pl.pallas_call(
        _fin_kernel,
        out_shape=jax.ShapeDtypeStruct((BATCH, SEQ, D_MODEL), F32),
        grid=(SEQ // TILE_STEPS,),
        in_specs=[pl.BlockSpec((ts, D_MODEL), lambda i: (i, 0)),
                  pl.BlockSpec((ts * SLAB, LANES), lambda i: (i, 0)),
                  _const_spec((1, BATCH, D_MODEL)),
                  _const_spec((1, D_MODEL))],
        out_specs=_X_SPEC,
        compiler_params=pltpu.CompilerParams(
            dimension_semantics=("parallel",), vmem_limit_bytes=VMEM_LIMIT),
        name="fin",
    )(x1s, routed, gt2, fg)


def kernel(x, c, w_ada, b_ada, norm1_g, w_in, s5_lambda_re, s5_lambda_im, s5_log_dt, s5_b_re, s5_b_im, s5_c_re, s5_c_im, s5_d, s5_glu_wv, s5_glu_wg, conv_dw_w, conv_dw_b, conv_ln_g, conv_ln_b, conv_pw_w, w_out, norm2_g, router_w, router_bias, exp_w_gate, exp_w_up, exp_w_down, shared_w_gate, shared_w_up, shared_w_down, final_norm_g):
    l = 0
    row = lambda a: a.reshape(1, -1)

    mod = _ada(c, w_ada[l], b_ada[l])
    sh1, sc1, gt1, sh2, sc2, gt2 = jnp.split(mod, N_MOD, axis=-1)
    sh1, sc1 = [m.reshape(BATCH, 1, D_MODEL) for m in (sh1, sc1)]
    gt1, sh2, sc2, gt2 = [m.reshape(1, BATCH, D_MODEL) for m in (gt1, sh2, sc2, gt2)]

    abar_r, abar_i, bbar_r, bbar_i = _s5prep(s5_lambda_re[l], s5_lambda_im[l], s5_log_dt[l],
                                             s5_b_re[l], s5_b_im[l])
    per_group = lambda a: a.reshape(S5_GROUPS, S5_GROUP, S5_STATE)
    a_r = per_group(abar_r)[:, 0, :].reshape(1, N_STATE)
    a_i = per_group(abar_i)[:, 0, :].reshape(1, N_STATE)
    bmat = jnp.concatenate([_block_diag(per_group(bbar_r)), _block_diag(per_group(bbar_i))],
                           axis=1).astype(BF16)
    c_r = _block_diag(jnp.transpose(s5_c_re[l], (0, 2, 1))).astype(BF16)
    c_i = _block_diag(jnp.transpose(s5_c_im[l], (0, 2, 1))).astype(BF16)

    u2, zc2, ga2, gb2 = _proj(x, sh1, sc1, row(norm1_g[l]), w_in[l])
    mixed = _mix(u2, zc2, ga2, gb2,
                 bmat, c_r, c_i, a_r, a_i, row(s5_d[l]),
                 s5_glu_wv[l].astype(BF16), s5_glu_wg[l].astype(BF16),
                 conv_dw_w[l], row(conv_dw_b[l]), row(conv_ln_g[l]), row(conv_ln_b[l]),
                 conv_pw_w[l].astype(BF16), w_out[l].astype(BF16))

    sgu = jnp.concatenate([shared_w_gate[l], shared_w_up[l]], axis=1).astype(BF16)
    x1s, h2s, kmap, wc, cnt = _post(x, mixed, gt1, sh2, sc2, gt2, row(norm2_g[l]),
                                    router_w[l].T.astype(BF16), router_bias[l].reshape(N_EXPERTS, 1),
                                    sgu, shared_w_down[l].astype(BF16))

    off = jnp.cumsum(cnt, axis=1) - cnt
    dst, w_k = _slots(kmap, wc, off)
    as_ints = lambda a: a.astype(I32).reshape(-1)
    per_block = lambda a: a.T.reshape(MOE_BLOCKS, 1, MOE_BLOCK * TOP_K)

    wgu = jnp.concatenate([exp_w_gate[l], exp_w_up[l]], axis=2).astype(BF16)
    routed = _moe(as_ints(cnt), as_ints(off), per_block(dst), per_block(w_k), h2s, wgu,
                  exp_w_down[l].astype(BF16))
    return _fin(x1s, routed, gt2, row(final_norm_g))
```

```python
import jax
import jax.numpy as jnp
from jax import lax
from jax.experimental import pallas as pl
from jax.experimental.pallas import tpu as pltpu

D_MODEL = 1024
BATCH = 8
SEQ = 4096
N_TOK = BATCH * SEQ
S5_WIDTH = 512
S5_GROUP = 16
S5_GROUPS = 32
S5_STATE = 64
N_STATE = S5_GROUPS * S5_STATE
CONV_WIDTH = 512
CONV_KERNEL = 31
N_EXPERTS = 64
TOP_K = 8
EXPERT_HIDDEN = 256
SHARED_HIDDEN = 256
ROUTED_SCALE = 2.5
N_MOD = 6
EPS = 1e-6

LANES = 128
SUBLANES = 8
VMEM_LIMIT = 56 * 1024 * 1024

TILE_STEPS = 64
TILE_ROWS = TILE_STEPS * BATCH
CONV_HALO = (CONV_KERNEL - 1) * BATCH
CONV_CHUNK = 64
SCAN_LANES = 512
MOE_BLOCK = 2048
MOE_BLOCKS = N_TOK // MOE_BLOCK
MOE_CHUNK = 320
MOE_EXPERTS_PER_STEP = 2
MOE_EXPERT_STEPS = N_EXPERTS // MOE_EXPERTS_PER_STEP
MOE_UNROLL = 4
SLAB = 4
STAGE_ROWS = TOP_K * MOE_BLOCK + MOE_CHUNK

F32 = jnp.float32
BF16 = jnp.bfloat16
U32 = jnp.uint32
I32 = jnp.int32


_X_SPEC = pl.BlockSpec((BATCH, TILE_STEPS, D_MODEL), lambda i: (0, i, 0))


def _time_major(a):
    return jnp.swapaxes(a, 0, 1).reshape(a.shape[0] * a.shape[1], a.shape[2])


def _batch_major(a):
    return jnp.swapaxes(a.reshape(a.shape[0] // BATCH, BATCH, a.shape[1]), 0, 1)


def _per_batch(rows, mod, op):
    r3 = rows.reshape(rows.shape[0] // BATCH, BATCH, rows.shape[1])
    return op(r3, mod).reshape(rows.shape)


def _pack_rows(x):
    half = D_MODEL // 2
    return pltpu.pack_elementwise([x[:, :half], x[:, half:]], packed_dtype=BF16)


def _unpack_words(w):
    lo = pltpu.unpack_elementwise(w, index=0, packed_dtype=BF16, unpacked_dtype=F32)
    hi = pltpu.unpack_elementwise(w, index=1, packed_dtype=BF16, unpacked_dtype=F32)
    return lo, hi


def _store_slabs(ref, base, words):
    rows = words.shape[0]
    for s in range(SLAB):
        ref[pl.ds(base + s, rows, stride=SLAB), :] = words[:, s * LANES:(s + 1) * LANES]


def _load_slabs(ref, base, rows):
    return [ref[pl.ds(base + s, rows, stride=SLAB), :] for s in range(SLAB)]


def _sigmoid(x):
    return jax.nn.sigmoid(x)


def _const_spec(shape):
    zeros = (0,) * len(shape)
    return pl.BlockSpec(shape, lambda *_: zeros, pipeline_mode=pl.Buffered(1))


def _ada_kernel(c_ref, w_ref, b_ref, o_ref):
    c = c_ref[...]
    c_act = c * _sigmoid(c)
    o_ref[...] = jnp.dot(c_act, w_ref[...], preferred_element_type=F32,
                         precision=lax.Precision.HIGHEST) + b_ref[...]


def _ada(c, w_ada, b_ada):
    n_out = N_MOD * D_MODEL
    blk = 1536
    return pl.pallas_call(
        _ada_kernel,
        out_shape=jax.ShapeDtypeStruct((BATCH, n_out), F32),
        grid=(n_out // blk,),
        in_specs=[pl.BlockSpec((BATCH, D_MODEL), lambda j: (0, 0)),
                  pl.BlockSpec((D_MODEL, blk), lambda j: (0, j)),
                  pl.BlockSpec((1, blk), lambda j: (0, j))],
        out_specs=pl.BlockSpec((BATCH, blk), lambda j: (0, j)),
        compiler_params=pltpu.CompilerParams(vmem_limit_bytes=VMEM_LIMIT),
        name="ada",
    )(c, w_ada, b_ada.reshape(1, n_out))


def _s5prep_kernel(lr_ref, li_ref, ldt_ref, br_ref, bi_ref, ar_ref, ai_ref, bbr_ref, bbi_ref):
    lr = lr_ref[...]
    li = li_ref[...]
    dt = jnp.exp(ldt_ref[...])
    mag = jnp.exp(lr * dt)
    abar_r = mag * jnp.cos(li * dt)
    abar_i = mag * jnp.sin(li * dt)
    den = lr * lr + li * li
    nr = abar_r - 1.0
    ni = abar_i
    k_r = (nr * lr + ni * li) / den
    k_i = (ni * lr - nr * li) / den
    br = br_ref[...]
    bi = bi_ref[...]
    ar_ref[...] = abar_r
    ai_ref[...] = abar_i
    bbr_ref[...] = k_r * br - k_i * bi
    bbi_ref[...] = k_r * bi + k_i * br


def _s5prep(lam_re, lam_im, log_dt, b_re, b_im):
    rows = S5_GROUPS * S5_GROUP
    rep = lambda a: jnp.broadcast_to(a[:, None, :], (S5_GROUPS, S5_GROUP, S5_STATE)).reshape(rows, S5_STATE)
    lr = rep(lam_re)
    li = rep(lam_im)
    ldt = rep(jnp.broadcast_to(log_dt[:, None], (S5_GROUPS, S5_STATE)))
    br = jnp.transpose(b_re, (0, 2, 1)).reshape(rows, S5_STATE)
    bi = jnp.transpose(b_im, (0, 2, 1)).reshape(rows, S5_STATE)
    shp = jax.ShapeDtypeStruct((rows, S5_STATE), F32)
    return pl.pallas_call(_s5prep_kernel, out_shape=(shp, shp, shp, shp), name="s5prep")(lr, li, ldt, br, bi)


def _block_diag(blocks):
    g, r, c = blocks.shape
    eye = jnp.eye(g, dtype=blocks.dtype)
    return jnp.einsum("grc,gk->grkc", blocks, eye).reshape(g * r, g * c)


def _gelu_tanh(x):
    sqrt_2_over_pi = 0.7978845608028654
    cdf = 0.5 * (1.0 + jnp.tanh(sqrt_2_over_pi * (x + 0.044715 * (x * x * x))))
    return x * cdf


def _rms_mod(x, g, scale, shift):
    ms = jnp.mean(x * x, axis=-1, keepdims=True)
    h = x * lax.rsqrt(ms + EPS) * g
    return _per_batch(h, (scale, shift), lambda r, m: r * (1.0 + m[0]) + m[1])


def _tok_kernel(x_ref, sh1_ref, sc1_ref, gt1_ref, sh2_ref, sc2_ref, gt2_ref, g1_ref, g2_ref,
                wu_ref, wcv_ref, wcg_ref, wa_ref, wb_ref,
                bsp_ref, csr_ref, csi_ref, ar_ref, ai_ref, d_ref, wv_ref, wg_ref,
                dww_ref, dwb_ref, lng_ref, lnb_ref, pw_ref, wout_ref,
                rw_ref, rb_ref, sgu_ref, sd_ref,
                x1s_ref, h2_ref, kmap_ref, wc_ref, cnt_ref,
                bu_ref, sr_ref, si_ref, cbuf_ref, cv_ref):
    step = pl.program_id(0)

    @pl.when(step == 0)
    def _():
        sr_ref[...] = jnp.zeros_like(sr_ref)
        si_ref[...] = jnp.zeros_like(si_ref)
        cbuf_ref[0:CONV_HALO, :] = jnp.zeros((CONV_HALO, CONV_WIDTH), F32)

    x = _time_major(x_ref[...])
    h = _rms_mod(x, g1_ref[...], sc1_ref[...], sh1_ref[...]).astype(BF16)
    u = jnp.dot(h, wu_ref[...], preferred_element_type=F32)
    u_b = u.astype(BF16)
    zc = (jnp.dot(h, wcv_ref[...], preferred_element_type=F32)
          * _sigmoid(jnp.dot(h, wcg_ref[...], preferred_element_type=F32)))

    for j in range(N_STATE // LANES):
        kb = (j * LANES // S5_STATE * S5_GROUP) // LANES
        res = jnp.dot(u_b[:, kb * LANES:(kb + 1) * LANES], bsp_ref[j], preferred_element_type=F32)
        bu_ref[:, j * LANES:(j + 1) * LANES] = res[:, :LANES]
        bu_ref[:, N_STATE + j * LANES:N_STATE + (j + 1) * LANES] = res[:, LANES:]

    gate_a = _sigmoid(jnp.dot(h, wa_ref[...], preferred_element_type=F32))
    gate_b = _sigmoid(jnp.dot(h, wb_ref[...], preferred_element_type=F32))

    for lg in range(N_STATE // SCAN_LANES):
        lo = lg * SCAN_LANES
        re = slice(lo, lo + SCAN_LANES)
        im = slice(N_STATE + lo, N_STATE + lo + SCAN_LANES)
        a_r = jnp.broadcast_to(ar_ref[:, re], (SUBLANES, SCAN_LANES))
        a_i = jnp.broadcast_to(ai_ref[:, re], (SUBLANES, SCAN_LANES))
        s_r = sr_ref[:, re]
        s_i = si_ref[:, re]
        for t in range(TILE_STEPS):
            rows = slice(t * SUBLANES, (t + 1) * SUBLANES)
            s_r, s_i = (a_r * s_r - a_i * s_i + bu_ref[rows, re],
                        a_r * s_i + a_i * s_r + bu_ref[rows, im])
            bu_ref[rows, re] = s_r
            bu_ref[rows, im] = s_i
        sr_ref[:, re] = s_r
        si_ref[:, re] = s_i

    half_s = N_STATE // 2
    ys = []
    for hf in range(2):
        s_re = bu_ref[:, hf * half_s:(hf + 1) * half_s].astype(BF16)
        s_im = bu_ref[:, N_STATE + hf * half_s:N_STATE + (hf + 1) * half_s].astype(BF16)
        ys.append(jnp.dot(s_re, csr_ref[hf], preferred_element_type=F32)
                  - jnp.dot(s_im, csi_ref[hf], preferred_element_type=F32))
    y = jnp.concatenate(ys, axis=-1) + d_ref[...] * u
    z = _gelu_tanh(y).astype(BF16)
    y_a = (jnp.dot(z, wv_ref[...], preferred_element_type=F32)
           * _sigmoid(jnp.dot(z, wg_ref[...], preferred_element_type=F32)))

    cbuf_ref[CONV_HALO:CONV_HALO + TILE_ROWS, :] = zc
    for ci in range(TILE_ROWS // CONV_CHUNK):
        r0 = ci * CONV_CHUNK
        for lt in range(CONV_WIDTH // LANES):
            ls = slice(lt * LANES, (lt + 1) * LANES)
            win = cbuf_ref[r0:r0 + CONV_CHUNK + CONV_HALO, ls]
            acc = jnp.broadcast_to(dwb_ref[:, ls], (CONV_CHUNK, LANES))
            for k in range(CONV_KERNEL):
                acc = acc + dww_ref[k:k + 1, ls] * win[k * BATCH:k * BATCH + CONV_CHUNK]
            cv_ref[r0:r0 + CONV_CHUNK, ls] = acc
    cbuf_ref[0:CONV_HALO, :] = cbuf_ref[TILE_ROWS:TILE_ROWS + CONV_HALO, :]

    cv = cv_ref[...]
    mu = jnp.mean(cv, axis=-1, keepdims=True)
    var = jnp.mean(jnp.square(cv - mu), axis=-1, keepdims=True)
    ln = (cv - mu) * lax.rsqrt(var + EPS) * lng_ref[...] + lnb_ref[...]
    zb = (ln * _sigmoid(ln)).astype(BF16)
    y_b = jnp.dot(zb, pw_ref[...], preferred_element_type=F32)

    m = gate_a * y_a + gate_b * y_b
    mixed = jnp.dot(m.astype(BF16), wout_ref[...], preferred_element_type=F32)

    x1 = x + _per_batch(mixed, gt1_ref[...], lambda r, m: r * m)
    hb = _rms_mod(x1, g2_ref[...], sc2_ref[...], sh2_ref[...]).astype(BF16)
    _store_slabs(h2_ref, 0, _pack_rows(hb.astype(F32)))

    scores = _sigmoid(lax.dot_general(rw_ref[...], hb, (((1,), (1,)), ((), ())),
                                      preferred_element_type=F32))
    expert = lax.broadcasted_iota(I32, scores.shape, 0)
    neg_inf = jnp.float32(-jnp.inf)
    work = scores + rb_ref[...]
    kmap = jnp.zeros(scores.shape, I32)
    for k in range(TOP_K):
        top = jnp.max(work, axis=0, keepdims=True)
        idx = jnp.min(jnp.where(work == top, expert, N_EXPERTS), axis=0, keepdims=True)
        hit = expert == idx
        kmap = jnp.where(hit, k + 1, kmap)
        work = jnp.where(hit, neg_inf, work)
    sel = kmap > 0
    w_sel = jnp.where(sel, scores, 0.0)
    kmap_ref[...] = kmap
    wc_ref[...] = w_sel / jnp.sum(w_sel, axis=0, keepdims=True) * ROUTED_SCALE

    @pl.when(step % (MOE_BLOCK // TILE_ROWS) == 0)
    def _():
        cnt_ref[...] = jnp.zeros_like(cnt_ref)

    cnt_ref[...] += jnp.sum(sel.astype(F32), axis=1, keepdims=True)

    gu = jnp.dot(hb, sgu_ref[...], preferred_element_type=F32)
    g = gu[:, :SHARED_HIDDEN]
    act = (g * _sigmoid(g)) * gu[:, SHARED_HIDDEN:]
    shared = jnp.dot(act.astype(BF16), sd_ref[...], preferred_element_type=F32)
    x1s_ref[...] = x1 + _per_batch(shared, gt2_ref[...], lambda r, m: r * m)


def _block_diag_tiles(bmat, c_r, c_i):
    tiles = []
    for j in range(N_STATE // LANES):
        kb = (j * LANES // S5_STATE * S5_GROUP) // LANES
        k_rows = slice(kb * LANES, (kb + 1) * LANES)
        tiles.append(jnp.concatenate([bmat[k_rows, j * LANES:(j + 1) * LANES],
                                      bmat[k_rows, N_STATE + j * LANES:N_STATE + (j + 1) * LANES]], axis=1))
    half_c, half_s = S5_WIDTH // 2, N_STATE // 2
    halves = lambda m: jnp.stack([m[hf * half_s:(hf + 1) * half_s, hf * half_c:(hf + 1) * half_c]
                                  for hf in range(2)])
    return jnp.stack(tiles), halves(c_r), halves(c_i)


def _tok(x, mods, consts):
    ts = TILE_ROWS
    tiles_per_block = MOE_BLOCK // ts
    row_spec = lambda rows, c: pl.BlockSpec((rows, c), lambda i: (i, 0))
    route_spec = pl.BlockSpec((N_EXPERTS, ts), lambda i: (0, i))
    return pl.pallas_call(
        _tok_kernel,
        out_shape=(jax.ShapeDtypeStruct((N_TOK, D_MODEL), F32),
                   jax.ShapeDtypeStruct((N_TOK * SLAB, LANES), U32),
                   jax.ShapeDtypeStruct((N_EXPERTS, N_TOK), I32),
                   jax.ShapeDtypeStruct((N_EXPERTS, N_TOK), F32),
                   jax.ShapeDtypeStruct((MOE_BLOCKS, N_EXPERTS, 1), F32)),
        grid=(SEQ // TILE_STEPS,),
        in_specs=[_X_SPEC] + [_const_spec(a.shape) for a in tuple(mods) + tuple(consts)],
        out_specs=(row_spec(ts, D_MODEL), row_spec(ts * SLAB, LANES), route_spec, route_spec,
                   pl.BlockSpec((None, N_EXPERTS, 1), lambda i: (i // tiles_per_block, 0, 0))),
        scratch_shapes=[pltpu.VMEM((ts, 2 * N_STATE), F32),
                        pltpu.VMEM((SUBLANES, N_STATE), F32),
                        pltpu.VMEM((SUBLANES, N_STATE), F32),
                        pltpu.VMEM((CONV_HALO + ts, CONV_WIDTH), F32),
                        pltpu.VMEM((ts, CONV_WIDTH), F32)],
        compiler_params=pltpu.CompilerParams(
            dimension_semantics=("arbitrary",), vmem_limit_bytes=VMEM_LIMIT),
        name="tok",
    )(x, *mods, *consts)


def _slots_kernel(kmap_ref, wc_ref, off_ref, tri_ref, dst_ref, wk_ref, seen_ref):
    @pl.when(pl.program_id(0) % (MOE_BLOCK // TILE_ROWS) == 0)
    def _():
        seen_ref[...] = jnp.zeros_like(seen_ref)

    kmap = kmap_ref[...]
    wc = wc_ref[...]
    sel = (kmap > 0).astype(F32)
    seen = seen_ref[...]
    slot = jnp.dot(sel.astype(BF16), tri_ref[...], preferred_element_type=F32) + (seen + off_ref[...])
    seen_ref[...] = seen + jnp.sum(sel, axis=1, keepdims=True)
    dst, w_k = [], []
    for k in range(TOP_K):
        hit = kmap == k + 1
        dst.append(jnp.sum(jnp.where(hit, slot, 0.0), axis=0, keepdims=True))
        w_k.append(jnp.sum(jnp.where(hit, wc, 0.0), axis=0, keepdims=True))
    dst_ref[...] = jnp.concatenate(dst, axis=0).astype(I32) * SLAB
    wk_ref[...] = jnp.concatenate(w_k, axis=0)


def _slots(kmap, wc, off):
    ts = TILE_ROWS
    tiles_per_block = MOE_BLOCK // ts
    route_spec = pl.BlockSpec((N_EXPERTS, ts), lambda i: (0, i))
    slot_spec = pl.BlockSpec((TOP_K, ts), lambda i: (0, i))
    tri = (lax.broadcasted_iota(I32, (ts, ts), 0) < lax.broadcasted_iota(I32, (ts, ts), 1)).astype(BF16)
    return pl.pallas_call(
        _slots_kernel,
        out_shape=(jax.ShapeDtypeStruct((TOP_K, N_TOK), I32), jax.ShapeDtypeStruct((TOP_K, N_TOK), F32)),
        grid=(N_TOK // ts,),
        in_specs=[route_spec, route_spec,
                  pl.BlockSpec((None, N_EXPERTS, 1), lambda i: (i // tiles_per_block, 0, 0)),
                  _const_spec(tri.shape)],
        out_specs=(slot_spec, slot_spec),
        scratch_shapes=[pltpu.VMEM((N_EXPERTS, 1), F32)],
        compiler_params=pltpu.CompilerParams(
            dimension_semantics=("arbitrary",), vmem_limit_bytes=VMEM_LIMIT),
        name="slots",
    )(kmap, wc, off, tri)


def _moe_kernel(cnt_ref, off_ref, dst_ref, w_ref, h_ref, wg_ref, wu_ref, wd_ref, o_ref, stage_ref):
    blk = pl.program_id(0)
    step = pl.program_id(1)

    @pl.when(jnp.logical_and(blk == 0, step == 0))
    def _():
        stage_ref[TOP_K * MOE_BLOCK * SLAB:, :] = jnp.zeros((MOE_CHUNK * SLAB, LANES), U32)

    def slab_at(ref, sublane_row):
        return ref.at[pl.ds(pl.multiple_of(sublane_row, SLAB), SLAB), :]

    group = MOE_UNROLL * TOP_K

    def selections(ref, i):
        return ref.at[0, pl.ds(pl.multiple_of(i * group, group), group)]

    @pl.when(step == 0)
    def _():
        def dispatch(i, carry):
            dst = selections(dst_ref, i)
            for j in range(MOE_UNROLL):
                slab = slab_at(h_ref, (i * MOE_UNROLL + j) * SLAB)[...]
                for k in range(TOP_K):
                    slab_at(stage_ref, dst[j * TOP_K + k])[...] = slab
            return carry

        lax.fori_loop(0, MOE_BLOCK // MOE_UNROLL, dispatch, 0)

    row = lax.broadcasted_iota(I32, (MOE_CHUNK, LANES), 0)

    def load_chunk(first, c):
        return _load_slabs(stage_ref, (first + c * MOE_CHUNK) * SLAB, MOE_CHUNK)

    def expert_mlp(j, words):
        halves = [_unpack_words(w) for w in words]
        xs = jnp.concatenate([lo for lo, _ in halves] + [hi for _, hi in halves], axis=-1)
        xs = xs.astype(BF16)
        g = jnp.dot(xs, wg_ref[j], preferred_element_type=F32)
        act = (g * _sigmoid(g)) * jnp.dot(xs, wu_ref[j], preferred_element_type=F32)
        return _pack_rows(jnp.dot(act.astype(BF16), wd_ref[j], preferred_element_type=F32))

    def store_chunk(first, n_rows, c, words, out):
        base = (first + c * MOE_CHUNK) * SLAB
        live = row < (n_rows - c * MOE_CHUNK)
        for s in range(SLAB):
            merged = jnp.where(live, out[:, s * LANES:(s + 1) * LANES], words[s])
            stage_ref[pl.ds(base + s, MOE_CHUNK, stride=SLAB), :] = merged

    @pl.when(jnp.logical_and(step >= 1, step <= MOE_EXPERT_STEPS))
    def _():
        experts = [(step - 1) * MOE_EXPERTS_PER_STEP + j for j in range(MOE_EXPERTS_PER_STEP)]
        counts = [cnt_ref[blk * N_EXPERTS + e] for e in experts]
        firsts = [off_ref[blk * N_EXPERTS + e] for e in experts]
        words = [load_chunk(firsts[j], 0) for j in range(MOE_EXPERTS_PER_STEP)]
        outs = [expert_mlp(j, words[j]) for j in range(MOE_EXPERTS_PER_STEP)]
        for j in range(MOE_EXPERTS_PER_STEP):
            store_chunk(firsts[j], counts[j], 0, words[j], outs[j])
        for j in range(MOE_EXPERTS_PER_STEP):
            def more(c, carry, j=j):
                w_c = load_chunk(firsts[j], c)
                store_chunk(firsts[j], counts[j], c, w_c, expert_mlp(j, w_c))
                return carry

            lax.fori_loop(1, (counts[j] + MOE_CHUNK - 1) // MOE_CHUNK, more, 0)

    @pl.when(step == MOE_EXPERT_STEPS + 1)
    def _():
        def combine(i, carry):
            dst = selections(dst_ref, i)
            wts = selections(w_ref, i)
            for j in range(MOE_UNROLL):
                t = i * MOE_UNROLL + j
                terms = []
                for k in range(TOP_K):
                    w = wts[j * TOP_K + k]
                    lo, hi = _unpack_words(slab_at(stage_ref, dst[j * TOP_K + k])[...])
                    terms.append((lo * w, hi * w))
                while len(terms) > 1:
                    terms = [(a[0] + b[0], a[1] + b[1]) for a, b in zip(terms[0::2], terms[1::2])]
                slab_at(o_ref, t * SLAB)[...] = pltpu.pack_elementwise(list(terms[0]), packed_dtype=BF16)
            return carry

        lax.fori_loop(0, MOE_BLOCK // MOE_UNROLL, combine, 0)


def _moe(cnt, off, dst, w, h2s, wg, wu, wd):
    expert = lambda b, s, *_: (jnp.clip(s - 1, 0, MOE_EXPERT_STEPS - 1), 0, 0)
    smem_spec = pl.BlockSpec((None, 1, TOP_K * MOE_BLOCK), lambda b, s, *_: (b, 0, 0), memory_space=pltpu.SMEM)
    return pl.pallas_call(
        _moe_kernel,
        out_shape=jax.ShapeDtypeStruct((N_TOK * SLAB, LANES), U32),
        grid_spec=pltpu.PrefetchScalarGridSpec(
            num_scalar_prefetch=2,
            grid=(MOE_BLOCKS, MOE_EXPERT_STEPS + 2),
            in_specs=[smem_spec, smem_spec,
                      pl.BlockSpec((MOE_BLOCK * SLAB, LANES), lambda b, s, *_: (b, 0),
                                   pipeline_mode=pl.Buffered(1)),
                      pl.BlockSpec((MOE_EXPERTS_PER_STEP, D_MODEL, EXPERT_HIDDEN), expert),
                      pl.BlockSpec((MOE_EXPERTS_PER_STEP, D_MODEL, EXPERT_HIDDEN), expert),
                      pl.BlockSpec((MOE_EXPERTS_PER_STEP, EXPERT_HIDDEN, D_MODEL), expert)],
            out_specs=pl.BlockSpec((MOE_BLOCK * SLAB, LANES), lambda b, s, *_: (b, 0)),
            scratch_shapes=[pltpu.VMEM((STAGE_ROWS * SLAB, LANES), U32)]),
        compiler_params=pltpu.CompilerParams(
            dimension_semantics=("arbitrary", "arbitrary"), vmem_limit_bytes=VMEM_LIMIT),
        name="moe",
    )(cnt, off, dst, w, h2s, wg, wu, wd)


def _fin_kernel(x1s_ref, r_ref, gt2_ref, fg_ref, o_ref):
    halves = [_unpack_words(w) for w in _load_slabs(r_ref, 0, TILE_ROWS)]
    routed = jnp.concatenate([lo for lo, _ in halves] + [hi for _, hi in halves], axis=-1)
    x2 = x1s_ref[...] + _per_batch(routed, gt2_ref[...], lambda r, m: r * m)
    ms = jnp.mean(x2 * x2, axis=-1, keepdims=True)
    o_ref[...] = _batch_major(x2 * lax.rsqrt(ms + EPS) * fg_ref[...])


def _fin(x1s, routed, gt2, fg):
    ts = TILE_ROWS
    return pl.pallas_call(
        _fin_kernel,
        out_shape=jax.ShapeDtypeStruct((BATCH, SEQ, D_MODEL), F32),
        grid=(SEQ // TILE_STEPS,),
        in_specs=[pl.BlockSpec((ts, D_MODEL), lambda i: (i, 0)),
                  pl.BlockSpec((ts * SLAB, LANES), lambda i: (i, 0)),
                  _const_spec((1, BATCH, D_MODEL)),
                  _const_spec((1, D_MODEL))],
        out_specs=_X_SPEC,
        compiler_params=pltpu.CompilerParams(
            dimension_semantics=("parallel",), vmem_limit_bytes=VMEM_LIMIT),
        name="fin",
    )(x1s, routed, gt2, fg)


def kernel(x, c, w_ada, b_ada, norm1_g, w_in, s5_lambda_re, s5_lambda_im, s5_log_dt, s5_b_re, s5_b_im, s5_c_re, s5_c_im, s5_d, s5_glu_wv, s5_glu_wg, conv_dw_w, conv_dw_b, conv_ln_g, conv_ln_b, conv_pw_w, w_out, norm2_g, router_w, router_bias, exp_w_gate, exp_w_up, exp_w_down, shared_w_gate, shared_w_up, shared_w_down, final_norm_g):
    l = 0
    row = lambda a: a.reshape(1, -1)

    mod = _ada(c, w_ada[l], b_ada[l])
    mods = [m.reshape(1, BATCH, D_MODEL) for m in jnp.split(mod, N_MOD, axis=-1)]

    abar_r, abar_i, bbar_r, bbar_i = _s5prep(s5_lambda_re[l], s5_lambda_im[l], s5_log_dt[l],
                                             s5_b_re[l], s5_b_im[l])
    per_group = lambda a: a.reshape(S5_GROUPS, S5_GROUP, S5_STATE)
    a_r = per_group(abar_r)[:, 0, :].reshape(1, N_STATE)
    a_i = per_group(abar_i)[:, 0, :].reshape(1, N_STATE)
    bmat = jnp.concatenate([_block_diag(per_group(bbar_r)), _block_diag(per_group(bbar_i))],
                           axis=1).astype(BF16)
    c_r = _block_diag(jnp.transpose(s5_c_re[l], (0, 2, 1))).astype(BF16)
    c_i = _block_diag(jnp.transpose(s5_c_im[l], (0, 2, 1))).astype(BF16)

    o1 = S5_WIDTH
    o2 = o1 + CONV_WIDTH
    o3 = o2 + CONV_WIDTH
    o4 = o3 + D_MODEL
    w = w_in[l].astype(BF16)
    sgu = jnp.concatenate([shared_w_gate[l], shared_w_up[l]], axis=1).astype(BF16)
    consts = (row(norm1_g[l]), row(norm2_g[l]),
              w[:, :o1], w[:, o1:o2], w[:, o2:o3], w[:, o3:o4], w[:, o4:],
              *_block_diag_tiles(bmat, c_r, c_i), a_r, a_i, row(s5_d[l]),
              s5_glu_wv[l].astype(BF16), s5_glu_wg[l].astype(BF16),
              conv_dw_w[l], row(conv_dw_b[l]), row(conv_ln_g[l]), row(conv_ln_b[l]),
              conv_pw_w[l].astype(BF16), w_out[l].astype(BF16),
              router_w[l].T.astype(BF16), router_bias[l].reshape(N_EXPERTS, 1),
              sgu, shared_w_down[l].astype(BF16))
    x1s, h2s, kmap, wc, cnt = _tok(x, mods, consts)

    off = jnp.cumsum(cnt, axis=1) - cnt
    dst, w_k = _slots(kmap, wc, off)
    as_ints = lambda a: a.astype(I32).reshape(-1)
    per_block = lambda a: a.T.reshape(MOE_BLOCKS, 1, MOE_BLOCK * TOP_K)

    routed = _moe(as_ints(cnt), as_ints(off), per_block(dst), per_block(w_k), h2s,
                  exp_w_gate[l].astype(BF16), exp_w_up[l].astype(BF16), exp_w_down[l].astype(BF16))
    return _fin(x1s, routed, mods[5], row(final_norm_g))
```

```python
import jax
import jax.numpy as jnp
from jax import lax
from jax.experimental import pallas as pl
from jax.experimental.pallas import tpu as pltpu

D_MODEL = 1024
BATCH = 8
SEQ = 4096
N_TOK = BATCH * SEQ
S5_WIDTH = 512
S5_GROUP = 16
S5_GROUPS = 32
S5_STATE = 64
N_STATE = S5_GROUPS * S5_STATE
CONV_WIDTH = 512
CONV_KERNEL = 31
N_EXPERTS = 64
TOP_K = 8
EXPERT_HIDDEN = 256
SHARED_HIDDEN = 256
ROUTED_SCALE = 2.5
N_MOD = 6
EPS = 1e-6

LANES = 128
SUBLANES = 8
VMEM_LIMIT = 56 * 1024 * 1024

TILE_STEPS = 64
TILE_ROWS = TILE_STEPS * BATCH
CONV_HALO = (CONV_KERNEL - 1) * BATCH
CONV_CHUNK = 64
SCAN_LANES = 512
MOE_BLOCK = 2048
MOE_BLOCKS = N_TOK // MOE_BLOCK
MOE_CHUNK = 288
MOE_EXPERTS_PER_STEP = 2
MOE_EXPERT_STEPS = N_EXPERTS // MOE_EXPERTS_PER_STEP
MOE_UNROLL = 4
SLAB = 4
STAGE_ROWS = TOP_K * MOE_BLOCK + MOE_CHUNK

F32 = jnp.float32
BF16 = jnp.bfloat16
U32 = jnp.uint32
I32 = jnp.int32


_X_SPEC = pl.BlockSpec((BATCH, TILE_STEPS, D_MODEL), lambda i: (0, i, 0))


def _time_major(a):
    return jnp.swapaxes(a, 0, 1).reshape(a.shape[0] * a.shape[1], a.shape[2])


def _batch_major(a):
    return jnp.swapaxes(a.reshape(a.shape[0] // BATCH, BATCH, a.shape[1]), 0, 1)


def _per_batch(rows, mod, op):
    r3 = rows.reshape(rows.shape[0] // BATCH, BATCH, rows.shape[1])
    return op(r3, mod).reshape(rows.shape)


def _pack_rows(x):
    half = D_MODEL // 2
    return pltpu.pack_elementwise([x[:, :half], x[:, half:]], packed_dtype=BF16)


def _unpack_words(w):
    lo = pltpu.unpack_elementwise(w, index=0, packed_dtype=BF16, unpacked_dtype=F32)
    hi = pltpu.unpack_elementwise(w, index=1, packed_dtype=BF16, unpacked_dtype=F32)
    return lo, hi


def _store_slabs(ref, base, words):
    rows = words.shape[0]
    for s in range(SLAB):
        ref[pl.ds(base + s, rows, stride=SLAB), :] = words[:, s * LANES:(s + 1) * LANES]


def _load_slabs(ref, base, rows):
    return [ref[pl.ds(base + s, rows, stride=SLAB), :] for s in range(SLAB)]


def _sigmoid(x):
    return jax.nn.sigmoid(x)


def _const_spec(shape):
    zeros = (0,) * len(shape)
    return pl.BlockSpec(shape, lambda *_: zeros, pipeline_mode=pl.Buffered(1))


def _ada_kernel(c_ref, w_ref, b_ref, o_ref):
    c = c_ref[...]
    c_act = c * _sigmoid(c)
    o_ref[...] = jnp.dot(c_act, w_ref[...], preferred_element_type=F32,
                         precision=lax.Precision.HIGHEST) + b_ref[...]


def _ada(c, w_ada, b_ada):
    n_out = N_MOD * D_MODEL
    blk = 1536
    return pl.pallas_call(
        _ada_kernel,
        out_shape=jax.ShapeDtypeStruct((BATCH, n_out), F32),
        grid=(n_out // blk,),
        in_specs=[pl.BlockSpec((BATCH, D_MODEL), lambda j: (0, 0)),
                  pl.BlockSpec((D_MODEL, blk), lambda j: (0, j)),
                  pl.BlockSpec((1, blk), lambda j: (0, j))],
        out_specs=pl.BlockSpec((BATCH, blk), lambda j: (0, j)),
        compiler_params=pltpu.CompilerParams(vmem_limit_bytes=VMEM_LIMIT),
        name="ada",
    )(c, w_ada, b_ada.reshape(1, n_out))


def _s5prep_kernel(lr_ref, li_ref, ldt_ref, br_ref, bi_ref, ar_ref, ai_ref, bbr_ref, bbi_ref):
    lr = lr_ref[...]
    li = li_ref[...]
    dt = jnp.exp(ldt_ref[...])
    mag = jnp.exp(lr * dt)
    abar_r = mag * jnp.cos(li * dt)
    abar_i = mag * jnp.sin(li * dt)
    den = lr * lr + li * li
    nr = abar_r - 1.0
    ni = abar_i
    k_r = (nr * lr + ni * li) / den
    k_i = (ni * lr - nr * li) / den
    br = br_ref[...]
    bi = bi_ref[...]
    ar_ref[...] = abar_r
    ai_ref[...] = abar_i
    bbr_ref[...] = k_r * br - k_i * bi
    bbi_ref[...] = k_r * bi + k_i * br


def _s5prep(lam_re, lam_im, log_dt, b_re, b_im):
    rows = S5_GROUPS * S5_GROUP
    rep = lambda a: jnp.broadcast_to(a[:, None, :], (S5_GROUPS, S5_GROUP, S5_STATE)).reshape(rows, S5_STATE)
    lr = rep(lam_re)
    li = rep(lam_im)
    ldt = rep(jnp.broadcast_to(log_dt[:, None], (S5_GROUPS, S5_STATE)))
    br = jnp.transpose(b_re, (0, 2, 1)).reshape(rows, S5_STATE)
    bi = jnp.transpose(b_im, (0, 2, 1)).reshape(rows, S5_STATE)
    shp = jax.ShapeDtypeStruct((rows, S5_STATE), F32)
    return pl.pallas_call(_s5prep_kernel, out_shape=(shp, shp, shp, shp), name="s5prep")(lr, li, ldt, br, bi)


def _block_diag(blocks):
    g, r, c = blocks.shape
    eye = jnp.eye(g, dtype=blocks.dtype)
    return jnp.einsum("grc,gk->grkc", blocks, eye).reshape(g * r, g * c)


def _gelu_tanh(x):
    sqrt_2_over_pi = 0.7978845608028654
    cdf = 0.5 * (1.0 + jnp.tanh(sqrt_2_over_pi * (x + 0.044715 * (x * x * x))))
    return x * cdf


def _rms_mod(x, g, scale, shift):
    ms = jnp.mean(x * x, axis=-1, keepdims=True)
    h = x * lax.rsqrt(ms + EPS) * g
    return _per_batch(h, (scale, shift), lambda r, m: r * (1.0 + m[0]) + m[1])


def _tok_kernel(x_ref, sh1_ref, sc1_ref, gt1_ref, sh2_ref, sc2_ref, gt2_ref, g1_ref, g2_ref,
                wu_ref, wcv_ref, wcg_ref, wa_ref, wb_ref,
                bsp_ref, csr_ref, csi_ref, ar_ref, ai_ref, d_ref, wv_ref, wg_ref,
                dww_ref, dwb_ref, lng_ref, lnb_ref, pw_ref, wout_ref,
                rw_ref, rb_ref, sgu_ref, sd_ref,
                x1s_ref, h2_ref, kmap_ref, wc_ref, cnt_ref,
                bu_ref, sb_ref, sr_ref, si_ref, cbuf_ref, cv_ref):
    step = pl.program_id(0)

    @pl.when(step == 0)
    def _():
        sr_ref[...] = jnp.zeros_like(sr_ref)
        si_ref[...] = jnp.zeros_like(si_ref)
        cbuf_ref[0:CONV_HALO, :] = jnp.zeros((CONV_HALO, CONV_WIDTH), F32)

    x = _time_major(x_ref[...])
    h = _rms_mod(x, g1_ref[...], sc1_ref[...], sh1_ref[...]).astype(BF16)
    u = jnp.dot(h, wu_ref[...], preferred_element_type=F32)
    u_b = u.astype(BF16)
    zc = (jnp.dot(h, wcv_ref[...], preferred_element_type=F32)
          * _sigmoid(jnp.dot(h, wcg_ref[...], preferred_element_type=F32)))

    for j in range(N_STATE // LANES):
        kb = (j * LANES // S5_STATE * S5_GROUP) // LANES
        res = jnp.dot(u_b[:, kb * LANES:(kb + 1) * LANES], bsp_ref[j], preferred_element_type=F32)
        bu_ref[:, j * LANES:(j + 1) * LANES] = res[:, :LANES]
        bu_ref[:, N_STATE + j * LANES:N_STATE + (j + 1) * LANES] = res[:, LANES:]

    gate_a = _sigmoid(jnp.dot(h, wa_ref[...], preferred_element_type=F32))
    gate_b = _sigmoid(jnp.dot(h, wb_ref[...], preferred_element_type=F32))

    for lg in range(N_STATE // SCAN_LANES):
        lo = lg * SCAN_LANES
        re = slice(lo, lo + SCAN_LANES)
        im = slice(N_STATE + lo, N_STATE + lo + SCAN_LANES)
        a_r = jnp.broadcast_to(ar_ref[:, re], (SUBLANES, SCAN_LANES))
        a_i = jnp.broadcast_to(ai_ref[:, re], (SUBLANES, SCAN_LANES))
        s_r = sr_ref[:, re]
        s_i = si_ref[:, re]
        for t in range(0, TILE_STEPS, 2):
            pair_r, pair_i = [], []
            for tt in (t, t + 1):
                rows = slice(tt * SUBLANES, (tt + 1) * SUBLANES)
                s_r, s_i = (a_r * s_r - a_i * s_i + bu_ref[rows, re],
                            a_r * s_i + a_i * s_r + bu_ref[rows, im])
                pair_r.append(s_r)
                pair_i.append(s_i)
            rows2 = slice(t * SUBLANES, (t + 2) * SUBLANES)
            sb_ref[rows2, re] = jnp.concatenate(pair_r, axis=0).astype(BF16)
            sb_ref[rows2, im] = jnp.concatenate(pair_i, axis=0).astype(BF16)
        sr_ref[:, re] = s_r
        si_ref[:, re] = s_i

    half_s = N_STATE // 2
    ys = []
    for hf in range(2):
        s_re = sb_ref[:, hf * half_s:(hf + 1) * half_s]
        s_im = sb_ref[:, N_STATE + hf * half_s:N_STATE + (hf + 1) * half_s]
        ys.append(jnp.dot(s_re, csr_ref[hf], preferred_element_type=F32)
                  - jnp.dot(s_im, csi_ref[hf], preferred_element_type=F32))
    y = jnp.concatenate(ys, axis=-1) + d_ref[...] * u
    z = _gelu_tanh(y).astype(BF16)
    y_a = (jnp.dot(z, wv_ref[...], preferred_element_type=F32)
           * _sigmoid(jnp.dot(z, wg_ref[...], preferred_element_type=F32)))

    cbuf_ref[CONV_HALO:CONV_HALO + TILE_ROWS, :] = zc
    for ci in range(TILE_ROWS // CONV_CHUNK):
        r0 = ci * CONV_CHUNK
        for lt in range(CONV_WIDTH // LANES):
            ls = slice(lt * LANES, (lt + 1) * LANES)
            win = cbuf_ref[r0:r0 + CONV_CHUNK + CONV_HALO, ls]
            acc = jnp.broadcast_to(dwb_ref[:, ls], (CONV_CHUNK, LANES))
            for k in range(CONV_KERNEL):
                acc = acc + dww_ref[k:k + 1, ls] * win[k * BATCH:k * BATCH + CONV_CHUNK]
            cv_ref[r0:r0 + CONV_CHUNK, ls] = acc
    cbuf_ref[0:CONV_HALO, :] = cbuf_ref[TILE_ROWS:TILE_ROWS + CONV_HALO, :]

    cv = cv_ref[...]
    mu = jnp.mean(cv, axis=-1, keepdims=True)
    var = jnp.mean(jnp.square(cv - mu), axis=-1, keepdims=True)
    ln = (cv - mu) * lax.rsqrt(var + EPS) * lng_ref[...] + lnb_ref[...]
    zb = (ln * _sigmoid(ln)).astype(BF16)
    y_b = jnp.dot(zb, pw_ref[...], preferred_element_type=F32)

    m = gate_a * y_a + gate_b * y_b
    mixed = jnp.dot(m.astype(BF16), wout_ref[...], preferred_element_type=F32)

    x1 = x + _per_batch(mixed, gt1_ref[...], lambda r, m: r * m)
    hb = _rms_mod(x1, g2_ref[...], sc2_ref[...], sh2_ref[...]).astype(BF16)
    _store_slabs(h2_ref, 0, _pack_rows(hb.astype(F32)))

    scores = _sigmoid(lax.dot_general(rw_ref[...], hb, (((1,), (1,)), ((), ())),
                                      preferred_element_type=F32))
    expert = lax.broadcasted_iota(I32, scores.shape, 0)
    neg_inf = jnp.float32(-jnp.inf)
    work = scores + rb_ref[...]
    kmap = jnp.zeros(scores.shape, I32)
    for k in range(TOP_K):
        top = jnp.max(work, axis=0, keepdims=True)
        idx = jnp.min(jnp.where(work == top, expert, N_EXPERTS), axis=0, keepdims=True)
        hit = expert == idx
        kmap = jnp.where(hit, k + 1, kmap)
        work = jnp.where(hit, neg_inf, work)
    sel = kmap > 0
    w_sel = jnp.where(sel, scores, 0.0)
    kmap_ref[...] = kmap
    wc_ref[...] = w_sel / jnp.sum(w_sel, axis=0, keepdims=True) * ROUTED_SCALE

    @pl.when(step % (MOE_BLOCK // TILE_ROWS) == 0)
    def _():
        cnt_ref[...] = jnp.zeros_like(cnt_ref)

    cnt_ref[...] += jnp.sum(sel.astype(F32), axis=1, keepdims=True)

    gu = jnp.dot(hb, sgu_ref[...], preferred_element_type=F32)
    g = gu[:, :SHARED_HIDDEN]
    act = (g * _sigmoid(g)) * gu[:, SHARED_HIDDEN:]
    shared = jnp.dot(act.astype(BF16), sd_ref[...], preferred_element_type=F32)
    x1s_ref[...] = x1 + _per_batch(shared, gt2_ref[...], lambda r, m: r * m)


def _block_diag_tiles(bmat, c_r, c_i):
    tiles = []
    for j in range(N_STATE // LANES):
        kb = (j * LANES // S5_STATE * S5_GROUP) // LANES
        k_rows = slice(kb * LANES, (kb + 1) * LANES)
        tiles.append(jnp.concatenate([bmat[k_rows, j * LANES:(j + 1) * LANES],
                                      bmat[k_rows, N_STATE + j * LANES:N_STATE + (j + 1) * LANES]], axis=1))
    half_c, half_s = S5_WIDTH // 2, N_STATE // 2
    halves = lambda m: jnp.stack([m[hf * half_s:(hf + 1) * half_s, hf * half_c:(hf + 1) * half_c]
                                  for hf in range(2)])
    return jnp.stack(tiles), halves(c_r), halves(c_i)


def _tok(x, mods, consts):
    ts = TILE_ROWS
    tiles_per_block = MOE_BLOCK // ts
    row_spec = lambda rows, c: pl.BlockSpec((rows, c), lambda i: (i, 0))
    route_spec = pl.BlockSpec((N_EXPERTS, ts), lambda i: (0, i))
    return pl.pallas_call(
        _tok_kernel,
        out_shape=(jax.ShapeDtypeStruct((N_TOK, D_MODEL), F32),
                   jax.ShapeDtypeStruct((N_TOK * SLAB, LANES), U32),
                   jax.ShapeDtypeStruct((N_EXPERTS, N_TOK), I32),
                   jax.ShapeDtypeStruct((N_EXPERTS, N_TOK), F32),
                   jax.ShapeDtypeStruct((MOE_BLOCKS, N_EXPERTS, 1), F32)),
        grid=(SEQ // TILE_STEPS,),
        in_specs=[_X_SPEC] + [_const_spec(a.shape) for a in tuple(mods) + tuple(consts)],
        out_specs=(row_spec(ts, D_MODEL), row_spec(ts * SLAB, LANES), route_spec, route_spec,
                   pl.BlockSpec((None, N_EXPERTS, 1), lambda i: (i // tiles_per_block, 0, 0))),
        scratch_shapes=[pltpu.VMEM((ts, 2 * N_STATE), F32),
                        pltpu.VMEM((ts, 2 * N_STATE), BF16),
                        pltpu.VMEM((SUBLANES, N_STATE), F32),
                        pltpu.VMEM((SUBLANES, N_STATE), F32),
                        pltpu.VMEM((CONV_HALO + ts, CONV_WIDTH), F32),
                        pltpu.VMEM((ts, CONV_WIDTH), F32)],
        compiler_params=pltpu.CompilerParams(
            dimension_semantics=("arbitrary",), vmem_limit_bytes=VMEM_LIMIT),
        name="tok",
    )(x, *mods, *consts)


def _slots_kernel(kmap_ref, wc_ref, off_ref, tri_ref, dst_ref, wk_ref, seen_ref):
    @pl.when(pl.program_id(0) % (MOE_BLOCK // TILE_ROWS) == 0)
    def _():
        seen_ref[...] = jnp.zeros_like(seen_ref)

    kmap = kmap_ref[...]
    wc = wc_ref[...]
    sel = (kmap > 0).astype(F32)
    seen = seen_ref[...]
    slot = jnp.dot(sel.astype(BF16), tri_ref[...], preferred_element_type=F32) + (seen + off_ref[...])
    seen_ref[...] = seen + jnp.sum(sel, axis=1, keepdims=True)
    dst, w_k = [], []
    for k in range(TOP_K):
        hit = kmap == k + 1
        dst.append(jnp.sum(jnp.where(hit, slot, 0.0), axis=0, keepdims=True))
        w_k.append(jnp.sum(jnp.where(hit, wc, 0.0), axis=0, keepdims=True))
    dst_ref[...] = jnp.concatenate(dst, axis=0).astype(I32) * SLAB
    wk_ref[...] = jnp.concatenate(w_k, axis=0)


def _slots(kmap, wc, off):
    ts = TILE_ROWS
    tiles_per_block = MOE_BLOCK // ts
    route_spec = pl.BlockSpec((N_EXPERTS, ts), lambda i: (0, i))
    slot_spec = pl.BlockSpec((TOP_K, ts), lambda i: (0, i))
    tri = (lax.broadcasted_iota(I32, (ts, ts), 0) < lax.broadcasted_iota(I32, (ts, ts), 1)).astype(BF16)
    return pl.pallas_call(
        _slots_kernel,
        out_shape=(jax.ShapeDtypeStruct((TOP_K, N_TOK), I32), jax.ShapeDtypeStruct((TOP_K, N_TOK), F32)),
        grid=(N_TOK // ts,),
        in_specs=[route_spec, route_spec,
                  pl.BlockSpec((None, N_EXPERTS, 1), lambda i: (i // tiles_per_block, 0, 0)),
                  _const_spec(tri.shape)],
        out_specs=(slot_spec, slot_spec),
        scratch_shapes=[pltpu.VMEM((N_EXPERTS, 1), F32)],
        compiler_params=pltpu.CompilerParams(
            dimension_semantics=("arbitrary",), vmem_limit_bytes=VMEM_LIMIT),
        name="slots",
    )(kmap, wc, off, tri)


def _moe_kernel(cnt_ref, off_ref, dst_ref, w_ref, h_ref, wg_hbm, wu_hbm, wd_hbm, o_ref,
                stage_ref, wg_buf, wu_buf, wd_buf, sem):
    blk = pl.program_id(0)

    def weight_copies(step, slot):
        experts = pl.ds(step * MOE_EXPERTS_PER_STEP, MOE_EXPERTS_PER_STEP)
        pairs = ((wg_hbm, wg_buf), (wu_hbm, wu_buf), (wd_hbm, wd_buf))
        return [pltpu.make_async_copy(hbm.at[experts], buf.at[slot], sem.at[i, slot])
                for i, (hbm, buf) in enumerate(pairs)]

    def fetch(step, slot):
        for copy in weight_copies(step, slot):
            copy.start()

    def arrived(step, slot):
        for copy in weight_copies(step, slot):
            copy.wait()

    fetch(0, 0)

    @pl.when(blk == 0)
    def _():
        stage_ref[TOP_K * MOE_BLOCK * SLAB:, :] = jnp.zeros((MOE_CHUNK * SLAB, LANES), U32)

    def slab_at(ref, sublane_row):
        return ref.at[pl.ds(pl.multiple_of(sublane_row, SLAB), SLAB), :]

    group = MOE_UNROLL * TOP_K

    def selections(ref, i):
        return ref.at[0, pl.ds(pl.multiple_of(i * group, group), group)]

    def dispatch(i, carry):
        dst = selections(dst_ref, i)
        for j in range(MOE_UNROLL):
            slab = slab_at(h_ref, (i * MOE_UNROLL + j) * SLAB)[...]
            for k in range(TOP_K):
                slab_at(stage_ref, dst[j * TOP_K + k])[...] = slab
        return carry

    lax.fori_loop(0, MOE_BLOCK // MOE_UNROLL, dispatch, 0)

    row = lax.broadcasted_iota(I32, (MOE_CHUNK, LANES), 0)

    def load_chunk(first, c):
        return _load_slabs(stage_ref, (first + c * MOE_CHUNK) * SLAB, MOE_CHUNK)

    def expert_mlp(slot, j, words):
        halves = [_unpack_words(w) for w in words]
        xs = jnp.concatenate([lo for lo, _ in halves] + [hi for _, hi in halves], axis=-1)
        xs = xs.astype(BF16)
        g = jnp.dot(xs, wg_buf[slot, j], preferred_element_type=F32)
        act = (g * _sigmoid(g)) * jnp.dot(xs, wu_buf[slot, j], preferred_element_type=F32)
        return _pack_rows(jnp.dot(act.astype(BF16), wd_buf[slot, j], preferred_element_type=F32))

    def store_chunk(first, n_rows, c, words, out):
        base = (first + c * MOE_CHUNK) * SLAB
        live = row < (n_rows - c * MOE_CHUNK)
        for s in range(SLAB):
            merged = jnp.where(live, out[:, s * LANES:(s + 1) * LANES], words[s])
            stage_ref[pl.ds(base + s, MOE_CHUNK, stride=SLAB), :] = merged

    def run_step(step, slot):
        experts = [step * MOE_EXPERTS_PER_STEP + j for j in range(MOE_EXPERTS_PER_STEP)]
        counts = [cnt_ref[blk * N_EXPERTS + e] for e in experts]
        firsts = [off_ref[blk * N_EXPERTS + e] for e in experts]
        words = [load_chunk(firsts[j], 0) for j in range(MOE_EXPERTS_PER_STEP)]
        outs = [expert_mlp(slot, j, words[j]) for j in range(MOE_EXPERTS_PER_STEP)]
        for j in range(MOE_EXPERTS_PER_STEP):
            store_chunk(firsts[j], counts[j], 0, words[j], outs[j])
        for j in range(MOE_EXPERTS_PER_STEP):
            def more(c, carry, j=j):
                w_c = load_chunk(firsts[j], c)
                store_chunk(firsts[j], counts[j], c, w_c, expert_mlp(slot, j, w_c))
                return carry

            lax.fori_loop(1, (counts[j] + MOE_CHUNK - 1) // MOE_CHUNK, more, 0)

    def expert_step(step, carry):
        slot = step % 2

        @pl.when(step + 1 < MOE_EXPERT_STEPS)
        def _():
            fetch(step + 1, 1 - slot)

        arrived(step, slot)
        run_step(step, slot)
        return carry

    lax.fori_loop(0, MOE_EXPERT_STEPS, expert_step, 0)

    def combine(i, carry):
        dst = selections(dst_ref, i)
        wts = selections(w_ref, i)
        for j in range(MOE_UNROLL):
            t = i * MOE_UNROLL + j
            terms = []
            for k in range(TOP_K):
                w = wts[j * TOP_K + k]
                lo, hi = _unpack_words(slab_at(stage_ref, dst[j * TOP_K + k])[...])
                terms.append((lo * w, hi * w))
            while len(terms) > 1:
                terms = [(a[0] + b[0], a[1] + b[1]) for a, b in zip(terms[0::2], terms[1::2])]
            slab_at(o_ref, t * SLAB)[...] = pltpu.pack_elementwise(list(terms[0]), packed_dtype=BF16)
        return carry

    lax.fori_loop(0, MOE_BLOCK // MOE_UNROLL, combine, 0)


def _moe(cnt, off, dst, w, h2s, wg, wu, wd):
    smem_spec = pl.BlockSpec((None, 1, TOP_K * MOE_BLOCK), lambda b, *_: (b, 0, 0), memory_space=pltpu.SMEM)
    hbm_spec = pl.BlockSpec(memory_space=pl.ANY)
    weight_buf = lambda rows, cols: pltpu.VMEM((2, MOE_EXPERTS_PER_STEP, rows, cols), BF16)
    return pl.pallas_call(
        _moe_kernel,
        out_shape=jax.ShapeDtypeStruct((N_TOK * SLAB, LANES), U32),
        grid_spec=pltpu.PrefetchScalarGridSpec(
            num_scalar_prefetch=2,
            grid=(MOE_BLOCKS,),
            in_specs=[smem_spec, smem_spec,
                      pl.BlockSpec((MOE_BLOCK * SLAB, LANES), lambda b, *_: (b, 0),
                                   pipeline_mode=pl.Buffered(1)),
                      hbm_spec, hbm_spec, hbm_spec],
            out_specs=pl.BlockSpec((MOE_BLOCK * SLAB, LANES), lambda b, *_: (b, 0)),
            scratch_shapes=[pltpu.VMEM((STAGE_ROWS * SLAB, LANES), U32),
                            weight_buf(D_MODEL, EXPERT_HIDDEN), weight_buf(D_MODEL, EXPERT_HIDDEN),
                            weight_buf(EXPERT_HIDDEN, D_MODEL),
                            pltpu.SemaphoreType.DMA((3, 2))]),
        compiler_params=pltpu.CompilerParams(
            dimension_semantics=("arbitrary",), vmem_limit_bytes=VMEM_LIMIT),
        name="moe",
    )(cnt, off, dst, w, h2s, wg, wu, wd)


def _fin_kernel(x1s_ref, r_ref, gt2_ref, fg_ref, o_ref):
    halves = [_unpack_words(w) for w in _load_slabs(r_ref, 0, TILE_ROWS)]
    routed = jnp.concatenate([lo for lo, _ in halves] + [hi for _, hi in halves], axis=-1)
    x2 = x1s_ref[...] + _per_batch(routed, gt2_ref[...], lambda r, m: r * m)
    ms = jnp.mean(x2 * x2, axis=-1, keepdims=True)
    o_ref[...] = _batch_major(x2 * lax.rsqrt(ms + EPS) * fg_ref[...])


def _fin(x1s, routed, gt2, fg):
    ts = TILE_ROWS
    return pl.pallas_call(
        _fin_kernel,
        out_shape=jax.ShapeDtypeStruct((BATCH, SEQ, D_MODEL), F32),
        grid=(SEQ // TILE_STEPS,),
        in_specs=[pl.BlockSpec((ts, D_MODEL), lambda i: (i, 0)),
                  pl.BlockSpec((ts * SLAB, LANES), lambda i: (i, 0)),
                  _const_spec((1, BATCH, D_MODEL)),
                  _const_spec((1, D_MODEL))],
        out_specs=_X_SPEC,
        compiler_params=pltpu.CompilerParams(
            dimension_semantics=("parallel",), vmem_limit_bytes=VMEM_LIMIT),
        name="fin",
    )(x1s, routed, gt2, fg)


def kernel(x, c, w_ada, b_ada, norm1_g, w_in, s5_lambda_re, s5_lambda_im, s5_log_dt, s5_b_re, s5_b_im, s5_c_re, s5_c_im, s5_d, s5_glu_wv, s5_glu_wg, conv_dw_w, conv_dw_b, conv_ln_g, conv_ln_b, conv_pw_w, w_out, norm2_g, router_w, router_bias, exp_w_gate, exp_w_up, exp_w_down, shared_w_gate, shared_w_up, shared_w_down, final_norm_g):
    l = 0
    row = lambda a: a.reshape(1, -1)

    mod = _ada(c, w_ada[l], b_ada[l])
    mods = [m.reshape(1, BATCH, D_MODEL) for m in jnp.split(mod, N_MOD, axis=-1)]

    abar_r, abar_i, bbar_r, bbar_i = _s5prep(s5_lambda_re[l], s5_lambda_im[l], s5_log_dt[l],
                                             s5_b_re[l], s5_b_im[l])
    per_group = lambda a: a.reshape(S5_GROUPS, S5_GROUP, S5_STATE)
    a_r = per_group(abar_r)[:, 0, :].reshape(1, N_STATE)
    a_i = per_group(abar_i)[:, 0, :].reshape(1, N_STATE)
    bmat = jnp.concatenate([_block_diag(per_group(bbar_r)), _block_diag(per_group(bbar_i))],
                           axis=1).astype(BF16)
    c_r = _block_diag(jnp.transpose(s5_c_re[l], (0, 2, 1))).astype(BF16)
    c_i = _block_diag(jnp.transpose(s5_c_im[l], (0, 2, 1))).astype(BF16)

    o1 = S5_WIDTH
    o2 = o1 + CONV_WIDTH
    o3 = o2 + CONV_WIDTH
    o4 = o3 + D_MODEL
    w = w_in[l].astype(BF16)
    sgu = jnp.concatenate([shared_w_gate[l], shared_w_up[l]], axis=1).astype(BF16)
    consts = (row(norm1_g[l]), row(norm2_g[l]),
              w[:, :o1], w[:, o1:o2], w[:, o2:o3], w[:, o3:o4], w[:, o4:],
              *_block_diag_tiles(bmat, c_r, c_i), a_r, a_i, row(s5_d[l]),
              s5_glu_wv[l].astype(BF16), s5_glu_wg[l].astype(BF16),
              conv_dw_w[l], row(conv_dw_b[l]), row(conv_ln_g[l]), row(conv_ln_b[l]),
              conv_pw_w[l].astype(BF16), w_out[l].astype(BF16),
              router_w[l].T.astype(BF16), router_bias[l].reshape(N_EXPERTS, 1),
              sgu, shared_w_down[l].astype(BF16))
    x1s, h2s, kmap, wc, cnt = _tok(x, mods, consts)

    off = jnp.cumsum(cnt, axis=1) - cnt
    dst, w_k = _slots(kmap, wc, off)
    as_ints = lambda a: a.astype(I32).reshape(-1)
    per_block = lambda a: a.T.reshape(MOE_BLOCKS, 1, MOE_BLOCK * TOP_K)

    routed = _moe(as_ints(cnt), as_ints(off), per_block(dst), per_block(w_k), h2s,
                  exp_w_gate[l].astype(BF16), exp_w_up[l].astype(BF16), exp_w_down[l].astype(BF16))
    return _fin(x1s, routed, mods[5], row(final_norm_g))
```

```python
import jax
import jax.numpy as jnp
from jax import lax
from jax.experimental import pallas as pl
from jax.experimental.pallas import tpu as pltpu

D_MODEL = 1024
BATCH = 8
SEQ = 4096
N_TOK = BATCH * SEQ
S5_WIDTH = 512
S5_GROUP = 16
S5_GROUPS = 32
S5_STATE = 64
N_STATE = S5_GROUPS * S5_STATE
CONV_WIDTH = 512
CONV_KERNEL = 31
N_EXPERTS = 64
TOP_K = 8
EXPERT_HIDDEN = 256
SHARED_HIDDEN = 256
ROUTED_SCALE = 2.5
N_MOD = 6
EPS = 1e-6

LANES = 128
SUBLANES = 8
VMEM_LIMIT = 56 * 1024 * 1024

TILE_STEPS = 64
TILE_ROWS = TILE_STEPS * BATCH
CONV_HALO = (CONV_KERNEL - 1) * BATCH
CONV_CHUNK = 64
SCAN_LANES = 512
MOE_BLOCK = 2048
MOE_BLOCKS = N_TOK // MOE_BLOCK
MOE_CHUNK = 512
MOE_GRAIN = 64
MOE_EXPERTS_PER_STEP = 2
MOE_EXPERT_STEPS = N_EXPERTS // MOE_EXPERTS_PER_STEP
MOE_UNROLL = 4
SLAB = 4
STAGE_ROWS = TOP_K * MOE_BLOCK + MOE_CHUNK

F32 = jnp.float32
BF16 = jnp.bfloat16
U32 = jnp.uint32
I32 = jnp.int32


_X_SPEC = pl.BlockSpec((BATCH, TILE_STEPS, D_MODEL), lambda i: (0, i, 0))


def _time_major(a):
    return jnp.swapaxes(a, 0, 1).reshape(a.shape[0] * a.shape[1], a.shape[2])


def _batch_major(a):
    return jnp.swapaxes(a.reshape(a.shape[0] // BATCH, BATCH, a.shape[1]), 0, 1)


def _per_batch(rows, mod, op):
    r3 = rows.reshape(rows.shape[0] // BATCH, BATCH, rows.shape[1])
    return op(r3, mod).reshape(rows.shape)


def _pack_rows(x):
    half = D_MODEL // 2
    return pltpu.pack_elementwise([x[:, :half], x[:, half:]], packed_dtype=BF16)


def _unpack_words(w):
    lo = pltpu.unpack_elementwise(w, index=0, packed_dtype=BF16, unpacked_dtype=F32)
    hi = pltpu.unpack_elementwise(w, index=1, packed_dtype=BF16, unpacked_dtype=F32)
    return lo, hi


def _store_slabs(ref, base, words):
    rows = words.shape[0]
    for s in range(SLAB):
        ref[pl.ds(base + s, rows, stride=SLAB), :] = words[:, s * LANES:(s + 1) * LANES]


def _load_slabs(ref, base, rows):
    return [ref[pl.ds(base + s, rows, stride=SLAB), :] for s in range(SLAB)]


def _sigmoid(x):
    return jax.nn.sigmoid(x)


def _const_spec(shape):
    zeros = (0,) * len(shape)
    return pl.BlockSpec(shape, lambda *_: zeros, pipeline_mode=pl.Buffered(1))


def _ada_kernel(c_ref, w_ref, b_ref, o_ref):
    c = c_ref[...]
    c_act = c * _sigmoid(c)
    o_ref[...] = jnp.dot(c_act, w_ref[...], preferred_element_type=F32,
                         precision=lax.Precision.HIGHEST) + b_ref[...]


def _ada(c, w_ada, b_ada):
    n_out = N_MOD * D_MODEL
    blk = 1536
    return pl.pallas_call(
        _ada_kernel,
        out_shape=jax.ShapeDtypeStruct((BATCH, n_out), F32),
        grid=(n_out // blk,),
        in_specs=[pl.BlockSpec((BATCH, D_MODEL), lambda j: (0, 0)),
                  pl.BlockSpec((D_MODEL, blk), lambda j: (0, j)),
                  pl.BlockSpec((1, blk), lambda j: (0, j))],
        out_specs=pl.BlockSpec((BATCH, blk), lambda j: (0, j)),
        compiler_params=pltpu.CompilerParams(vmem_limit_bytes=VMEM_LIMIT),
        name="ada",
    )(c, w_ada, b_ada.reshape(1, n_out))


def _s5prep_kernel(lr_ref, li_ref, ldt_ref, br_ref, bi_ref, ar_ref, ai_ref, bbr_ref, bbi_ref):
    lr = lr_ref[...]
    li = li_ref[...]
    dt = jnp.exp(ldt_ref[...])
    mag = jnp.exp(lr * dt)
    abar_r = mag * jnp.cos(li * dt)
    abar_i = mag * jnp.sin(li * dt)
    den = lr * lr + li * li
    nr = abar_r - 1.0
    ni = abar_i
    k_r = (nr * lr + ni * li) / den
    k_i = (ni * lr - nr * li) / den
    br = br_ref[...]
    bi = bi_ref[...]
    ar_ref[...] = abar_r
    ai_ref[...] = abar_i
    bbr_ref[...] = k_r * br - k_i * bi
    bbi_ref[...] = k_r * bi + k_i * br


def _s5prep(lam_re, lam_im, log_dt, b_re, b_im):
    rows = S5_GROUPS * S5_GROUP
    rep = lambda a: jnp.broadcast_to(a[:, None, :], (S5_GROUPS, S5_GROUP, S5_STATE)).reshape(rows, S5_STATE)
    lr = rep(lam_re)
    li = rep(lam_im)
    ldt = rep(jnp.broadcast_to(log_dt[:, None], (S5_GROUPS, S5_STATE)))
    br = jnp.transpose(b_re, (0, 2, 1)).reshape(rows, S5_STATE)
    bi = jnp.transpose(b_im, (0, 2, 1)).reshape(rows, S5_STATE)
    shp = jax.ShapeDtypeStruct((rows, S5_STATE), F32)
    return pl.pallas_call(_s5prep_kernel, out_shape=(shp, shp, shp, shp), name="s5prep")(lr, li, ldt, br, bi)


def _block_diag(blocks):
    g, r, c = blocks.shape
    eye = jnp.eye(g, dtype=blocks.dtype)
    return jnp.einsum("grc,gk->grkc", blocks, eye).reshape(g * r, g * c)


def _gelu_tanh(x):
    sqrt_2_over_pi = 0.7978845608028654
    cdf = 0.5 * (1.0 + jnp.tanh(sqrt_2_over_pi * (x + 0.044715 * (x * x * x))))
    return x * cdf


def _rms_mod(x, g, scale, shift):
    ms = jnp.mean(x * x, axis=-1, keepdims=True)
    h = x * lax.rsqrt(ms + EPS) * g
    return _per_batch(h, (scale, shift), lambda r, m: r * (1.0 + m[0]) + m[1])


def _tok_kernel(x_ref, sh1_ref, sc1_ref, gt1_ref, sh2_ref, sc2_ref, gt2_ref, g1_ref, g2_ref,
                wu_ref, wcv_ref, wcg_ref, wa_ref, wb_ref,
                bsp_ref, csr_ref, csi_ref, ar_ref, ai_ref, d_ref, wv_ref, wg_ref,
                dww_ref, dwb_ref, lng_ref, lnb_ref, pw_ref, wout_ref,
                rw_ref, rb_ref, sgu_ref, sd_ref,
                x1s_ref, h2_ref, kmap_ref, wc_ref, cnt_ref,
                bu_ref, sb_ref, sr_ref, si_ref, cbuf_ref, cv_ref):
    step = pl.program_id(0)

    @pl.when(step == 0)
    def _():
        sr_ref[...] = jnp.zeros_like(sr_ref)
        si_ref[...] = jnp.zeros_like(si_ref)
        cbuf_ref[0:CONV_HALO, :] = jnp.zeros((CONV_HALO, CONV_WIDTH), F32)

    x = _time_major(x_ref[...])
    h = _rms_mod(x, g1_ref[...], sc1_ref[...], sh1_ref[...]).astype(BF16)
    u = jnp.dot(h, wu_ref[...], preferred_element_type=F32)
    u_b = u.astype(BF16)
    zc = (jnp.dot(h, wcv_ref[...], preferred_element_type=F32)
          * _sigmoid(jnp.dot(h, wcg_ref[...], preferred_element_type=F32)))

    for j in range(N_STATE // LANES):
        kb = (j * LANES // S5_STATE * S5_GROUP) // LANES
        res = jnp.dot(u_b[:, kb * LANES:(kb + 1) * LANES], bsp_ref[j], preferred_element_type=F32)
        bu_ref[:, j * LANES:(j + 1) * LANES] = res[:, :LANES]
        bu_ref[:, N_STATE + j * LANES:N_STATE + (j + 1) * LANES] = res[:, LANES:]

    gate_a = _sigmoid(jnp.dot(h, wa_ref[...], preferred_element_type=F32))
    gate_b = _sigmoid(jnp.dot(h, wb_ref[...], preferred_element_type=F32))

    for lg in range(N_STATE // SCAN_LANES):
        lo = lg * SCAN_LANES
        re = slice(lo, lo + SCAN_LANES)
        im = slice(N_STATE + lo, N_STATE + lo + SCAN_LANES)
        a_r = jnp.broadcast_to(ar_ref[:, re], (SUBLANES, SCAN_LANES))
        a_i = jnp.broadcast_to(ai_ref[:, re], (SUBLANES, SCAN_LANES))
        s_r = sr_ref[:, re]
        s_i = si_ref[:, re]
        for t in range(0, TILE_STEPS, 2):
            pair_r, pair_i = [], []
            for tt in (t, t + 1):
                rows = slice(tt * SUBLANES, (tt + 1) * SUBLANES)
                s_r, s_i = (a_r * s_r - a_i * s_i + bu_ref[rows, re],
                            a_r * s_i + a_i * s_r + bu_ref[rows, im])
                pair_r.append(s_r)
                pair_i.append(s_i)
            rows2 = slice(t * SUBLANES, (t + 2) * SUBLANES)
            sb_ref[rows2, re] = jnp.concatenate(pair_r, axis=0).astype(BF16)
            sb_ref[rows2, im] = jnp.concatenate(pair_i, axis=0).astype(BF16)
        sr_ref[:, re] = s_r
        si_ref[:, re] = s_i

    half_s = N_STATE // 2
    ys = []
    for hf in range(2):
        s_re = sb_ref[:, hf * half_s:(hf + 1) * half_s]
        s_im = sb_ref[:, N_STATE + hf * half_s:N_STATE + (hf + 1) * half_s]
        ys.append(jnp.dot(s_re, csr_ref[hf], preferred_element_type=F32)
                  - jnp.dot(s_im, csi_ref[hf], preferred_element_type=F32))
    y = jnp.concatenate(ys, axis=-1) + d_ref[...] * u
    z = _gelu_tanh(y).astype(BF16)
    y_a = (jnp.dot(z, wv_ref[...], preferred_element_type=F32)
           * _sigmoid(jnp.dot(z, wg_ref[...], preferred_element_type=F32)))

    cbuf_ref[CONV_HALO:CONV_HALO + TILE_ROWS, :] = zc
    for ci in range(TILE_ROWS // CONV_CHUNK):
        r0 = ci * CONV_CHUNK
        for lt in range(CONV_WIDTH // LANES):
            ls = slice(lt * LANES, (lt + 1) * LANES)
            win = cbuf_ref[r0:r0 + CONV_CHUNK + CONV_HALO, ls]
            acc = jnp.broadcast_to(dwb_ref[:, ls], (CONV_CHUNK, LANES))
            for k in range(CONV_KERNEL):
                acc = acc + dww_ref[k:k + 1, ls] * win[k * BATCH:k * BATCH + CONV_CHUNK]
            cv_ref[r0:r0 + CONV_CHUNK, ls] = acc
    cbuf_ref[0:CONV_HALO, :] = cbuf_ref[TILE_ROWS:TILE_ROWS + CONV_HALO, :]

    cv = cv_ref[...]
    mu = jnp.mean(cv, axis=-1, keepdims=True)
    var = jnp.mean(jnp.square(cv - mu), axis=-1, keepdims=True)
    ln = (cv - mu) * lax.rsqrt(var + EPS) * lng_ref[...] + lnb_ref[...]
    zb = (ln * _sigmoid(ln)).astype(BF16)
    y_b = jnp.dot(zb, pw_ref[...], preferred_element_type=F32)

    m = gate_a * y_a + gate_b * y_b
    mixed = jnp.dot(m.astype(BF16), wout_ref[...], preferred_element_type=F32)

    x1 = x + _per_batch(mixed, gt1_ref[...], lambda r, m: r * m)
    hb = _rms_mod(x1, g2_ref[...], sc2_ref[...], sh2_ref[...]).astype(BF16)
    _store_slabs(h2_ref, 0, _pack_rows(hb.astype(F32)))

    scores = _sigmoid(lax.dot_general(rw_ref[...], hb, (((1,), (1,)), ((), ())),
                                      preferred_element_type=F32))
    expert = lax.broadcasted_iota(I32, scores.shape, 0)
    neg_inf = jnp.float32(-jnp.inf)
    work = scores + rb_ref[...]
    kmap = jnp.zeros(scores.shape, I32)
    for k in range(TOP_K):
        top = jnp.max(work, axis=0, keepdims=True)
        idx = jnp.min(jnp.where(work == top, expert, N_EXPERTS), axis=0, keepdims=True)
        hit = expert == idx
        kmap = jnp.where(hit, k + 1, kmap)
        work = jnp.where(hit, neg_inf, work)
    sel = kmap > 0
    w_sel = jnp.where(sel, scores, 0.0)
    kmap_ref[...] = kmap
    wc_ref[...] = w_sel / jnp.sum(w_sel, axis=0, keepdims=True) * ROUTED_SCALE

    @pl.when(step % (MOE_BLOCK // TILE_ROWS) == 0)
    def _():
        cnt_ref[...] = jnp.zeros_like(cnt_ref)

    cnt_ref[...] += jnp.sum(sel.astype(F32), axis=1, keepdims=True)

    gu = jnp.dot(hb, sgu_ref[...], preferred_element_type=F32)
    g = gu[:, :SHARED_HIDDEN]
    act = (g * _sigmoid(g)) * gu[:, SHARED_HIDDEN:]
    shared = jnp.dot(act.astype(BF16), sd_ref[...], preferred_element_type=F32)
    x1s_ref[...] = x1 + _per_batch(shared, gt2_ref[...], lambda r, m: r * m)


def _block_diag_tiles(bmat, c_r, c_i):
    tiles = []
    for j in range(N_STATE // LANES):
        kb = (j * LANES // S5_STATE * S5_GROUP) // LANES
        k_rows = slice(kb * LANES, (kb + 1) * LANES)
        tiles.append(jnp.concatenate([bmat[k_rows, j * LANES:(j + 1) * LANES],
                                      bmat[k_rows, N_STATE + j * LANES:N_STATE + (j + 1) * LANES]], axis=1))
    half_c, half_s = S5_WIDTH // 2, N_STATE // 2
    halves = lambda m: jnp.stack([m[hf * half_s:(hf + 1) * half_s, hf * half_c:(hf + 1) * half_c]
                                  for hf in range(2)])
    return jnp.stack(tiles), halves(c_r), halves(c_i)


def _tok(x, mods, consts):
    ts = TILE_ROWS
    tiles_per_block = MOE_BLOCK // ts
    row_spec = lambda rows, c: pl.BlockSpec((rows, c), lambda i: (i, 0))
    route_spec = pl.BlockSpec((N_EXPERTS, ts), lambda i: (0, i))
    return pl.pallas_call(
        _tok_kernel,
        out_shape=(jax.ShapeDtypeStruct((N_TOK, D_MODEL), F32),
                   jax.ShapeDtypeStruct((N_TOK * SLAB, LANES), U32),
                   jax.ShapeDtypeStruct((N_EXPERTS, N_TOK), I32),
                   jax.ShapeDtypeStruct((N_EXPERTS, N_TOK), F32),
                   jax.ShapeDtypeStruct((MOE_BLOCKS, N_EXPERTS, 1), F32)),
        grid=(SEQ // TILE_STEPS,),
        in_specs=[_X_SPEC] + [_const_spec(a.shape) for a in tuple(mods) + tuple(consts)],
        out_specs=(row_spec(ts, D_MODEL), row_spec(ts * SLAB, LANES), route_spec, route_spec,
                   pl.BlockSpec((None, N_EXPERTS, 1), lambda i: (i // tiles_per_block, 0, 0))),
        scratch_shapes=[pltpu.VMEM((ts, 2 * N_STATE), F32),
                        pltpu.VMEM((ts, 2 * N_STATE), BF16),
                        pltpu.VMEM((SUBLANES, N_STATE), F32),
                        pltpu.VMEM((SUBLANES, N_STATE), F32),
                        pltpu.VMEM((CONV_HALO + ts, CONV_WIDTH), F32),
                        pltpu.VMEM((ts, CONV_WIDTH), F32)],
        compiler_params=pltpu.CompilerParams(
            dimension_semantics=("arbitrary",), vmem_limit_bytes=VMEM_LIMIT),
        name="tok",
    )(x, *mods, *consts)


def _slots_kernel(kmap_ref, wc_ref, off_ref, tri_ref, dst_ref, wk_ref, seen_ref):
    @pl.when(pl.program_id(0) % (MOE_BLOCK // TILE_ROWS) == 0)
    def _():
        seen_ref[...] = jnp.zeros_like(seen_ref)

    kmap = kmap_ref[...]
    wc = wc_ref[...]
    sel = (kmap > 0).astype(F32)
    seen = seen_ref[...]
    slot = jnp.dot(sel.astype(BF16), tri_ref[...], preferred_element_type=F32) + (seen + off_ref[...])
    seen_ref[...] = seen + jnp.sum(sel, axis=1, keepdims=True)
    dst, w_k = [], []
    for k in range(TOP_K):
        hit = kmap == k + 1
        dst.append(jnp.sum(jnp.where(hit, slot, 0.0), axis=0, keepdims=True))
        w_k.append(jnp.sum(jnp.where(hit, wc, 0.0), axis=0, keepdims=True))
    dst_ref[...] = jnp.concatenate(dst, axis=0).astype(I32) * SLAB
    wk_ref[...] = jnp.concatenate(w_k, axis=0)


def _slots(kmap, wc, off):
    ts = TILE_ROWS
    tiles_per_block = MOE_BLOCK // ts
    route_spec = pl.BlockSpec((N_EXPERTS, ts), lambda i: (0, i))
    slot_spec = pl.BlockSpec((TOP_K, ts), lambda i: (0, i))
    tri = (lax.broadcasted_iota(I32, (ts, ts), 0) < lax.broadcasted_iota(I32, (ts, ts), 1)).astype(BF16)
    return pl.pallas_call(
        _slots_kernel,
        out_shape=(jax.ShapeDtypeStruct((TOP_K, N_TOK), I32), jax.ShapeDtypeStruct((TOP_K, N_TOK), F32)),
        grid=(N_TOK // ts,),
        in_specs=[route_spec, route_spec,
                  pl.BlockSpec((None, N_EXPERTS, 1), lambda i: (i // tiles_per_block, 0, 0)),
                  _const_spec(tri.shape)],
        out_specs=(slot_spec, slot_spec),
        scratch_shapes=[pltpu.VMEM((N_EXPERTS, 1), F32)],
        compiler_params=pltpu.CompilerParams(
            dimension_semantics=("arbitrary",), vmem_limit_bytes=VMEM_LIMIT),
        name="slots",
    )(kmap, wc, off, tri)


def _moe_kernel(cnt_ref, off_ref, dst_ref, w_ref, h_ref, wg_hbm, wu_hbm, wd_hbm, o_ref,
                stage_ref, wg_buf, wu_buf, wd_buf, sem):
    blk = pl.program_id(0)

    def weight_copies(step, slot):
        experts = pl.ds(step * MOE_EXPERTS_PER_STEP, MOE_EXPERTS_PER_STEP)
        pairs = ((wg_hbm, wg_buf), (wu_hbm, wu_buf), (wd_hbm, wd_buf))
        return [pltpu.make_async_copy(hbm.at[experts], buf.at[slot], sem.at[i, slot])
                for i, (hbm, buf) in enumerate(pairs)]

    def fetch(step, slot):
        for copy in weight_copies(step, slot):
            copy.start()

    def arrived(step, slot):
        for copy in weight_copies(step, slot):
            copy.wait()

    fetch(0, 0)

    @pl.when(blk == 0)
    def _():
        stage_ref[TOP_K * MOE_BLOCK * SLAB:, :] = jnp.zeros((MOE_CHUNK * SLAB, LANES), U32)

    def slab_at(ref, sublane_row):
        return ref.at[pl.ds(pl.multiple_of(sublane_row, SLAB), SLAB), :]

    group = MOE_UNROLL * TOP_K

    def selections(ref, i):
        return ref.at[0, pl.ds(pl.multiple_of(i * group, group), group)]

    def dispatch(i, carry):
        dst = selections(dst_ref, i)
        for j in range(MOE_UNROLL):
            slab = slab_at(h_ref, (i * MOE_UNROLL + j) * SLAB)[...]
            for k in range(TOP_K):
                slab_at(stage_ref, dst[j * TOP_K + k])[...] = slab
        return carry

    lax.fori_loop(0, MOE_BLOCK // MOE_UNROLL, dispatch, 0)

    def run_chunk(slot, j, first, live_rows, rows):
        base = first * SLAB
        words = _load_slabs(stage_ref, base, rows)
        halves = [_unpack_words(w) for w in words]
        xs = jnp.concatenate([lo for lo, _ in halves] + [hi for _, hi in halves], axis=-1)
        xs = xs.astype(BF16)
        g = jnp.dot(xs, wg_buf[slot, j], preferred_element_type=F32)
        act = (g * _sigmoid(g)) * jnp.dot(xs, wu_buf[slot, j], preferred_element_type=F32)
        out = _pack_rows(jnp.dot(act.astype(BF16), wd_buf[slot, j], preferred_element_type=F32))
        live = lax.broadcasted_iota(I32, (rows, LANES), 0) < live_rows
        for s in range(SLAB):
            merged = jnp.where(live, out[:, s * LANES:(s + 1) * LANES], words[s])
            stage_ref[pl.ds(base + s, rows, stride=SLAB), :] = merged

    def run_expert(slot, j, first, n_rows):
        n_whole = jnp.maximum(n_rows - 1, 0) // MOE_CHUNK

        def whole(c, carry):
            run_chunk(slot, j, first + c * MOE_CHUNK, MOE_CHUNK, MOE_CHUNK)
            return carry

        lax.fori_loop(0, n_whole, whole, 0)
        rest = n_rows - n_whole * MOE_CHUNK
        size_class = (rest + MOE_GRAIN - 1) // MOE_GRAIN
        for cls in range(1, MOE_CHUNK // MOE_GRAIN + 1):
            @pl.when(size_class == cls)
            def _(cls=cls):
                run_chunk(slot, j, first + n_whole * MOE_CHUNK, rest, cls * MOE_GRAIN)

    def expert_step(step, carry):
        slot = step % 2

        @pl.when(step + 1 < MOE_EXPERT_STEPS)
        def _():
            fetch(step + 1, 1 - slot)

        arrived(step, slot)
        for j in range(MOE_EXPERTS_PER_STEP):
            e = blk * N_EXPERTS + step * MOE_EXPERTS_PER_STEP + j
            run_expert(slot, j, off_ref[e], cnt_ref[e])
        return carry

    lax.fori_loop(0, MOE_EXPERT_STEPS, expert_step, 0)

    def combine(i, carry):
        dst = selections(dst_ref, i)
        wts = selections(w_ref, i)
        for j in range(MOE_UNROLL):
            t = i * MOE_UNROLL + j
            terms = []
            for k in range(TOP_K):
                w = wts[j * TOP_K + k]
                lo, hi = _unpack_words(slab_at(stage_ref, dst[j * TOP_K + k])[...])
                terms.append((lo * w, hi * w))
            while len(terms) > 1:
                terms = [(a[0] + b[0], a[1] + b[1]) for a, b in zip(terms[0::2], terms[1::2])]
            slab_at(o_ref, t * SLAB)[...] = pltpu.pack_elementwise(list(terms[0]), packed_dtype=BF16)
        return carry

    lax.fori_loop(0, MOE_BLOCK // MOE_UNROLL, combine, 0)


def _moe(cnt, off, dst, w, h2s, wg, wu, wd):
    smem_spec = pl.BlockSpec((None, 1, TOP_K * MOE_BLOCK), lambda b, *_: (b, 0, 0), memory_space=pltpu.SMEM)
    hbm_spec = pl.BlockSpec(memory_space=pl.ANY)
    weight_buf = lambda rows, cols: pltpu.VMEM((2, MOE_EXPERTS_PER_STEP, rows, cols), BF16)
    return pl.pallas_call(
        _moe_kernel,
        out_shape=jax.ShapeDtypeStruct((N_TOK * SLAB, LANES), U32),
        grid_spec=pltpu.PrefetchScalarGridSpec(
            num_scalar_prefetch=2,
            grid=(MOE_BLOCKS,),
            in_specs=[smem_spec, smem_spec,
                      pl.BlockSpec((MOE_BLOCK * SLAB, LANES), lambda b, *_: (b, 0),
                                   pipeline_mode=pl.Buffered(1)),
                      hbm_spec, hbm_spec, hbm_spec],
            out_specs=pl.BlockSpec((MOE_BLOCK * SLAB, LANES), lambda b, *_: (b, 0)),
            scratch_shapes=[pltpu.VMEM((STAGE_ROWS * SLAB, LANES), U32),
                            weight_buf(D_MODEL, EXPERT_HIDDEN), weight_buf(D_MODEL, EXPERT_HIDDEN),
                            weight_buf(EXPERT_HIDDEN, D_MODEL),
                            pltpu.SemaphoreType.DMA((3, 2))]),
        compiler_params=pltpu.CompilerParams(
            dimension_semantics=("arbitrary",), vmem_limit_bytes=VMEM_LIMIT),
        name="moe",
    )(cnt, off, dst, w, h2s, wg, wu, wd)


def _fin_kernel(x1s_ref, r_ref, gt2_ref, fg_ref, o_ref):
    halves = [_unpack_words(w) for w in _load_slabs(r_ref, 0, TILE_ROWS)]
    routed = jnp.concatenate([lo for lo, _ in halves] + [hi for _, hi in halves], axis=-1)
    x2 = x1s_ref[...] + _per_batch(routed, gt2_ref[...], lambda r, m: r * m)
    ms = jnp.mean(x2 * x2, axis=-1, keepdims=True)
    o_ref[...] = _batch_major(x2 * lax.rsqrt(ms + EPS) * fg_ref[...])


def _fin(x1s, routed, gt2, fg):
    ts = TILE_ROWS
    return pl.pallas_call(
        _fin_kernel,
        out_shape=jax.ShapeDtypeStruct((BATCH, SEQ, D_MODEL), F32),
        grid=(SEQ // TILE_STEPS,),
        in_specs=[pl.BlockSpec((ts, D_MODEL), lambda i: (i, 0)),
                  pl.BlockSpec((ts * SLAB, LANES), lambda i: (i, 0)),
                  _const_spec((1, BATCH, D_MODEL)),
                  _const_spec((1, D_MODEL))],
        out_specs=_X_SPEC,
        compiler_params=pltpu.CompilerParams(
            dimension_semantics=("parallel",), vmem_limit_bytes=VMEM_LIMIT),
        name="fin",
    )(x1s, routed, gt2, fg)


def kernel(x, c, w_ada, b_ada, norm1_g, w_in, s5_lambda_re, s5_lambda_im, s5_log_dt, s5_b_re, s5_b_im, s5_c_re, s5_c_im, s5_d, s5_glu_wv, s5_glu_wg, conv_dw_w, conv_dw_b, conv_ln_g, conv_ln_b, conv_pw_w, w_out, norm2_g, router_w, router_bias, exp_w_gate, exp_w_up, exp_w_down, shared_w_gate, shared_w_up, shared_w_down, final_norm_g):
    l = 0
    row = lambda a: a.reshape(1, -1)

    mod = _ada(c, w_ada[l], b_ada[l])
    mods = [m.reshape(1, BATCH, D_MODEL) for m in jnp.split(mod, N_MOD, axis=-1)]

    abar_r, abar_i, bbar_r, bbar_i = _s5prep(s5_lambda_re[l], s5_lambda_im[l], s5_log_dt[l],
                                             s5_b_re[l], s5_b_im[l])
    per_group = lambda a: a.reshape(S5_GROUPS, S5_GROUP, S5_STATE)
    a_r = per_group(abar_r)[:, 0, :].reshape(1, N_STATE)
    a_i = per_group(abar_i)[:, 0, :].reshape(1, N_STATE)
    bmat = jnp.concatenate([_block_diag(per_group(bbar_r)), _block_diag(per_group(bbar_i))],
                           axis=1).astype(BF16)
    c_r = _block_diag(jnp.transpose(s5_c_re[l], (0, 2, 1))).astype(BF16)
    c_i = _block_diag(jnp.transpose(s5_c_im[l], (0, 2, 1))).astype(BF16)

    o1 = S5_WIDTH
    o2 = o1 + CONV_WIDTH
    o3 = o2 + CONV_WIDTH
    o4 = o3 + D_MODEL
    w = w_in[l].astype(BF16)
    sgu = jnp.concatenate([shared_w_gate[l], shared_w_up[l]], axis=1).astype(BF16)
    consts = (row(norm1_g[l]), row(norm2_g[l]),
              w[:, :o1], w[:, o1:o2], w[:, o2:o3], w[:, o3:o4], w[:, o4:],
              *_block_diag_tiles(bmat, c_r, c_i), a_r, a_i, row(s5_d[l]),
              s5_glu_wv[l].astype(BF16), s5_glu_wg[l].astype(BF16),
              conv_dw_w[l], row(conv_dw_b[l]), row(conv_ln_g[l]), row(conv_ln_b[l]),
              conv_pw_w[l].astype(BF16), w_out[l].astype(BF16),
              router_w[l].T.astype(BF16), router_bias[l].reshape(N_EXPERTS, 1),
              sgu, shared_w_down[l].astype(BF16))
    x1s, h2s, kmap, wc, cnt = _tok(x, mods, consts)

    off = jnp.cumsum(cnt, axis=1) - cnt
    dst, w_k = _slots(kmap, wc, off)
    as_ints = lambda a: a.astype(I32).reshape(-1)
    per_block = lambda a: a.T.reshape(MOE_BLOCKS, 1, MOE_BLOCK * TOP_K)

    routed = _moe(as_ints(cnt), as_ints(off), per_block(dst), per_block(w_k), h2s,
                  exp_w_gate[l].astype(BF16), exp_w_up[l].astype(BF16), exp_w_down[l].astype(BF16))
    return _fin(x1s, routed, mods[5], row(final_norm_g))
```

```python
import jax
import jax.numpy as jnp
from jax import lax
from jax.experimental import pallas as pl
from jax.experimental.pallas import tpu as pltpu

D_MODEL = 1024
BATCH = 8
SEQ = 4096
N_TOK = BATCH * SEQ
S5_WIDTH = 512
S5_GROUP = 16
S5_GROUPS = 32
S5_STATE = 64
N_STATE = S5_GROUPS * S5_STATE
CONV_WIDTH = 512
CONV_KERNEL = 31
N_EXPERTS = 64
TOP_K = 8
EXPERT_HIDDEN = 256
SHARED_HIDDEN = 256
ROUTED_SCALE = 2.5
N_MOD = 6
EPS = 1e-6

LANES = 128
SUBLANES = 8
VMEM_LIMIT = 56 * 1024 * 1024

TILE_STEPS = 64
TILE_ROWS = TILE_STEPS * BATCH
FIN_STEPS = 128
CONV_HALO = (CONV_KERNEL - 1) * BATCH
CONV_CHUNK = 64
SCAN_LANES = 512
MOE_BLOCK = 2048
MOE_BLOCKS = N_TOK // MOE_BLOCK
MOE_CHUNK = 512
MOE_GRAIN = 64
MOE_EXPERTS_PER_STEP = 2
MOE_EXPERT_STEPS = N_EXPERTS // MOE_EXPERTS_PER_STEP
MOE_UNROLL = 4
SLAB = 4
STAGE_ROWS = TOP_K * MOE_BLOCK + MOE_CHUNK

F32 = jnp.float32
BF16 = jnp.bfloat16
U32 = jnp.uint32
I32 = jnp.int32


_X_SPEC = pl.BlockSpec((BATCH, TILE_STEPS, D_MODEL), lambda i: (0, i, 0))


def _time_major(a):
    return jnp.swapaxes(a, 0, 1).reshape(a.shape[0] * a.shape[1], a.shape[2])


def _batch_major(a):
    return jnp.swapaxes(a.reshape(a.shape[0] // BATCH, BATCH, a.shape[1]), 0, 1)


def _per_batch(rows, mod, op):
    r3 = rows.reshape(rows.shape[0] // BATCH, BATCH, rows.shape[1])
    return op(r3, mod).reshape(rows.shape)


def _pack_rows(x):
    half = D_MODEL // 2
    return pltpu.pack_elementwise([x[:, :half], x[:, half:]], packed_dtype=BF16)


def _unpack_words(w):
    lo = pltpu.unpack_elementwise(w, index=0, packed_dtype=BF16, unpacked_dtype=F32)
    hi = pltpu.unpack_elementwise(w, index=1, packed_dtype=BF16, unpacked_dtype=F32)
    return lo, hi


def _store_slabs(ref, base, words):
    rows = words.shape[0]
    for s in range(SLAB):
        ref[pl.ds(base + s, rows, stride=SLAB), :] = words[:, s * LANES:(s + 1) * LANES]


def _load_slabs(ref, base, rows):
    return [ref[pl.ds(base + s, rows, stride=SLAB), :] for s in range(SLAB)]


def _sigmoid(x):
    return jax.nn.sigmoid(x)


def _const_spec(shape):
    zeros = (0,) * len(shape)
    return pl.BlockSpec(shape, lambda *_: zeros, pipeline_mode=pl.Buffered(1))


def _ada_kernel(c_ref, w_ref, b_ref, o_ref):
    c = c_ref[...]
    c_act = c * _sigmoid(c)
    o_ref[...] = jnp.dot(c_act, w_ref[...], preferred_element_type=F32,
                         precision=lax.Precision.HIGHEST) + b_ref[...]


def _ada(c, w_ada, b_ada):
    n_out = N_MOD * D_MODEL
    blk = 1536
    return pl.pallas_call(
        _ada_kernel,
        out_shape=jax.ShapeDtypeStruct((BATCH, n_out), F32),
        grid=(n_out // blk,),
        in_specs=[pl.BlockSpec((BATCH, D_MODEL), lambda j: (0, 0)),
                  pl.BlockSpec((D_MODEL, blk), lambda j: (0, j)),
                  pl.BlockSpec((1, blk), lambda j: (0, j))],
        out_specs=pl.BlockSpec((BATCH, blk), lambda j: (0, j)),
        compiler_params=pltpu.CompilerParams(vmem_limit_bytes=VMEM_LIMIT),
        name="ada",
    )(c, w_ada, b_ada.reshape(1, n_out))


def _s5prep_kernel(lr_ref, li_ref, ldt_ref, br_ref, bi_ref, ar_ref, ai_ref, bbr_ref, bbi_ref):
    lr = lr_ref[...]
    li = li_ref[...]
    dt = jnp.exp(ldt_ref[...])
    mag = jnp.exp(lr * dt)
    abar_r = mag * jnp.cos(li * dt)
    abar_i = mag * jnp.sin(li * dt)
    den = lr * lr + li * li
    nr = abar_r - 1.0
    ni = abar_i
    k_r = (nr * lr + ni * li) / den
    k_i = (ni * lr - nr * li) / den
    br = br_ref[...]
    bi = bi_ref[...]
    ar_ref[...] = abar_r
    ai_ref[...] = abar_i
    bbr_ref[...] = k_r * br - k_i * bi
    bbi_ref[...] = k_r * bi + k_i * br


def _s5prep(lam_re, lam_im, log_dt, b_re, b_im):
    rows = S5_GROUPS * S5_GROUP
    rep = lambda a: jnp.broadcast_to(a[:, None, :], (S5_GROUPS, S5_GROUP, S5_STATE)).reshape(rows, S5_STATE)
    lr = rep(lam_re)
    li = rep(lam_im)
    ldt = rep(jnp.broadcast_to(log_dt[:, None], (S5_GROUPS, S5_STATE)))
    br = jnp.transpose(b_re, (0, 2, 1)).reshape(rows, S5_STATE)
    bi = jnp.transpose(b_im, (0, 2, 1)).reshape(rows, S5_STATE)
    shp = jax.ShapeDtypeStruct((rows, S5_STATE), F32)
    return pl.pallas_call(_s5prep_kernel, out_shape=(shp, shp, shp, shp), name="s5prep")(lr, li, ldt, br, bi)


def _block_diag(blocks):
    g, r, c = blocks.shape
    eye = jnp.eye(g, dtype=blocks.dtype)
    return jnp.einsum("grc,gk->grkc", blocks, eye).reshape(g * r, g * c)


def _gelu_tanh(x):
    sqrt_2_over_pi = 0.7978845608028654
    cdf = 0.5 * (1.0 + jnp.tanh(sqrt_2_over_pi * (x + 0.044715 * (x * x * x))))
    return x * cdf


def _rms_mod(x, g, scale, shift):
    ms = jnp.mean(x * x, axis=-1, keepdims=True)
    h = x * lax.rsqrt(ms + EPS) * g
    return _per_batch(h, (scale, shift), lambda r, m: r * (1.0 + m[0]) + m[1])


def _tok_kernel(x_ref, sh1_ref, sc1_ref, gt1_ref, sh2_ref, sc2_ref, gt2_ref, g1_ref, g2_ref,
                wu_ref, wcv_ref, wcg_ref, wa_ref, wb_ref,
                bsp_ref, csr_ref, csi_ref, ar_ref, ai_ref, d_ref, wv_ref, wg_ref,
                dww_ref, dwb_ref, lng_ref, lnb_ref, pw_ref, wout_ref,
                rw_ref, rb_ref, sgu_ref, sd_ref, ew_gate_ref, ew_up_ref, ew_down_ref,
                x1s_ref, h2_ref, kmap_ref, wc_ref, cnt_ref, eb_gate_ref, eb_up_ref, eb_down_ref,
                bu_ref, sb_ref, sr_ref, si_ref, cbuf_ref, cv_ref):
    step = pl.program_id(0)

    @pl.when(step == 0)
    def _():
        sr_ref[...] = jnp.zeros_like(sr_ref)
        si_ref[...] = jnp.zeros_like(si_ref)
        cbuf_ref[0:CONV_HALO, :] = jnp.zeros((CONV_HALO, CONV_WIDTH), F32)

    x = _time_major(x_ref[...])
    h = _rms_mod(x, g1_ref[...], sc1_ref[...], sh1_ref[...]).astype(BF16)
    u = jnp.dot(h, wu_ref[...], preferred_element_type=F32)
    u_b = u.astype(BF16)
    zc = (jnp.dot(h, wcv_ref[...], preferred_element_type=F32)
          * _sigmoid(jnp.dot(h, wcg_ref[...], preferred_element_type=F32)))

    for j in range(N_STATE // LANES):
        kb = (j * LANES // S5_STATE * S5_GROUP) // LANES
        res = jnp.dot(u_b[:, kb * LANES:(kb + 1) * LANES], bsp_ref[j], preferred_element_type=F32)
        bu_ref[:, j * LANES:(j + 1) * LANES] = res[:, :LANES]
        bu_ref[:, N_STATE + j * LANES:N_STATE + (j + 1) * LANES] = res[:, LANES:]

    gate_a = _sigmoid(jnp.dot(h, wa_ref[...], preferred_element_type=F32))
    gate_b = _sigmoid(jnp.dot(h, wb_ref[...], preferred_element_type=F32))

    for lg in range(N_STATE // SCAN_LANES):
        lo = lg * SCAN_LANES
        re = slice(lo, lo + SCAN_LANES)
        im = slice(N_STATE + lo, N_STATE + lo + SCAN_LANES)
        a_r = jnp.broadcast_to(ar_ref[:, re], (SUBLANES, SCAN_LANES))
        a_i = jnp.broadcast_to(ai_ref[:, re], (SUBLANES, SCAN_LANES))
        s_r = sr_ref[:, re]
        s_i = si_ref[:, re]
        for t in range(0, TILE_STEPS, 2):
            pair_r, pair_i = [], []
            for tt in (t, t + 1):
                rows = slice(tt * SUBLANES, (tt + 1) * SUBLANES)
                s_r, s_i = (a_r * s_r - a_i * s_i + bu_ref[rows, re],
                            a_r * s_i + a_i * s_r + bu_ref[rows, im])
                pair_r.append(s_r)
                pair_i.append(s_i)
            rows2 = slice(t * SUBLANES, (t + 2) * SUBLANES)
            sb_ref[rows2, re] = jnp.concatenate(pair_r, axis=0).astype(BF16)
            sb_ref[rows2, im] = jnp.concatenate(pair_i, axis=0).astype(BF16)
        sr_ref[:, re] = s_r
        si_ref[:, re] = s_i

    half_s = N_STATE // 2
    ys = []
    for hf in range(2):
        s_re = sb_ref[:, hf * half_s:(hf + 1) * half_s]
        s_im = sb_ref[:, N_STATE + hf * half_s:N_STATE + (hf + 1) * half_s]
        ys.append(jnp.dot(s_re, csr_ref[hf], preferred_element_type=F32)
                  - jnp.dot(s_im, csi_ref[hf], preferred_element_type=F32))
    y = jnp.concatenate(ys, axis=-1) + d_ref[...] * u
    z = _gelu_tanh(y).astype(BF16)
    y_a = (jnp.dot(z, wv_ref[...], preferred_element_type=F32)
           * _sigmoid(jnp.dot(z, wg_ref[...], preferred_element_type=F32)))

    cbuf_ref[CONV_HALO:CONV_HALO + TILE_ROWS, :] = zc
    for ci in range(TILE_ROWS // CONV_CHUNK):
        r0 = ci * CONV_CHUNK
        for lt in range(CONV_WIDTH // LANES):
            ls = slice(lt * LANES, (lt + 1) * LANES)
            win = cbuf_ref[r0:r0 + CONV_CHUNK + CONV_HALO, ls]
            acc = jnp.broadcast_to(dwb_ref[:, ls], (CONV_CHUNK, LANES))
            for k in range(CONV_KERNEL):
                acc = acc + dww_ref[k:k + 1, ls] * win[k * BATCH:k * BATCH + CONV_CHUNK]
            cv_ref[r0:r0 + CONV_CHUNK, ls] = acc
    cbuf_ref[0:CONV_HALO, :] = cbuf_ref[TILE_ROWS:TILE_ROWS + CONV_HALO, :]

    cv = cv_ref[...]
    mu = jnp.mean(cv, axis=-1, keepdims=True)
    var = jnp.mean(jnp.square(cv - mu), axis=-1, keepdims=True)
    ln = (cv - mu) * lax.rsqrt(var + EPS) * lng_ref[...] + lnb_ref[...]
    zb = (ln * _sigmoid(ln)).astype(BF16)
    y_b = jnp.dot(zb, pw_ref[...], preferred_element_type=F32)

    m = gate_a * y_a + gate_b * y_b
    mixed = jnp.dot(m.astype(BF16), wout_ref[...], preferred_element_type=F32)

    x1 = x + _per_batch(mixed, gt1_ref[...], lambda r, m: r * m)
    hb = _rms_mod(x1, g2_ref[...], sc2_ref[...], sh2_ref[...]).astype(BF16)
    _store_slabs(h2_ref, 0, _pack_rows(hb.astype(F32)))

    scores = _sigmoid(lax.dot_general(rw_ref[...], hb, (((1,), (1,)), ((), ())),
                                      preferred_element_type=F32))
    expert = lax.broadcasted_iota(I32, scores.shape, 0)
    neg_inf = jnp.float32(-jnp.inf)
    work = scores + rb_ref[...]
    kmap = jnp.zeros(scores.shape, I32)
    for k in range(TOP_K):
        top = jnp.max(work, axis=0, keepdims=True)
        idx = jnp.min(jnp.where(work == top, expert, N_EXPERTS), axis=0, keepdims=True)
        hit = expert == idx
        kmap = jnp.where(hit, k + 1, kmap)
        work = jnp.where(hit, neg_inf, work)
    sel = kmap > 0
    w_sel = jnp.where(sel, scores, 0.0)
    kmap_ref[...] = kmap
    wc_ref[...] = w_sel / jnp.sum(w_sel, axis=0, keepdims=True) * ROUTED_SCALE

    @pl.when(step % (MOE_BLOCK // TILE_ROWS) == 0)
    def _():
        cnt_ref[...] = jnp.zeros_like(cnt_ref)

    cnt_ref[...] += jnp.sum(sel.astype(F32), axis=1, keepdims=True)

    gu = jnp.dot(hb, sgu_ref[...], preferred_element_type=F32)
    g = gu[:, :SHARED_HIDDEN]
    act = (g * _sigmoid(g)) * gu[:, SHARED_HIDDEN:]
    shared = jnp.dot(act.astype(BF16), sd_ref[...], preferred_element_type=F32)
    x1s_ref[...] = x1 + _per_batch(shared, gt2_ref[...], lambda r, m: r * m)

    eb_gate_ref[...] = ew_gate_ref[...].astype(BF16)
    eb_up_ref[...] = ew_up_ref[...].astype(BF16)
    eb_down_ref[...] = ew_down_ref[...].astype(BF16)


def _block_diag_tiles(bmat, c_r, c_i):
    tiles = []
    for j in range(N_STATE // LANES):
        kb = (j * LANES // S5_STATE * S5_GROUP) // LANES
        k_rows = slice(kb * LANES, (kb + 1) * LANES)
        tiles.append(jnp.concatenate([bmat[k_rows, j * LANES:(j + 1) * LANES],
                                      bmat[k_rows, N_STATE + j * LANES:N_STATE + (j + 1) * LANES]], axis=1))
    half_c, half_s = S5_WIDTH // 2, N_STATE // 2
    halves = lambda m: jnp.stack([m[hf * half_s:(hf + 1) * half_s, hf * half_c:(hf + 1) * half_c]
                                  for hf in range(2)])
    return jnp.stack(tiles), halves(c_r), halves(c_i)


def _tok(x, mods, consts, expert_weights):
    ts = TILE_ROWS
    steps = SEQ // TILE_STEPS
    tiles_per_block = MOE_BLOCK // ts
    row_spec = lambda rows, c: pl.BlockSpec((rows, c), lambda i: (i, 0))
    route_spec = pl.BlockSpec((N_EXPERTS, ts), lambda i: (0, i))
    per_step = N_EXPERTS // steps
    assert per_step * steps == N_EXPERTS
    cast_specs = [pl.BlockSpec((per_step,) + a.shape[1:], lambda i: (i, 0, 0)) for a in expert_weights]
    return pl.pallas_call(
        _tok_kernel,
        out_shape=(jax.ShapeDtypeStruct((N_TOK, D_MODEL), F32),
                   jax.ShapeDtypeStruct((N_TOK * SLAB, LANES), U32),
                   jax.ShapeDtypeStruct((N_EXPERTS, N_TOK), I32),
                   jax.ShapeDtypeStruct((N_EXPERTS, N_TOK), F32),
                   jax.ShapeDtypeStruct((MOE_BLOCKS, N_EXPERTS, 1), F32))
                  + tuple(jax.ShapeDtypeStruct(a.shape, BF16) for a in expert_weights),
        grid=(steps,),
        in_specs=[_X_SPEC] + [_const_spec(a.shape) for a in tuple(mods) + tuple(consts)] + cast_specs,
        out_specs=(row_spec(ts, D_MODEL), row_spec(ts * SLAB, LANES), route_spec, route_spec,
                   pl.BlockSpec((None, N_EXPERTS, 1), lambda i: (i // tiles_per_block, 0, 0)))
                  + tuple(cast_specs),
        scratch_shapes=[pltpu.VMEM((ts, 2 * N_STATE), F32),
                        pltpu.VMEM((ts, 2 * N_STATE), BF16),
                        pltpu.VMEM((SUBLANES, N_STATE), F32),
                        pltpu.VMEM((SUBLANES, N_STATE), F32),
                        pltpu.VMEM((CONV_HALO + ts, CONV_WIDTH), F32),
                        pltpu.VMEM((ts, CONV_WIDTH), F32)],
        compiler_params=pltpu.CompilerParams(
            dimension_semantics=("arbitrary",), vmem_limit_bytes=VMEM_LIMIT),
        name="tok",
    )(x, *mods, *consts, *expert_weights)


def _slots_kernel(kmap_ref, wc_ref, off_ref, tri_ref, dst_ref, wk_ref, seen_ref):
    @pl.when(pl.program_id(0) % (MOE_BLOCK // TILE_ROWS) == 0)
    def _():
        seen_ref[...] = jnp.zeros_like(seen_ref)

    kmap = kmap_ref[...]
    wc = wc_ref[...]
    sel = (kmap > 0).astype(F32)
    seen = seen_ref[...]
    slot = jnp.dot(sel.astype(BF16), tri_ref[...], preferred_element_type=F32) + (seen + off_ref[...])
    seen_ref[...] = seen + jnp.sum(sel, axis=1, keepdims=True)
    dst, w_k = [], []
    for k in range(TOP_K):
        hit = kmap == k + 1
        dst.append(jnp.sum(jnp.where(hit, slot, 0.0), axis=0, keepdims=True))
        w_k.append(jnp.sum(jnp.where(hit, wc, 0.0), axis=0, keepdims=True))
    dst_ref[...] = jnp.concatenate(dst, axis=0).astype(I32) * SLAB
    wk_ref[...] = jnp.concatenate(w_k, axis=0)


def _slots(kmap, wc, off):
    ts = TILE_ROWS
    tiles_per_block = MOE_BLOCK // ts
    route_spec = pl.BlockSpec((N_EXPERTS, ts), lambda i: (0, i))
    slot_spec = pl.BlockSpec((TOP_K, ts), lambda i: (0, i))
    tri = (lax.broadcasted_iota(I32, (ts, ts), 0) < lax.broadcasted_iota(I32, (ts, ts), 1)).astype(BF16)
    return pl.pallas_call(
        _slots_kernel,
        out_shape=(jax.ShapeDtypeStruct((TOP_K, N_TOK), I32), jax.ShapeDtypeStruct((TOP_K, N_TOK), F32)),
        grid=(N_TOK // ts,),
        in_specs=[route_spec, route_spec,
                  pl.BlockSpec((None, N_EXPERTS, 1), lambda i: (i // tiles_per_block, 0, 0)),
                  _const_spec(tri.shape)],
        out_specs=(slot_spec, slot_spec),
        scratch_shapes=[pltpu.VMEM((N_EXPERTS, 1), F32)],
        compiler_params=pltpu.CompilerParams(
            dimension_semantics=("arbitrary",), vmem_limit_bytes=VMEM_LIMIT),
        name="slots",
    )(kmap, wc, off, tri)


def _moe_kernel(cnt_ref, off_ref, dst_ref, w_ref, h_ref, wg_hbm, wu_hbm, wd_hbm, o_ref,
                stage_ref, wg_buf, wu_buf, wd_buf, sem):
    blk = pl.program_id(0)

    def weight_copies(step, slot):
        experts = pl.ds(step * MOE_EXPERTS_PER_STEP, MOE_EXPERTS_PER_STEP)
        pairs = ((wg_hbm, wg_buf), (wu_hbm, wu_buf), (wd_hbm, wd_buf))
        return [pltpu.make_async_copy(hbm.at[experts], buf.at[slot], sem.at[i, slot])
                for i, (hbm, buf) in enumerate(pairs)]

    def fetch(step, slot):
        for copy in weight_copies(step, slot):
            copy.start()

    def arrived(step, slot):
        for copy in weight_copies(step, slot):
            copy.wait()

    fetch(0, 0)

    @pl.when(blk == 0)
    def _():
        stage_ref[TOP_K * MOE_BLOCK * SLAB:, :] = jnp.zeros((MOE_CHUNK * SLAB, LANES), U32)

    def slab_at(ref, sublane_row):
        return ref.at[pl.ds(pl.multiple_of(sublane_row, SLAB), SLAB), :]

    group = MOE_UNROLL * TOP_K

    def selections(ref, i):
        return ref.at[0, pl.ds(pl.multiple_of(i * group, group), group)]

    def dispatch(i, carry):
        dst = selections(dst_ref, i)
        for j in range(MOE_UNROLL):
            slab = slab_at(h_ref, (i * MOE_UNROLL + j) * SLAB)[...]
            for k in range(TOP_K):
                slab_at(stage_ref, dst[j * TOP_K + k])[...] = slab
        return carry

    lax.fori_loop(0, MOE_BLOCK // MOE_UNROLL, dispatch, 0)

    def run_chunk(slot, j, first, live_rows, rows):
        base = first * SLAB
        words = _load_slabs(stage_ref, base, rows)
        halves = [_unpack_words(w) for w in words]
        xs = jnp.concatenate([lo for lo, _ in halves] + [hi for _, hi in halves], axis=-1)
        xs = xs.astype(BF16)
        g = jnp.dot(xs, wg_buf[slot, j], preferred_element_type=F32)
        act = (g * _sigmoid(g)) * jnp.dot(xs, wu_buf[slot, j], preferred_element_type=F32)
        out = _pack_rows(jnp.dot(act.astype(BF16), wd_buf[slot, j], preferred_element_type=F32))
        live = lax.broadcasted_iota(I32, (rows, LANES), 0) < live_rows
        for s in range(SLAB):
            merged = jnp.where(live, out[:, s * LANES:(s + 1) * LANES], words[s])
            stage_ref[pl.ds(base + s, rows, stride=SLAB), :] = merged

    def run_expert(slot, j, first, n_rows):
        n_whole = jnp.maximum(n_rows - 1, 0) // MOE_CHUNK

        def whole(c, carry):
            run_chunk(slot, j, first + c * MOE_CHUNK, MOE_CHUNK, MOE_CHUNK)
            return carry

        lax.fori_loop(0, n_whole, whole, 0)
        rest = n_rows - n_whole * MOE_CHUNK
        size_class = (rest + MOE_GRAIN - 1) // MOE_GRAIN
        for cls in range(1, MOE_CHUNK // MOE_GRAIN + 1):
            @pl.when(size_class == cls)
            def _(cls=cls):
                run_chunk(slot, j, first + n_whole * MOE_CHUNK, rest, cls * MOE_GRAIN)

    def expert_step(step, carry):
        slot = step % 2

        @pl.when(step + 1 < MOE_EXPERT_STEPS)
        def _():
            fetch(step + 1, 1 - slot)

        arrived(step, slot)
        for j in range(MOE_EXPERTS_PER_STEP):
            e = blk * N_EXPERTS + step * MOE_EXPERTS_PER_STEP + j
            run_expert(slot, j, off_ref[e], cnt_ref[e])
        return carry

    lax.fori_loop(0, MOE_EXPERT_STEPS, expert_step, 0)

    def combine(i, carry):
        dst = selections(dst_ref, i)
        wts = selections(w_ref, i)
        for j in range(MOE_UNROLL):
            t = i * MOE_UNROLL + j
            terms = []
            for k in range(TOP_K):
                w = wts[j * TOP_K + k]
                lo, hi = _unpack_words(slab_at(stage_ref, dst[j * TOP_K + k])[...])
                terms.append((lo * w, hi * w))
            while len(terms) > 1:
                terms = [(a[0] + b[0], a[1] + b[1]) for a, b in zip(terms[0::2], terms[1::2])]
            slab_at(o_ref, t * SLAB)[...] = pltpu.pack_elementwise(list(terms[0]), packed_dtype=BF16)
        return carry

    lax.fori_loop(0, MOE_BLOCK // MOE_UNROLL, combine, 0)


def _moe(cnt, off, dst, w, h2s, wg, wu, wd):
    smem_spec = pl.BlockSpec((None, 1, TOP_K * MOE_BLOCK), lambda b, *_: (b, 0, 0), memory_space=pltpu.SMEM)
    hbm_spec = pl.BlockSpec(memory_space=pl.ANY)
    weight_buf = lambda rows, cols: pltpu.VMEM((2, MOE_EXPERTS_PER_STEP, rows, cols), BF16)
    return pl.pallas_call(
        _moe_kernel,
        out_shape=jax.ShapeDtypeStruct((N_TOK * SLAB, LANES), U32),
        grid_spec=pltpu.PrefetchScalarGridSpec(
            num_scalar_prefetch=2,
            grid=(MOE_BLOCKS,),
            in_specs=[smem_spec, smem_spec,
                      pl.BlockSpec((MOE_BLOCK * SLAB, LANES), lambda b, *_: (b, 0),
                                   pipeline_mode=pl.Buffered(1)),
                      hbm_spec, hbm_spec, hbm_spec],
            out_specs=pl.BlockSpec((MOE_BLOCK * SLAB, LANES), lambda b, *_: (b, 0)),
            scratch_shapes=[pltpu.VMEM((STAGE_ROWS * SLAB, LANES), U32),
                            weight_buf(D_MODEL, EXPERT_HIDDEN), weight_buf(D_MODEL, EXPERT_HIDDEN),
                            weight_buf(EXPERT_HIDDEN, D_MODEL),
                            pltpu.SemaphoreType.DMA((3, 2))]),
        compiler_params=pltpu.CompilerParams(
            dimension_semantics=("arbitrary",), vmem_limit_bytes=VMEM_LIMIT),
        name="moe",
    )(cnt, off, dst, w, h2s, wg, wu, wd)


def _fin_kernel(x1s_ref, r_ref, gt2_ref, fg_ref, o_ref):
    halves = [_unpack_words(w) for w in _load_slabs(r_ref, 0, x1s_ref.shape[0])]
    routed = jnp.concatenate([lo for lo, _ in halves] + [hi for _, hi in halves], axis=-1)
    x2 = x1s_ref[...] + _per_batch(routed, gt2_ref[...], lambda r, m: r * m)
    ms = jnp.mean(x2 * x2, axis=-1, keepdims=True)
    o_ref[...] = _batch_major(x2 * lax.rsqrt(ms + EPS) * fg_ref[...])


def _fin(x1s, routed, gt2, fg):
    ts = FIN_STEPS * BATCH
    return pl.pallas_call(
        _fin_kernel,
        out_shape=jax.ShapeDtypeStruct((BATCH, SEQ, D_MODEL), F32),
        grid=(SEQ // FIN_STEPS,),
        in_specs=[pl.BlockSpec((ts, D_MODEL), lambda i: (i, 0)),
                  pl.BlockSpec((ts * SLAB, LANES), lambda i: (i, 0)),
                  _const_spec((1, BATCH, D_MODEL)),
                  _const_spec((1, D_MODEL))],
        out_specs=pl.BlockSpec((BATCH, FIN_STEPS, D_MODEL), lambda i: (0, i, 0)),
        compiler_params=pltpu.CompilerParams(
            dimension_semantics=("parallel",), vmem_limit_bytes=VMEM_LIMIT),
        name="fin",
    )(x1s, routed, gt2, fg)


def kernel(x, c, w_ada, b_ada, norm1_g, w_in, s5_lambda_re, s5_lambda_im, s5_log_dt, s5_b_re, s5_b_im, s5_c_re, s5_c_im, s5_d, s5_glu_wv, s5_glu_wg, conv_dw_w, conv_dw_b, conv_ln_g, conv_ln_b, conv_pw_w, w_out, norm2_g, router_w, router_bias, exp_w_gate, exp_w_up, exp_w_down, shared_w_gate, shared_w_up, shared_w_down, final_norm_g):
    l = 0
    row = lambda a: a.reshape(1, -1)

    mod = _ada(c, w_ada[l], b_ada[l])
    mods = [m.reshape(1, BATCH, D_MODEL) for m in jnp.split(mod, N_MOD, axis=-1)]

    abar_r, abar_i, bbar_r, bbar_i = _s5prep(s5_lambda_re[l], s5_lambda_im[l], s5_log_dt[l],
                                             s5_b_re[l], s5_b_im[l])
    per_group = lambda a: a.reshape(S5_GROUPS, S5_GROUP, S5_STATE)
    a_r = per_group(abar_r)[:, 0, :].reshape(1, N_STATE)
    a_i = per_group(abar_i)[:, 0, :].reshape(1, N_STATE)
    bmat = jnp.concatenate([_block_diag(per_group(bbar_r)), _block_diag(per_group(bbar_i))],
                           axis=1).astype(BF16)
    c_r = _block_diag(jnp.transpose(s5_c_re[l], (0, 2, 1))).astype(BF16)
    c_i = _block_diag(jnp.transpose(s5_c_im[l], (0, 2, 1))).astype(BF16)

    o1 = S5_WIDTH
    o2 = o1 + CONV_WIDTH
    o3 = o2 + CONV_WIDTH
    o4 = o3 + D_MODEL
    w = w_in[l].astype(BF16)
    sgu = jnp.concatenate([shared_w_gate[l], shared_w_up[l]], axis=1).astype(BF16)
    consts = (row(norm1_g[l]), row(norm2_g[l]),
              w[:, :o1], w[:, o1:o2], w[:, o2:o3], w[:, o3:o4], w[:, o4:],
              *_block_diag_tiles(bmat, c_r, c_i), a_r, a_i, row(s5_d[l]),
              s5_glu_wv[l].astype(BF16), s5_glu_wg[l].astype(BF16),
              conv_dw_w[l], row(conv_dw_b[l]), row(conv_ln_g[l]), row(conv_ln_b[l]),
              conv_pw_w[l].astype(BF16), w_out[l].astype(BF16),
              router_w[l].T.astype(BF16), router_bias[l].reshape(N_EXPERTS, 1),
              sgu, shared_w_down[l].astype(BF16))
    x1s, h2s, kmap, wc, cnt, wg, wu, wd = _tok(x, mods, consts,
                                               (exp_w_gate[l], exp_w_up[l], exp_w_down[l]))

    off = jnp.cumsum(cnt, axis=1) - cnt
    dst, w_k = _slots(kmap, wc, off)
    as_ints = lambda a: a.astype(I32).reshape(-1)
    per_block = lambda a: a.T.reshape(MOE_BLOCKS, 1, MOE_BLOCK * TOP_K)

    routed = _moe(as_ints(cnt), as_ints(off), per_block(dst), per_block(w_k), h2s, wg, wu, wd)
    return _fin(x1s, routed, mods[5], row(final_norm_g))
```

```python
import jax
import jax.numpy as jnp
from jax import lax
from jax.experimental import pallas as pl
from jax.experimental.pallas import tpu as pltpu

D_MODEL = 1024
BATCH = 8
SEQ = 4096
N_TOK = BATCH * SEQ
S5_WIDTH = 512
S5_GROUP = 16
S5_GROUPS = 32
S5_STATE = 64
N_STATE = S5_GROUPS * S5_STATE
CONV_WIDTH = 512
CONV_KERNEL = 31
N_EXPERTS = 64
TOP_K = 8
EXPERT_HIDDEN = 256
SHARED_HIDDEN = 256
ROUTED_SCALE = 2.5
N_MOD = 6
EPS = 1e-6

LANES = 128
SUBLANES = 8
VMEM_LIMIT = 56 * 1024 * 1024

TILE_STEPS = 64
TILE_ROWS = TILE_STEPS * BATCH
FIN_STEPS = 128
CONV_HALO = (CONV_KERNEL - 1) * BATCH
CONV_CHUNK = 64
SCAN_LANES = 512
MOE_BLOCK = 2048
MOE_BLOCKS = N_TOK // MOE_BLOCK
MOE_CHUNK = 512
MOE_GRAIN = 64
MOE_EXPERTS_PER_STEP = 2
MOE_EXPERT_STEPS = N_EXPERTS // MOE_EXPERTS_PER_STEP
MOE_UNROLL = 4
SLAB = 4
STAGE_ROWS = TOP_K * MOE_BLOCK + MOE_CHUNK

F32 = jnp.float32
BF16 = jnp.bfloat16
U32 = jnp.uint32
I32 = jnp.int32


_X_SPEC = pl.BlockSpec((BATCH, TILE_STEPS, D_MODEL), lambda i: (0, i, 0))


def _time_major(a):
    return jnp.swapaxes(a, 0, 1).reshape(a.shape[0] * a.shape[1], a.shape[2])


def _batch_major(a):
    return jnp.swapaxes(a.reshape(a.shape[0] // BATCH, BATCH, a.shape[1]), 0, 1)


def _per_batch(rows, mod, op):
    r3 = rows.reshape(rows.shape[0] // BATCH, BATCH, rows.shape[1])
    return op(r3, mod).reshape(rows.shape)


def _pack_rows(x):
    half = D_MODEL // 2
    return pltpu.pack_elementwise([x[:, :half], x[:, half:]], packed_dtype=BF16)


def _unpack_words(w):
    lo = pltpu.unpack_elementwise(w, index=0, packed_dtype=BF16, unpacked_dtype=F32)
    hi = pltpu.unpack_elementwise(w, index=1, packed_dtype=BF16, unpacked_dtype=F32)
    return lo, hi


def _store_slabs(ref, base, words):
    rows = words.shape[0]
    for s in range(SLAB):
        ref[pl.ds(base + s, rows, stride=SLAB), :] = words[:, s * LANES:(s + 1) * LANES]


def _load_slabs(ref, base, rows):
    return [ref[pl.ds(base + s, rows, stride=SLAB), :] for s in range(SLAB)]


def _sigmoid(x):
    return jax.nn.sigmoid(x)


def _const_spec(shape):
    zeros = (0,) * len(shape)
    return pl.BlockSpec(shape, lambda *_: zeros, pipeline_mode=pl.Buffered(1))


def _ada_kernel(c_ref, w_ref, b_ref, o_ref):
    c = c_ref[...]
    c_act = c * _sigmoid(c)
    o_ref[...] = jnp.dot(c_act, w_ref[...], preferred_element_type=F32,
                         precision=lax.Precision.HIGHEST) + b_ref[...]


def _ada(c, w_ada, b_ada):
    n_out = N_MOD * D_MODEL
    blk = 1536
    return pl.pallas_call(
        _ada_kernel,
        out_shape=jax.ShapeDtypeStruct((BATCH, n_out), F32),
        grid=(n_out // blk,),
        in_specs=[pl.BlockSpec((BATCH, D_MODEL), lambda j: (0, 0)),
                  pl.BlockSpec((D_MODEL, blk), lambda j: (0, j)),
                  pl.BlockSpec((1, blk), lambda j: (0, j))],
        out_specs=pl.BlockSpec((BATCH, blk), lambda j: (0, j)),
        compiler_params=pltpu.CompilerParams(vmem_limit_bytes=VMEM_LIMIT),
        name="ada",
    )(c, w_ada, b_ada.reshape(1, n_out))


def _s5prep_kernel(lr_ref, li_ref, ldt_ref, br_ref, bi_ref, ar_ref, ai_ref, bbr_ref, bbi_ref):
    lr = lr_ref[...]
    li = li_ref[...]
    dt = jnp.exp(ldt_ref[...])
    mag = jnp.exp(lr * dt)
    abar_r = mag * jnp.cos(li * dt)
    abar_i = mag * jnp.sin(li * dt)
    den = lr * lr + li * li
    nr = abar_r - 1.0
    ni = abar_i
    k_r = (nr * lr + ni * li) / den
    k_i = (ni * lr - nr * li) / den
    br = br_ref[...]
    bi = bi_ref[...]
    ar_ref[...] = abar_r
    ai_ref[...] = abar_i
    bbr_ref[...] = k_r * br - k_i * bi
    bbi_ref[...] = k_r * bi + k_i * br


def _s5prep(lam_re, lam_im, log_dt, b_re, b_im):
    rows = S5_GROUPS * S5_GROUP
    rep = lambda a: jnp.broadcast_to(a[:, None, :], (S5_GROUPS, S5_GROUP, S5_STATE)).reshape(rows, S5_STATE)
    lr = rep(lam_re)
    li = rep(lam_im)
    ldt = rep(jnp.broadcast_to(log_dt[:, None], (S5_GROUPS, S5_STATE)))
    br = jnp.transpose(b_re, (0, 2, 1)).reshape(rows, S5_STATE)
    bi = jnp.transpose(b_im, (0, 2, 1)).reshape(rows, S5_STATE)
    shp = jax.ShapeDtypeStruct((rows, S5_STATE), F32)
    return pl.pallas_call(_s5prep_kernel, out_shape=(shp, shp, shp, shp), name="s5prep")(lr, li, ldt, br, bi)


def _block_diag(blocks):
    g, r, c = blocks.shape
    eye = jnp.eye(g, dtype=blocks.dtype)
    return jnp.einsum("grc,gk->grkc", blocks, eye).reshape(g * r, g * c)


def _gelu_tanh(x):
    sqrt_2_over_pi = 0.7978845608028654
    cdf = 0.5 * (1.0 + jnp.tanh(sqrt_2_over_pi * (x + 0.044715 * (x * x * x))))
    return x * cdf


def _rms_mod(x, g, scale, shift):
    ms = jnp.mean(x * x, axis=-1, keepdims=True)
    h = x * lax.rsqrt(ms + EPS) * g
    return _per_batch(h, (scale, shift), lambda r, m: r * (1.0 + m[0]) + m[1])


def _tok_kernel(x_ref, sh1_ref, sc1_ref, gt1_ref, sh2_ref, sc2_ref, gt2_ref, g1_ref, g2_ref,
                wu_ref, wcv_ref, wcg_ref, wa_ref, wb_ref,
                bsp_ref, csr_ref, csi_ref, ar_ref, ai_ref, d_ref, wv_ref, wg_ref,
                dww_ref, dwb_ref, lng_ref, lnb_ref, pw_ref, wout_ref,
                rw_ref, rb_ref, sgu_ref, sd_ref, ew_gate_ref, ew_up_ref, ew_down_ref,
                x1s_ref, h2_ref, kmap_ref, wc_ref, cnt_ref, eb_gate_ref, eb_up_ref, eb_down_ref,
                bu_ref, sb_ref, sr_ref, si_ref, cbuf_ref, cv_ref):
    step = pl.program_id(0)

    @pl.when(step == 0)
    def _():
        sr_ref[...] = jnp.zeros_like(sr_ref)
        si_ref[...] = jnp.zeros_like(si_ref)
        cbuf_ref[0:CONV_HALO, :] = jnp.zeros((CONV_HALO, CONV_WIDTH), F32)

    x = _time_major(x_ref[...])
    h = _rms_mod(x, g1_ref[...], sc1_ref[...], sh1_ref[...]).astype(BF16)
    u = jnp.dot(h, wu_ref[...], preferred_element_type=F32)
    u_b = u.astype(BF16)
    zc = (jnp.dot(h, wcv_ref[...], preferred_element_type=F32)
          * _sigmoid(jnp.dot(h, wcg_ref[...], preferred_element_type=F32)))

    for j in range(N_STATE // LANES):
        kb = (j * LANES // S5_STATE * S5_GROUP) // LANES
        res = jnp.dot(u_b[:, kb * LANES:(kb + 1) * LANES], bsp_ref[j], preferred_element_type=F32)
        bu_ref[:, j * LANES:(j + 1) * LANES] = res[:, :LANES]
        bu_ref[:, N_STATE + j * LANES:N_STATE + (j + 1) * LANES] = res[:, LANES:]

    gate_a = _sigmoid(jnp.dot(h, wa_ref[...], preferred_element_type=F32))
    gate_b = _sigmoid(jnp.dot(h, wb_ref[...], preferred_element_type=F32))

    for lg in range(N_STATE // SCAN_LANES):
        lo = lg * SCAN_LANES
        re = slice(lo, lo + SCAN_LANES)
        im = slice(N_STATE + lo, N_STATE + lo + SCAN_LANES)
        a_r = jnp.broadcast_to(ar_ref[:, re], (SUBLANES, SCAN_LANES))
        a_i = jnp.broadcast_to(ai_ref[:, re], (SUBLANES, SCAN_LANES))
        s_r = sr_ref[:, re]
        s_i = si_ref[:, re]
        for t in range(0, TILE_STEPS, 2):
            pair_r, pair_i = [], []
            for tt in (t, t + 1):
                rows = slice(tt * SUBLANES, (tt + 1) * SUBLANES)
                s_r, s_i = (a_r * s_r - a_i * s_i + bu_ref[rows, re],
                            a_r * s_i + a_i * s_r + bu_ref[rows, im])
                pair_r.append(s_r)
                pair_i.append(s_i)
            rows2 = slice(t * SUBLANES, (t + 2) * SUBLANES)
            sb_ref[rows2, re] = jnp.concatenate(pair_r, axis=0).astype(BF16)
            sb_ref[rows2, im] = jnp.concatenate(pair_i, axis=0).astype(BF16)
        sr_ref[:, re] = s_r
        si_ref[:, re] = s_i

    half_s = N_STATE // 2
    ys = []
    for hf in range(2):
        s_re = sb_ref[:, hf * half_s:(hf + 1) * half_s]
        s_im = sb_ref[:, N_STATE + hf * half_s:N_STATE + (hf + 1) * half_s]
        ys.append(jnp.dot(s_re, csr_ref[hf], preferred_element_type=F32)
                  - jnp.dot(s_im, csi_ref[hf], preferred_element_type=F32))
    y = jnp.concatenate(ys, axis=-1) + d_ref[...] * u
    z = _gelu_tanh(y).astype(BF16)
    y_a = (jnp.dot(z, wv_ref[...], preferred_element_type=F32)
           * _sigmoid(jnp.dot(z, wg_ref[...], preferred_element_type=F32)))

    cbuf_ref[CONV_HALO:CONV_HALO + TILE_ROWS, :] = zc
    for ci in range(TILE_ROWS // CONV_CHUNK):
        r0 = ci * CONV_CHUNK
        for lt in range(CONV_WIDTH // LANES):
            ls = slice(lt * LANES, (lt + 1) * LANES)
            win = cbuf_ref[r0:r0 + CONV_CHUNK + CONV_HALO, ls]
            acc = jnp.broadcast_to(dwb_ref[:, ls], (CONV_CHUNK, LANES))
            for k in range(CONV_KERNEL):
                acc = acc + dww_ref[k:k + 1, ls] * win[k * BATCH:k * BATCH + CONV_CHUNK]
            cv_ref[r0:r0 + CONV_CHUNK, ls] = acc
    cbuf_ref[0:CONV_HALO, :] = cbuf_ref[TILE_ROWS:TILE_ROWS + CONV_HALO, :]

    cv = cv_ref[...]
    mu = jnp.mean(cv, axis=-1, keepdims=True)
    var = jnp.mean(jnp.square(cv - mu), axis=-1, keepdims=True)
    ln = (cv - mu) * lax.rsqrt(var + EPS) * lng_ref[...] + lnb_ref[...]
    zb = (ln * _sigmoid(ln)).astype(BF16)
    y_b = jnp.dot(zb, pw_ref[...], preferred_element_type=F32)

    m = gate_a * y_a + gate_b * y_b
    mixed = jnp.dot(m.astype(BF16), wout_ref[...], preferred_element_type=F32)

    x1 = x + _per_batch(mixed, gt1_ref[...], lambda r, m: r * m)
    hb = _rms_mod(x1, g2_ref[...], sc2_ref[...], sh2_ref[...]).astype(BF16)
    _store_slabs(h2_ref, 0, _pack_rows(hb.astype(F32)))

    scores = _sigmoid(lax.dot_general(rw_ref[...], hb, (((1,), (1,)), ((), ())),
                                      preferred_element_type=F32))
    expert = lax.broadcasted_iota(I32, scores.shape, 0)
    neg_inf = jnp.float32(-jnp.inf)
    work = scores + rb_ref[...]
    kmap = jnp.zeros(scores.shape, I32)
    for k in range(TOP_K):
        top = jnp.max(work, axis=0, keepdims=True)
        idx = jnp.min(jnp.where(work == top, expert, N_EXPERTS), axis=0, keepdims=True)
        hit = expert == idx
        kmap = jnp.where(hit, k + 1, kmap)
        work = jnp.where(hit, neg_inf, work)
    sel = kmap > 0
    w_sel = jnp.where(sel, scores, 0.0)
    kmap_ref[...] = kmap
    wc_ref[...] = w_sel / jnp.sum(w_sel, axis=0, keepdims=True) * ROUTED_SCALE

    @pl.when(step % (MOE_BLOCK // TILE_ROWS) == 0)
    def _():
        cnt_ref[...] = jnp.zeros_like(cnt_ref)

    cnt_ref[...] += jnp.sum(sel.astype(F32), axis=1, keepdims=True)

    gu = jnp.dot(hb, sgu_ref[...], preferred_element_type=F32)
    g = gu[:, :SHARED_HIDDEN]
    act = (g * _sigmoid(g)) * gu[:, SHARED_HIDDEN:]
    shared = jnp.dot(act.astype(BF16), sd_ref[...], preferred_element_type=F32)
    x1s_ref[...] = x1 + _per_batch(shared, gt2_ref[...], lambda r, m: r * m)

    eb_gate_ref[...] = ew_gate_ref[...].astype(BF16)
    eb_up_ref[...] = ew_up_ref[...].astype(BF16)
    eb_down_ref[...] = ew_down_ref[...].astype(BF16)


def _block_diag_tiles(bmat, c_r, c_i):
    tiles = []
    for j in range(N_STATE // LANES):
        kb = (j * LANES // S5_STATE * S5_GROUP) // LANES
        k_rows = slice(kb * LANES, (kb + 1) * LANES)
        tiles.append(jnp.concatenate([bmat[k_rows, j * LANES:(j + 1) * LANES],
                                      bmat[k_rows, N_STATE + j * LANES:N_STATE + (j + 1) * LANES]], axis=1))
    half_c, half_s = S5_WIDTH // 2, N_STATE // 2
    halves = lambda m: jnp.stack([m[hf * half_s:(hf + 1) * half_s, hf * half_c:(hf + 1) * half_c]
                                  for hf in range(2)])
    return jnp.stack(tiles), halves(c_r), halves(c_i)


def _tok(x, mods, consts, expert_weights):
    ts = TILE_ROWS
    steps = SEQ // TILE_STEPS
    tiles_per_block = MOE_BLOCK // ts
    row_spec = lambda rows, c: pl.BlockSpec((rows, c), lambda i: (i, 0))
    route_spec = pl.BlockSpec((N_EXPERTS, ts), lambda i: (0, i))
    per_step = N_EXPERTS // steps
    assert per_step * steps == N_EXPERTS
    cast_specs = [pl.BlockSpec((per_step,) + a.shape[1:], lambda i: (i, 0, 0)) for a in expert_weights]
    return pl.pallas_call(
        _tok_kernel,
        out_shape=(jax.ShapeDtypeStruct((N_TOK, D_MODEL), F32),
                   jax.ShapeDtypeStruct((N_TOK * SLAB, LANES), U32),
                   jax.ShapeDtypeStruct((N_EXPERTS, N_TOK), I32),
                   jax.ShapeDtypeStruct((N_EXPERTS, N_TOK), F32),
                   jax.ShapeDtypeStruct((MOE_BLOCKS, N_EXPERTS, 1), F32))
                  + tuple(jax.ShapeDtypeStruct(a.shape, BF16) for a in expert_weights),
        grid=(steps,),
        in_specs=[_X_SPEC] + [_const_spec(a.shape) for a in tuple(mods) + tuple(consts)] + cast_specs,
        out_specs=(row_spec(ts, D_MODEL), row_spec(ts * SLAB, LANES), route_spec, route_spec,
                   pl.BlockSpec((None, N_EXPERTS, 1), lambda i: (i // tiles_per_block, 0, 0)))
                  + tuple(cast_specs),
        scratch_shapes=[pltpu.VMEM((ts, 2 * N_STATE), F32),
                        pltpu.VMEM((ts, 2 * N_STATE), BF16),
                        pltpu.VMEM((SUBLANES, N_STATE), F32),
                        pltpu.VMEM((SUBLANES, N_STATE), F32),
                        pltpu.VMEM((CONV_HALO + ts, CONV_WIDTH), F32),
                        pltpu.VMEM((ts, CONV_WIDTH), F32)],
        compiler_params=pltpu.CompilerParams(
            dimension_semantics=("arbitrary",), vmem_limit_bytes=VMEM_LIMIT),
        name="tok",
    )(x, *mods, *consts, *expert_weights)


def _slots_kernel(kmap_ref, wc_ref, off_ref, tri_ref, dst_ref, wk_ref, seen_ref):
    @pl.when(pl.program_id(0) % (MOE_BLOCK // TILE_ROWS) == 0)
    def _():
        seen_ref[...] = jnp.zeros_like(seen_ref)

    kmap = kmap_ref[...]
    wc = wc_ref[...]
    sel = (kmap > 0).astype(F32)
    seen = seen_ref[...]
    slot = jnp.dot(sel.astype(BF16), tri_ref[...], preferred_element_type=F32) + (seen + off_ref[...])
    seen_ref[...] = seen + jnp.sum(sel, axis=1, keepdims=True)
    dst, w_k = [], []
    for k in range(TOP_K):
        hit = kmap == k + 1
        dst.append(jnp.sum(jnp.where(hit, slot, 0.0), axis=0, keepdims=True))
        w_k.append(jnp.sum(jnp.where(hit, wc, 0.0), axis=0, keepdims=True))
    dst_ref[...] = jnp.concatenate(dst, axis=0).astype(I32) * SLAB
    wk_ref[...] = jnp.concatenate(w_k, axis=0)


def _slots(kmap, wc, off):
    ts = TILE_ROWS
    tiles_per_block = MOE_BLOCK // ts
    route_spec = pl.BlockSpec((N_EXPERTS, ts), lambda i: (0, i))
    slot_spec = pl.BlockSpec((TOP_K, ts), lambda i: (0, i))
    tri = (lax.broadcasted_iota(I32, (ts, ts), 0) < lax.broadcasted_iota(I32, (ts, ts), 1)).astype(BF16)
    return pl.pallas_call(
        _slots_kernel,
        out_shape=(jax.ShapeDtypeStruct((TOP_K, N_TOK), I32), jax.ShapeDtypeStruct((TOP_K, N_TOK), F32)),
        grid=(N_TOK // ts,),
        in_specs=[route_spec, route_spec,
                  pl.BlockSpec((None, N_EXPERTS, 1), lambda i: (i // tiles_per_block, 0, 0)),
                  _const_spec(tri.shape)],
        out_specs=(slot_spec, slot_spec),
        scratch_shapes=[pltpu.VMEM((N_EXPERTS, 1), F32)],
        compiler_params=pltpu.CompilerParams(
            dimension_semantics=("arbitrary",), vmem_limit_bytes=VMEM_LIMIT),
        name="slots",
    )(kmap, wc, off, tri)


def _moe_kernel(cnt_ref, off_ref, order_ref, dst_ref, w_ref, h_ref, wg_hbm, wu_hbm, wd_hbm, o_ref,
                stage_ref, wg_buf, wu_buf, wd_buf, sem):
    blk = pl.program_id(0)

    def step_experts(step):
        a = order_ref[blk * N_EXPERTS + 2 * step]
        b = order_ref[blk * N_EXPERTS + 2 * step + 1]
        return jnp.minimum(a, b), jnp.maximum(a, b)

    def weight_copies(step, slot):
        pairs = ((wg_hbm, wg_buf), (wu_hbm, wu_buf), (wd_hbm, wd_buf))
        return [pltpu.make_async_copy(hbm.at[pl.ds(e, 1)], buf.at[slot, pl.ds(j, 1)], sem.at[i, slot, j])
                for i, (hbm, buf) in enumerate(pairs) for j, e in enumerate(step_experts(step))]

    def fetch(step, slot):
        for copy in weight_copies(step, slot):
            copy.start()

    def arrived(step, slot):
        for copy in weight_copies(step, slot):
            copy.wait()

    fetch(0, 0)

    @pl.when(blk == 0)
    def _():
        stage_ref[TOP_K * MOE_BLOCK * SLAB:, :] = jnp.zeros((MOE_CHUNK * SLAB, LANES), U32)

    def slab_at(ref, sublane_row):
        return ref.at[pl.ds(pl.multiple_of(sublane_row, SLAB), SLAB), :]

    group = MOE_UNROLL * TOP_K

    def selections(ref, i):
        return ref.at[0, pl.ds(pl.multiple_of(i * group, group), group)]

    def dispatch(i, carry):
        dst = selections(dst_ref, i)
        for j in range(MOE_UNROLL):
            slab = slab_at(h_ref, (i * MOE_UNROLL + j) * SLAB)[...]
            for k in range(TOP_K):
                slab_at(stage_ref, dst[j * TOP_K + k])[...] = slab
        return carry

    lax.fori_loop(0, MOE_BLOCK // MOE_UNROLL, dispatch, 0)

    def run_chunks(slot, firsts, live_rows, rows):
        words, outs = [], []
        for j, first in enumerate(firsts):
            words.append(_load_slabs(stage_ref, first * SLAB, rows))
        for j in range(len(firsts)):
            halves = [_unpack_words(w) for w in words[j]]
            xs = jnp.concatenate([lo for lo, _ in halves] + [hi for _, hi in halves], axis=-1)
            xs = xs.astype(BF16)
            g = jnp.dot(xs, wg_buf[slot, j], preferred_element_type=F32)
            act = (g * _sigmoid(g)) * jnp.dot(xs, wu_buf[slot, j], preferred_element_type=F32)
            outs.append(_pack_rows(jnp.dot(act.astype(BF16), wd_buf[slot, j], preferred_element_type=F32)))
        row = lax.broadcasted_iota(I32, (rows, LANES), 0)
        for j, first in enumerate(firsts):
            live = row < live_rows[j]
            for s in range(SLAB):
                merged = jnp.where(live, outs[j][:, s * LANES:(s + 1) * LANES], words[j][s])
                stage_ref[pl.ds(first * SLAB + s, rows, stride=SLAB), :] = merged

    def expert_step(step, carry):
        slot = step % 2

        @pl.when(step + 1 < MOE_EXPERT_STEPS)
        def _():
            fetch(step + 1, 1 - slot)

        arrived(step, slot)
        experts = step_experts(step)
        firsts = [off_ref[blk * N_EXPERTS + e] for e in experts]
        counts = [cnt_ref[blk * N_EXPERTS + e] for e in experts]
        rests = []
        for j in range(MOE_EXPERTS_PER_STEP):
            n_whole = jnp.maximum(counts[j] - 1, 0) // MOE_CHUNK

            def whole(c, carry, j=j):
                run_chunks_single(slot, j, firsts[j] + c * MOE_CHUNK)
                return carry

            lax.fori_loop(0, n_whole, whole, 0)
            firsts[j] = firsts[j] + n_whole * MOE_CHUNK
            rests.append(counts[j] - n_whole * MOE_CHUNK)
        size_class = (jnp.maximum(rests[0], rests[1]) + MOE_GRAIN - 1) // MOE_GRAIN
        for cls in range(1, MOE_CHUNK // MOE_GRAIN + 1):
            @pl.when(size_class == cls)
            def _(cls=cls):
                run_chunks(slot, firsts, rests, cls * MOE_GRAIN)
        return carry

    def run_chunks_single(slot, j, first):
        words = _load_slabs(stage_ref, first * SLAB, MOE_CHUNK)
        halves = [_unpack_words(w) for w in words]
        xs = jnp.concatenate([lo for lo, _ in halves] + [hi for _, hi in halves], axis=-1).astype(BF16)
        g = jnp.dot(xs, wg_buf[slot, j], preferred_element_type=F32)
        act = (g * _sigmoid(g)) * jnp.dot(xs, wu_buf[slot, j], preferred_element_type=F32)
        _store_slabs(stage_ref, first * SLAB,
                     _pack_rows(jnp.dot(act.astype(BF16), wd_buf[slot, j], preferred_element_type=F32)))

    lax.fori_loop(0, MOE_EXPERT_STEPS, expert_step, 0)

    def combine(i, carry):
        dst = selections(dst_ref, i)
        wts = selections(w_ref, i)
        for j in range(MOE_UNROLL):
            t = i * MOE_UNROLL + j
            terms = []
            for k in range(TOP_K):
                w = wts[j * TOP_K + k]
                lo, hi = _unpack_words(slab_at(stage_ref, dst[j * TOP_K + k])[...])
                terms.append((lo * w, hi * w))
            while len(terms) > 1:
                terms = [(a[0] + b[0], a[1] + b[1]) for a, b in zip(terms[0::2], terms[1::2])]
            slab_at(o_ref, t * SLAB)[...] = pltpu.pack_elementwise(list(terms[0]), packed_dtype=BF16)
        return carry

    lax.fori_loop(0, MOE_BLOCK // MOE_UNROLL, combine, 0)


def _moe(cnt, off, order, dst, w, h2s, wg, wu, wd):
    smem_spec = pl.BlockSpec((None, 1, TOP_K * MOE_BLOCK), lambda b, *_: (b, 0, 0), memory_space=pltpu.SMEM)
    hbm_spec = pl.BlockSpec(memory_space=pl.ANY)
    weight_buf = lambda rows, cols: pltpu.VMEM((2, MOE_EXPERTS_PER_STEP, rows, cols), BF16)
    return pl.pallas_call(
        _moe_kernel,
        out_shape=jax.ShapeDtypeStruct((N_TOK * SLAB, LANES), U32),
        grid_spec=pltpu.PrefetchScalarGridSpec(
            num_scalar_prefetch=3,
            grid=(MOE_BLOCKS,),
            in_specs=[smem_spec, smem_spec,
                      pl.BlockSpec((MOE_BLOCK * SLAB, LANES), lambda b, *_: (b, 0),
                                   pipeline_mode=pl.Buffered(1)),
                      hbm_spec, hbm_spec, hbm_spec],
            out_specs=pl.BlockSpec((MOE_BLOCK * SLAB, LANES), lambda b, *_: (b, 0)),
            scratch_shapes=[pltpu.VMEM((STAGE_ROWS * SLAB, LANES), U32),
                            weight_buf(D_MODEL, EXPERT_HIDDEN), weight_buf(D_MODEL, EXPERT_HIDDEN),
                            weight_buf(EXPERT_HIDDEN, D_MODEL),
                            pltpu.SemaphoreType.DMA((3, 2, MOE_EXPERTS_PER_STEP))]),
        compiler_params=pltpu.CompilerParams(
            dimension_semantics=("arbitrary",), vmem_limit_bytes=VMEM_LIMIT),
        name="moe",
    )(cnt, off, order, dst, w, h2s, wg, wu, wd)


def _fin_kernel(x1s_ref, r_ref, gt2_ref, fg_ref, o_ref):
    halves = [_unpack_words(w) for w in _load_slabs(r_ref, 0, x1s_ref.shape[0])]
    routed = jnp.concatenate([lo for lo, _ in halves] + [hi for _, hi in halves], axis=-1)
    x2 = x1s_ref[...] + _per_batch(routed, gt2_ref[...], lambda r, m: r * m)
    ms = jnp.mean(x2 * x2, axis=-1, keepdims=True)
    o_ref[...] = _batch_major(x2 * lax.rsqrt(ms + EPS) * fg_ref[...])


def _fin(x1s, routed, gt2, fg):
    ts = FIN_STEPS * BATCH
    return pl.pallas_call(
        _fin_kernel,
        out_shape=jax.ShapeDtypeStruct((BATCH, SEQ, D_MODEL), F32),
        grid=(SEQ // FIN_STEPS,),
        in_specs=[pl.BlockSpec((ts, D_MODEL), lambda i: (i, 0)),
                  pl.BlockSpec((ts * SLAB, LANES), lambda i: (i, 0)),
                  _const_spec((1, BATCH, D_MODEL)),
                  _const_spec((1, D_MODEL))],
        out_specs=pl.BlockSpec((BATCH, FIN_STEPS, D_MODEL), lambda i: (0, i, 0)),
        compiler_params=pltpu.CompilerParams(
            dimension_semantics=("parallel",), vmem_limit_bytes=VMEM_LIMIT),
        name="fin",
    )(x1s, routed, gt2, fg)


def kernel(x, c, w_ada, b_ada, norm1_g, w_in, s5_lambda_re, s5_lambda_im, s5_log_dt, s5_b_re, s5_b_im, s5_c_re, s5_c_im, s5_d, s5_glu_wv, s5_glu_wg, conv_dw_w, conv_dw_b, conv_ln_g, conv_ln_b, conv_pw_w, w_out, norm2_g, router_w, router_bias, exp_w_gate, exp_w_up, exp_w_down, shared_w_gate, shared_w_up, shared_w_down, final_norm_g):
    l = 0
    row = lambda a: a.reshape(1, -1)

    mod = _ada(c, w_ada[l], b_ada[l])
    mods = [m.reshape(1, BATCH, D_MODEL) for m in jnp.split(mod, N_MOD, axis=-1)]

    abar_r, abar_i, bbar_r, bbar_i = _s5prep(s5_lambda_re[l], s5_lambda_im[l], s5_log_dt[l],
                                             s5_b_re[l], s5_b_im[l])
    per_group = lambda a: a.reshape(S5_GROUPS, S5_GROUP, S5_STATE)
    a_r = per_group(abar_r)[:, 0, :].reshape(1, N_STATE)
    a_i = per_group(abar_i)[:, 0, :].reshape(1, N_STATE)
    bmat = jnp.concatenate([_block_diag(per_group(bbar_r)), _block_diag(per_group(bbar_i))],
                           axis=1).astype(BF16)
    c_r = _block_diag(jnp.transpose(s5_c_re[l], (0, 2, 1))).astype(BF16)
    c_i = _block_diag(jnp.transpose(s5_c_im[l], (0, 2, 1))).astype(BF16)

    o1 = S5_WIDTH
    o2 = o1 + CONV_WIDTH
    o3 = o2 + CONV_WIDTH
    o4 = o3 + D_MODEL
    w = w_in[l].astype(BF16)
    sgu = jnp.concatenate([shared_w_gate[l], shared_w_up[l]], axis=1).astype(BF16)
    consts = (row(norm1_g[l]), row(norm2_g[l]),
              w[:, :o1], w[:, o1:o2], w[:, o2:o3], w[:, o3:o4], w[:, o4:],
              *_block_diag_tiles(bmat, c_r, c_i), a_r, a_i, row(s5_d[l]),
              s5_glu_wv[l].astype(BF16), s5_glu_wg[l].astype(BF16),
              conv_dw_w[l], row(conv_dw_b[l]), row(conv_ln_g[l]), row(conv_ln_b[l]),
              conv_pw_w[l].astype(BF16), w_out[l].astype(BF16),
              router_w[l].T.astype(BF16), router_bias[l].reshape(N_EXPERTS, 1),
              sgu, shared_w_down[l].astype(BF16))
    x1s, h2s, kmap, wc, cnt, wg, wu, wd = _tok(x, mods, consts,
                                               (exp_w_gate[l], exp_w_up[l], exp_w_down[l]))

    off = jnp.cumsum(cnt, axis=1) - cnt
    dst, w_k = _slots(kmap, wc, off)
    as_ints = lambda a: a.astype(I32).reshape(-1)
    per_block = lambda a: a.T.reshape(MOE_BLOCKS, 1, MOE_BLOCK * TOP_K)

    order = jnp.argsort(cnt[:, :, 0], axis=1)
    routed = _moe(as_ints(cnt), as_ints(off), as_ints(order), per_block(dst), per_block(w_k), h2s,
                  wg, wu, wd)
    return _fin(x1s, routed, mods[5], row(final_norm_g))
```

```python
import jax
import jax.numpy as jnp
from jax import lax
from jax.experimental import pallas as pl
from jax.experimental.pallas import tpu as pltpu

D_MODEL = 1024
BATCH = 8
SEQ = 4096
N_TOK = BATCH * SEQ
S5_WIDTH = 512
S5_GROUP = 16
S5_GROUPS = 32
S5_STATE = 64
N_STATE = S5_GROUPS * S5_STATE
CONV_WIDTH = 512
CONV_KERNEL = 31
N_EXPERTS = 64
TOP_K = 8
EXPERT_HIDDEN = 256
SHARED_HIDDEN = 256
ROUTED_SCALE = 2.5
N_MOD = 6
EPS = 1e-6

LANES = 128
SUBLANES = 8
VMEM_LIMIT = 56 * 1024 * 1024

TILE_STEPS = 64
TILE_ROWS = TILE_STEPS * BATCH
FIN_STEPS = 128
CONV_HALO = (CONV_KERNEL - 1) * BATCH
CONV_CHUNK = 64
SCAN_LANES = 512
MOE_BLOCK = 2048
MOE_BLOCKS = N_TOK // MOE_BLOCK
MOE_CHUNK = 512
MOE_GRAIN = 64
MOE_EXPERTS_PER_STEP = 2
MOE_EXPERT_STEPS = N_EXPERTS // MOE_EXPERTS_PER_STEP
MOE_UNROLL = 8
SLAB = 4
STAGE_ROWS = TOP_K * MOE_BLOCK + MOE_CHUNK

F32 = jnp.float32
BF16 = jnp.bfloat16
U32 = jnp.uint32
I32 = jnp.int32


_X_SPEC = pl.BlockSpec((BATCH, TILE_STEPS, D_MODEL), lambda i: (0, i, 0))


def _time_major(a):
    return jnp.swapaxes(a, 0, 1).reshape(a.shape[0] * a.shape[1], a.shape[2])


def _batch_major(a):
    return jnp.swapaxes(a.reshape(a.shape[0] // BATCH, BATCH, a.shape[1]), 0, 1)


def _per_batch(rows, mod, op):
    r3 = rows.reshape(rows.shape[0] // BATCH, BATCH, rows.shape[1])
    return op(r3, mod).reshape(rows.shape)


def _pack_rows(x):
    half = D_MODEL // 2
    return pltpu.pack_elementwise([x[:, :half], x[:, half:]], packed_dtype=BF16)


def _unpack_words(w):
    lo = pltpu.unpack_elementwise(w, index=0, packed_dtype=BF16, unpacked_dtype=F32)
    hi = pltpu.unpack_elementwise(w, index=1, packed_dtype=BF16, unpacked_dtype=F32)
    return lo, hi


def _store_slabs(ref, base, words):
    rows = words.shape[0]
    for s in range(SLAB):
        ref[pl.ds(base + s, rows, stride=SLAB), :] = words[:, s * LANES:(s + 1) * LANES]


def _load_slabs(ref, base, rows):
    return [ref[pl.ds(base + s, rows, stride=SLAB), :] for s in range(SLAB)]


def _sigmoid(x):
    return jax.nn.sigmoid(x)


def _const_spec(shape):
    zeros = (0,) * len(shape)
    return pl.BlockSpec(shape, lambda *_: zeros, pipeline_mode=pl.Buffered(1))


def _ada_kernel(c_ref, w_ref, b_ref, o_ref):
    c = c_ref[...]
    c_act = c * _sigmoid(c)
    o_ref[...] = jnp.dot(c_act, w_ref[...], preferred_element_type=F32,
                         precision=lax.Precision.HIGHEST) + b_ref[...]


def _ada(c, w_ada, b_ada):
    n_out = N_MOD * D_MODEL
    blk = 1536
    return pl.pallas_call(
        _ada_kernel,
        out_shape=jax.ShapeDtypeStruct((BATCH, n_out), F32),
        grid=(n_out // blk,),
        in_specs=[pl.BlockSpec((BATCH, D_MODEL), lambda j: (0, 0)),
                  pl.BlockSpec((D_MODEL, blk), lambda j: (0, j)),
                  pl.BlockSpec((1, blk), lambda j: (0, j))],
        out_specs=pl.BlockSpec((BATCH, blk), lambda j: (0, j)),
        compiler_params=pltpu.CompilerParams(vmem_limit_bytes=VMEM_LIMIT),
        name="ada",
    )(c, w_ada, b_ada.reshape(1, n_out))


def _s5prep_kernel(lr_ref, li_ref, ldt_ref, br_ref, bi_ref, ar_ref, ai_ref, bbr_ref, bbi_ref):
    lr = lr_ref[...]
    li = li_ref[...]
    dt = jnp.exp(ldt_ref[...])
    mag = jnp.exp(lr * dt)
    abar_r = mag * jnp.cos(li * dt)
    abar_i = mag * jnp.sin(li * dt)
    den = lr * lr + li * li
    nr = abar_r - 1.0
    ni = abar_i
    k_r = (nr * lr + ni * li) / den
    k_i = (ni * lr - nr * li) / den
    br = br_ref[...]
    bi = bi_ref[...]
    ar_ref[...] = abar_r
    ai_ref[...] = abar_i
    bbr_ref[...] = k_r * br - k_i * bi
    bbi_ref[...] = k_r * bi + k_i * br


def _s5prep(lam_re, lam_im, log_dt, b_re, b_im):
    rows = S5_GROUPS * S5_GROUP
    rep = lambda a: jnp.broadcast_to(a[:, None, :], (S5_GROUPS, S5_GROUP, S5_STATE)).reshape(rows, S5_STATE)
    lr = rep(lam_re)
    li = rep(lam_im)
    ldt = rep(jnp.broadcast_to(log_dt[:, None], (S5_GROUPS, S5_STATE)))
    br = jnp.transpose(b_re, (0, 2, 1)).reshape(rows, S5_STATE)
    bi = jnp.transpose(b_im, (0, 2, 1)).reshape(rows, S5_STATE)
    shp = jax.ShapeDtypeStruct((rows, S5_STATE), F32)
    return pl.pallas_call(_s5prep_kernel, out_shape=(shp, shp, shp, shp), name="s5prep")(lr, li, ldt, br, bi)


def _block_diag(blocks):
    g, r, c = blocks.shape
    eye = jnp.eye(g, dtype=blocks.dtype)
    return jnp.einsum("grc,gk->grkc", blocks, eye).reshape(g * r, g * c)


def _gelu_tanh(x):
    sqrt_2_over_pi = 0.7978845608028654
    cdf = 0.5 * (1.0 + jnp.tanh(sqrt_2_over_pi * (x + 0.044715 * (x * x * x))))
    return x * cdf


def _rms_mod(x, g, scale, shift):
    ms = jnp.mean(x * x, axis=-1, keepdims=True)
    h = x * lax.rsqrt(ms + EPS) * g
    return _per_batch(h, (scale, shift), lambda r, m: r * (1.0 + m[0]) + m[1])


def _tok_kernel(x_ref, sh1_ref, sc1_ref, gt1_ref, sh2_ref, sc2_ref, gt2_ref, g1_ref, g2_ref,
                wu_ref, wcv_ref, wcg_ref, wa_ref, wb_ref,
                bsp_ref, csr_ref, csi_ref, ar_ref, ai_ref, d_ref, wv_ref, wg_ref,
                dww_ref, dwb_ref, lng_ref, lnb_ref, pw_ref, wout_ref,
                rw_ref, rb_ref, sgu_ref, sd_ref, ew_gate_ref, ew_up_ref, ew_down_ref,
                x1s_ref, h2_ref, kmap_ref, wc_ref, cnt_ref, eb_gate_ref, eb_up_ref, eb_down_ref,
                bu_ref, sb_ref, sr_ref, si_ref, cbuf_ref, cv_ref):
    step = pl.program_id(0)

    @pl.when(step == 0)
    def _():
        sr_ref[...] = jnp.zeros_like(sr_ref)
        si_ref[...] = jnp.zeros_like(si_ref)
        cbuf_ref[0:CONV_HALO, :] = jnp.zeros((CONV_HALO, CONV_WIDTH), F32)

    x = _time_major(x_ref[...])
    h = _rms_mod(x, g1_ref[...], sc1_ref[...], sh1_ref[...]).astype(BF16)
    u = jnp.dot(h, wu_ref[...], preferred_element_type=F32)
    u_b = u.astype(BF16)
    zc = (jnp.dot(h, wcv_ref[...], preferred_element_type=F32)
          * _sigmoid(jnp.dot(h, wcg_ref[...], preferred_element_type=F32)))

    for j in range(N_STATE // LANES):
        kb = (j * LANES // S5_STATE * S5_GROUP) // LANES
        res = jnp.dot(u_b[:, kb * LANES:(kb + 1) * LANES], bsp_ref[j], preferred_element_type=F32)
        bu_ref[:, j * LANES:(j + 1) * LANES] = res[:, :LANES]
        bu_ref[:, N_STATE + j * LANES:N_STATE + (j + 1) * LANES] = res[:, LANES:]

    gate_a = _sigmoid(jnp.dot(h, wa_ref[...], preferred_element_type=F32))
    gate_b = _sigmoid(jnp.dot(h, wb_ref[...], preferred_element_type=F32))

    for lg in range(N_STATE // SCAN_LANES):
        lo = lg * SCAN_LANES
        re = slice(lo, lo + SCAN_LANES)
        im = slice(N_STATE + lo, N_STATE + lo + SCAN_LANES)
        a_r = jnp.broadcast_to(ar_ref[:, re], (SUBLANES, SCAN_LANES))
        a_i = jnp.broadcast_to(ai_ref[:, re], (SUBLANES, SCAN_LANES))
        s_r = sr_ref[:, re]
        s_i = si_ref[:, re]
        for t in range(0, TILE_STEPS, 2):
            pair_r, pair_i = [], []
            for tt in (t, t + 1):
                rows = slice(tt * SUBLANES, (tt + 1) * SUBLANES)
                s_r, s_i = (a_r * s_r - a_i * s_i + bu_ref[rows, re],
                            a_r * s_i + a_i * s_r + bu_ref[rows, im])
                pair_r.append(s_r)
                pair_i.append(s_i)
            rows2 = slice(t * SUBLANES, (t + 2) * SUBLANES)
            sb_ref[rows2, re] = jnp.concatenate(pair_r, axis=0).astype(BF16)
            sb_ref[rows2, im] = jnp.concatenate(pair_i, axis=0).astype(BF16)
        sr_ref[:, re] = s_r
        si_ref[:, re] = s_i

    half_s = N_STATE // 2
    ys = []
    for hf in range(2):
        s_re = sb_ref[:, hf * half_s:(hf + 1) * half_s]
        s_im = sb_ref[:, N_STATE + hf * half_s:N_STATE + (hf + 1) * half_s]
        ys.append(jnp.dot(s_re, csr_ref[hf], preferred_element_type=F32)
                  - jnp.dot(s_im, csi_ref[hf], preferred_element_type=F32))
    y = jnp.concatenate(ys, axis=-1) + d_ref[...] * u
    z = _gelu_tanh(y).astype(BF16)
    y_a = (jnp.dot(z, wv_ref[...], preferred_element_type=F32)
           * _sigmoid(jnp.dot(z, wg_ref[...], preferred_element_type=F32)))

    cbuf_ref[CONV_HALO:CONV_HALO + TILE_ROWS, :] = zc
    for ci in range(TILE_ROWS // CONV_CHUNK):
        r0 = ci * CONV_CHUNK
        for lt in range(CONV_WIDTH // LANES):
            ls = slice(lt * LANES, (lt + 1) * LANES)
            win = cbuf_ref[r0:r0 + CONV_CHUNK + CONV_HALO, ls]
            acc = jnp.broadcast_to(dwb_ref[:, ls], (CONV_CHUNK, LANES))
            for k in range(CONV_KERNEL):
                acc = acc + dww_ref[k:k + 1, ls] * win[k * BATCH:k * BATCH + CONV_CHUNK]
            cv_ref[r0:r0 + CONV_CHUNK, ls] = acc
    cbuf_ref[0:CONV_HALO, :] = cbuf_ref[TILE_ROWS:TILE_ROWS + CONV_HALO, :]

    cv = cv_ref[...]
    mu = jnp.mean(cv, axis=-1, keepdims=True)
    var = jnp.mean(jnp.square(cv - mu), axis=-1, keepdims=True)
    ln = (cv - mu) * lax.rsqrt(var + EPS) * lng_ref[...] + lnb_ref[...]
    zb = (ln * _sigmoid(ln)).astype(BF16)
    y_b = jnp.dot(zb, pw_ref[...], preferred_element_type=F32)

    m = gate_a * y_a + gate_b * y_b
    mixed = jnp.dot(m.astype(BF16), wout_ref[...], preferred_element_type=F32)

    x1 = x + _per_batch(mixed, gt1_ref[...], lambda r, m: r * m)
    hb = _rms_mod(x1, g2_ref[...], sc2_ref[...], sh2_ref[...]).astype(BF16)
    _store_slabs(h2_ref, 0, _pack_rows(hb.astype(F32)))

    scores = _sigmoid(lax.dot_general(rw_ref[...], hb, (((1,), (1,)), ((), ())),
                                      preferred_element_type=F32))
    expert = lax.broadcasted_iota(I32, scores.shape, 0)
    neg_inf = jnp.float32(-jnp.inf)
    work = scores + rb_ref[...]
    kmap = jnp.zeros(scores.shape, I32)
    for k in range(TOP_K):
        top = jnp.max(work, axis=0, keepdims=True)
        idx = jnp.min(jnp.where(work == top, expert, N_EXPERTS), axis=0, keepdims=True)
        hit = expert == idx
        kmap = jnp.where(hit, k + 1, kmap)
        work = jnp.where(hit, neg_inf, work)
    sel = kmap > 0
    w_sel = jnp.where(sel, scores, 0.0)
    kmap_ref[...] = kmap
    wc_ref[...] = w_sel / jnp.sum(w_sel, axis=0, keepdims=True) * ROUTED_SCALE

    @pl.when(step % (MOE_BLOCK // TILE_ROWS) == 0)
    def _():
        cnt_ref[...] = jnp.zeros_like(cnt_ref)

    cnt_ref[...] += jnp.sum(sel.astype(F32), axis=1, keepdims=True)

    gu = jnp.dot(hb, sgu_ref[...], preferred_element_type=F32)
    g = gu[:, :SHARED_HIDDEN]
    act = (g * _sigmoid(g)) * gu[:, SHARED_HIDDEN:]
    shared = jnp.dot(act.astype(BF16), sd_ref[...], preferred_element_type=F32)
    x1s_ref[...] = x1 + _per_batch(shared, gt2_ref[...], lambda r, m: r * m)

    eb_gate_ref[...] = ew_gate_ref[...].astype(BF16)
    eb_up_ref[...] = ew_up_ref[...].astype(BF16)
    eb_down_ref[...] = ew_down_ref[...].astype(BF16)


def _block_diag_tiles(bmat, c_r, c_i):
    tiles = []
    for j in range(N_STATE // LANES):
        kb = (j * LANES // S5_STATE * S5_GROUP) // LANES
        k_rows = slice(kb * LANES, (kb + 1) * LANES)
        tiles.append(jnp.concatenate([bmat[k_rows, j * LANES:(j + 1) * LANES],
                                      bmat[k_rows, N_STATE + j * LANES:N_STATE + (j + 1) * LANES]], axis=1))
    half_c, half_s = S5_WIDTH // 2, N_STATE // 2
    halves = lambda m: jnp.stack([m[hf * half_s:(hf + 1) * half_s, hf * half_c:(hf + 1) * half_c]
                                  for hf in range(2)])
    return jnp.stack(tiles), halves(c_r), halves(c_i)


def _tok(x, mods, consts, expert_weights):
    ts = TILE_ROWS
    steps = SEQ // TILE_STEPS
    tiles_per_block = MOE_BLOCK // ts
    row_spec = lambda rows, c: pl.BlockSpec((rows, c), lambda i: (i, 0))
    route_spec = pl.BlockSpec((N_EXPERTS, ts), lambda i: (0, i))
    per_step = N_EXPERTS // steps
    assert per_step * steps == N_EXPERTS
    cast_specs = [pl.BlockSpec((per_step,) + a.shape[1:], lambda i: (i, 0, 0)) for a in expert_weights]
    return pl.pallas_call(
        _tok_kernel,
        out_shape=(jax.ShapeDtypeStruct((N_TOK, D_MODEL), F32),
                   jax.ShapeDtypeStruct((N_TOK * SLAB, LANES), U32),
                   jax.ShapeDtypeStruct((N_EXPERTS, N_TOK), I32),
                   jax.ShapeDtypeStruct((N_EXPERTS, N_TOK), F32),
                   jax.ShapeDtypeStruct((MOE_BLOCKS, N_EXPERTS, 1), F32))
                  + tuple(jax.ShapeDtypeStruct(a.shape, BF16) for a in expert_weights),
        grid=(steps,),
        in_specs=[_X_SPEC] + [_const_spec(a.shape) for a in tuple(mods) + tuple(consts)] + cast_specs,
        out_specs=(row_spec(ts, D_MODEL), row_spec(ts * SLAB, LANES), route_spec, route_spec,
                   pl.BlockSpec((None, N_EXPERTS, 1), lambda i: (i // tiles_per_block, 0, 0)))
                  + tuple(cast_specs),
        scratch_shapes=[pltpu.VMEM((ts, 2 * N_STATE), F32),
                        pltpu.VMEM((ts, 2 * N_STATE), BF16),
                        pltpu.VMEM((SUBLANES, N_STATE), F32),
                        pltpu.VMEM((SUBLANES, N_STATE), F32),
                        pltpu.VMEM((CONV_HALO + ts, CONV_WIDTH), F32),
                        pltpu.VMEM((ts, CONV_WIDTH), F32)],
        compiler_params=pltpu.CompilerParams(
            dimension_semantics=("arbitrary",), vmem_limit_bytes=VMEM_LIMIT),
        name="tok",
    )(x, *mods, *consts, *expert_weights)


def _slots_kernel(kmap_ref, wc_ref, off_ref, tri_ref, dst_ref, wk_ref, seen_ref):
    @pl.when(pl.program_id(0) % (MOE_BLOCK // TILE_ROWS) == 0)
    def _():
        seen_ref[...] = jnp.zeros_like(seen_ref)

    kmap = kmap_ref[...]
    wc = wc_ref[...]
    sel = (kmap > 0).astype(F32)
    seen = seen_ref[...]
    slot = jnp.dot(sel.astype(BF16), tri_ref[...], preferred_element_type=F32) + (seen + off_ref[...])
    seen_ref[...] = seen + jnp.sum(sel, axis=1, keepdims=True)
    dst, w_k = [], []
    for k in range(TOP_K):
        hit = kmap == k + 1
        dst.append(jnp.sum(jnp.where(hit, slot, 0.0), axis=0, keepdims=True))
        w_k.append(jnp.sum(jnp.where(hit, wc, 0.0), axis=0, keepdims=True))
    dst_ref[...] = jnp.concatenate(dst, axis=0).astype(I32) * SLAB
    wk_ref[...] = jnp.concatenate(w_k, axis=0)


def _slots(kmap, wc, off):
    ts = TILE_ROWS
    tiles_per_block = MOE_BLOCK // ts
    route_spec = pl.BlockSpec((N_EXPERTS, ts), lambda i: (0, i))
    slot_spec = pl.BlockSpec((TOP_K, ts), lambda i: (0, i))
    tri = (lax.broadcasted_iota(I32, (ts, ts), 0) < lax.broadcasted_iota(I32, (ts, ts), 1)).astype(BF16)
    return pl.pallas_call(
        _slots_kernel,
        out_shape=(jax.ShapeDtypeStruct((TOP_K, N_TOK), I32), jax.ShapeDtypeStruct((TOP_K, N_TOK), F32)),
        grid=(N_TOK // ts,),
        in_specs=[route_spec, route_spec,
                  pl.BlockSpec((None, N_EXPERTS, 1), lambda i: (i // tiles_per_block, 0, 0)),
                  _const_spec(tri.shape)],
        out_specs=(slot_spec, slot_spec),
        scratch_shapes=[pltpu.VMEM((N_EXPERTS, 1), F32)],
        compiler_params=pltpu.CompilerParams(
            dimension_semantics=("arbitrary",), vmem_limit_bytes=VMEM_LIMIT),
        name="slots",
    )(kmap, wc, off, tri)


def _moe_kernel(cnt_ref, off_ref, order_ref, dst_ref, w_ref, h_ref, wg_hbm, wu_hbm, wd_hbm, o_ref,
                stage_ref, wg_buf, wu_buf, wd_buf, sem):
    blk = pl.program_id(0)

    def step_experts(step):
        a = order_ref[blk * N_EXPERTS + 2 * step]
        b = order_ref[blk * N_EXPERTS + 2 * step + 1]
        return jnp.minimum(a, b), jnp.maximum(a, b)

    def weight_copies(step, slot):
        pairs = ((wg_hbm, wg_buf), (wu_hbm, wu_buf), (wd_hbm, wd_buf))
        return [pltpu.make_async_copy(hbm.at[pl.ds(e, 1)], buf.at[slot, pl.ds(j, 1)], sem.at[i, slot, j])
                for i, (hbm, buf) in enumerate(pairs) for j, e in enumerate(step_experts(step))]

    def fetch(step, slot):
        for copy in weight_copies(step, slot):
            copy.start()

    def arrived(step, slot):
        for copy in weight_copies(step, slot):
            copy.wait()

    fetch(0, 0)

    @pl.when(blk == 0)
    def _():
        stage_ref[TOP_K * MOE_BLOCK * SLAB:, :] = jnp.zeros((MOE_CHUNK * SLAB, LANES), U32)

    def slab_at(ref, sublane_row):
        return ref.at[pl.ds(pl.multiple_of(sublane_row, SLAB), SLAB), :]

    def selections(ref, i):
        first = pl.multiple_of(i * MOE_UNROLL, MOE_UNROLL)
        return [ref.at[0, pl.ds(k * MOE_BLOCK + first, MOE_UNROLL)] for k in range(TOP_K)]

    def dispatch(i, carry):
        dst = selections(dst_ref, i)
        for j in range(MOE_UNROLL):
            slab = slab_at(h_ref, (i * MOE_UNROLL + j) * SLAB)[...]
            for k in range(TOP_K):
                slab_at(stage_ref, dst[k][j])[...] = slab
        return carry

    lax.fori_loop(0, MOE_BLOCK // MOE_UNROLL, dispatch, 0)

    def run_chunks(slot, firsts, live_rows, rows):
        words, outs = [], []
        for j, first in enumerate(firsts):
            words.append(_load_slabs(stage_ref, first * SLAB, rows))
        for j in range(len(firsts)):
            halves = [_unpack_words(w) for w in words[j]]
            xs = jnp.concatenate([lo for lo, _ in halves] + [hi for _, hi in halves], axis=-1)
            xs = xs.astype(BF16)
            g = jnp.dot(xs, wg_buf[slot, j], preferred_element_type=F32)
            act = (g * _sigmoid(g)) * jnp.dot(xs, wu_buf[slot, j], preferred_element_type=F32)
            outs.append(_pack_rows(jnp.dot(act.astype(BF16), wd_buf[slot, j], preferred_element_type=F32)))
        row = lax.broadcasted_iota(I32, (rows, LANES), 0)
        for j, first in enumerate(firsts):
            live = row < live_rows[j]
            for s in range(SLAB):
                merged = jnp.where(live, outs[j][:, s * LANES:(s + 1) * LANES], words[j][s])
                stage_ref[pl.ds(first * SLAB + s, rows, stride=SLAB), :] = merged

    def expert_step(step, carry):
        slot = step % 2

        @pl.when(step + 1 < MOE_EXPERT_STEPS)
        def _():
            fetch(step + 1, 1 - slot)

        arrived(step, slot)
        experts = step_experts(step)
        firsts = [off_ref[blk * N_EXPERTS + e] for e in experts]
        counts = [cnt_ref[blk * N_EXPERTS + e] for e in experts]
        rests = []
        for j in range(MOE_EXPERTS_PER_STEP):
            n_whole = jnp.maximum(counts[j] - 1, 0) // MOE_CHUNK

            def whole(c, carry, j=j):
                run_chunks_single(slot, j, firsts[j] + c * MOE_CHUNK)
                return carry

            lax.fori_loop(0, n_whole, whole, 0)
            firsts[j] = firsts[j] + n_whole * MOE_CHUNK
            rests.append(counts[j] - n_whole * MOE_CHUNK)
        size_class = (jnp.maximum(rests[0], rests[1]) + MOE_GRAIN - 1) // MOE_GRAIN
        for cls in range(1, MOE_CHUNK // MOE_GRAIN + 1):
            @pl.when(size_class == cls)
            def _(cls=cls):
                run_chunks(slot, firsts, rests, cls * MOE_GRAIN)
        return carry

    def run_chunks_single(slot, j, first):
        words = _load_slabs(stage_ref, first * SLAB, MOE_CHUNK)
        halves = [_unpack_words(w) for w in words]
        xs = jnp.concatenate([lo for lo, _ in halves] + [hi for _, hi in halves], axis=-1).astype(BF16)
        g = jnp.dot(xs, wg_buf[slot, j], preferred_element_type=F32)
        act = (g * _sigmoid(g)) * jnp.dot(xs, wu_buf[slot, j], preferred_element_type=F32)
        _store_slabs(stage_ref, first * SLAB,
                     _pack_rows(jnp.dot(act.astype(BF16), wd_buf[slot, j], preferred_element_type=F32)))

    lax.fori_loop(0, MOE_EXPERT_STEPS, expert_step, 0)

    def combine(i, carry):
        dst = selections(dst_ref, i)
        wts = selections(w_ref, i)
        for j in range(MOE_UNROLL):
            t = i * MOE_UNROLL + j
            terms = []
            for k in range(TOP_K):
                w = wts[k][j]
                lo, hi = _unpack_words(slab_at(stage_ref, dst[k][j])[...])
                terms.append((lo * w, hi * w))
            while len(terms) > 1:
                terms = [(a[0] + b[0], a[1] + b[1]) for a, b in zip(terms[0::2], terms[1::2])]
            slab_at(o_ref, t * SLAB)[...] = pltpu.pack_elementwise(list(terms[0]), packed_dtype=BF16)
        return carry

    lax.fori_loop(0, MOE_BLOCK // MOE_UNROLL, combine, 0)


def _moe(cnt, off, order, dst, w, h2s, wg, wu, wd):
    smem_spec = pl.BlockSpec((None, 1, TOP_K * MOE_BLOCK), lambda b, *_: (b, 0, 0), memory_space=pltpu.SMEM)
    hbm_spec = pl.BlockSpec(memory_space=pl.ANY)
    weight_buf = lambda rows, cols: pltpu.VMEM((2, MOE_EXPERTS_PER_STEP, rows, cols), BF16)
    return pl.pallas_call(
        _moe_kernel,
        out_shape=jax.ShapeDtypeStruct((N_TOK * SLAB, LANES), U32),
        grid_spec=pltpu.PrefetchScalarGridSpec(
            num_scalar_prefetch=3,
            grid=(MOE_BLOCKS,),
            in_specs=[smem_spec, smem_spec,
                      pl.BlockSpec((MOE_BLOCK * SLAB, LANES), lambda b, *_: (b, 0),
                                   pipeline_mode=pl.Buffered(1)),
                      hbm_spec, hbm_spec, hbm_spec],
            out_specs=pl.BlockSpec((MOE_BLOCK * SLAB, LANES), lambda b, *_: (b, 0)),
            scratch_shapes=[pltpu.VMEM((STAGE_ROWS * SLAB, LANES), U32),
                            weight_buf(D_MODEL, EXPERT_HIDDEN), weight_buf(D_MODEL, EXPERT_HIDDEN),
                            weight_buf(EXPERT_HIDDEN, D_MODEL),
                            pltpu.SemaphoreType.DMA((3, 2, MOE_EXPERTS_PER_STEP))]),
        compiler_params=pltpu.CompilerParams(
            dimension_semantics=("arbitrary",), vmem_limit_bytes=VMEM_LIMIT),
        name="moe",
    )(cnt, off, order, dst, w, h2s, wg, wu, wd)


def _fin_kernel(x1s_ref, r_ref, gt2_ref, fg_ref, o_ref):
    halves = [_unpack_words(w) for w in _load_slabs(r_ref, 0, x1s_ref.shape[0])]
    routed = jnp.concatenate([lo for lo, _ in halves] + [hi for _, hi in halves], axis=-1)
    x2 = x1s_ref[...] + _per_batch(routed, gt2_ref[...], lambda r, m: r * m)
    ms = jnp.mean(x2 * x2, axis=-1, keepdims=True)
    o_ref[...] = _batch_major(x2 * lax.rsqrt(ms + EPS) * fg_ref[...])


def _fin(x1s, routed, gt2, fg):
    ts = FIN_STEPS * BATCH
    return pl.pallas_call(
        _fin_kernel,
        out_shape=jax.ShapeDtypeStruct((BATCH, SEQ, D_MODEL), F32),
        grid=(SEQ // FIN_STEPS,),
        in_specs=[pl.BlockSpec((ts, D_MODEL), lambda i: (i, 0)),
                  pl.BlockSpec((ts * SLAB, LANES), lambda i: (i, 0)),
                  _const_spec((1, BATCH, D_MODEL)),
                  _const_spec((1, D_MODEL))],
        out_specs=pl.BlockSpec((BATCH, FIN_STEPS, D_MODEL), lambda i: (0, i, 0)),
        compiler_params=pltpu.CompilerParams(
            dimension_semantics=("parallel",), vmem_limit_bytes=VMEM_LIMIT),
        name="fin",
    )(x1s, routed, gt2, fg)


def kernel(x, c, w_ada, b_ada, norm1_g, w_in, s5_lambda_re, s5_lambda_im, s5_log_dt, s5_b_re, s5_b_im, s5_c_re, s5_c_im, s5_d, s5_glu_wv, s5_glu_wg, conv_dw_w, conv_dw_b, conv_ln_g, conv_ln_b, conv_pw_w, w_out, norm2_g, router_w, router_bias, exp_w_gate, exp_w_up, exp_w_down, shared_w_gate, shared_w_up, shared_w_down, final_norm_g):
    l = 0
    row = lambda a: a.reshape(1, -1)

    mod = _ada(c, w_ada[l], b_ada[l])
    mods = [m.reshape(1, BATCH, D_MODEL) for m in jnp.split(mod, N_MOD, axis=-1)]

    abar_r, abar_i, bbar_r, bbar_i = _s5prep(s5_lambda_re[l], s5_lambda_im[l], s5_log_dt[l],
                                             s5_b_re[l], s5_b_im[l])
    per_group = lambda a: a.reshape(S5_GROUPS, S5_GROUP, S5_STATE)
    a_r = per_group(abar_r)[:, 0, :].reshape(1, N_STATE)
    a_i = per_group(abar_i)[:, 0, :].reshape(1, N_STATE)
    bmat = jnp.concatenate([_block_diag(per_group(bbar_r)), _block_diag(per_group(bbar_i))],
                           axis=1).astype(BF16)
    c_r = _block_diag(jnp.transpose(s5_c_re[l], (0, 2, 1))).astype(BF16)
    c_i = _block_diag(jnp.transpose(s5_c_im[l], (0, 2, 1))).astype(BF16)

    o1 = S5_WIDTH
    o2 = o1 + CONV_WIDTH
    o3 = o2 + CONV_WIDTH
    o4 = o3 + D_MODEL
    w = w_in[l].astype(BF16)
    sgu = jnp.concatenate([shared_w_gate[l], shared_w_up[l]], axis=1).astype(BF16)
    consts = (row(norm1_g[l]), row(norm2_g[l]),
              w[:, :o1], w[:, o1:o2], w[:, o2:o3], w[:, o3:o4], w[:, o4:],
              *_block_diag_tiles(bmat, c_r, c_i), a_r, a_i, row(s5_d[l]),
              s5_glu_wv[l].astype(BF16), s5_glu_wg[l].astype(BF16),
              conv_dw_w[l], row(conv_dw_b[l]), row(conv_ln_g[l]), row(conv_ln_b[l]),
              conv_pw_w[l].astype(BF16), w_out[l].astype(BF16),
              router_w[l].T.astype(BF16), router_bias[l].reshape(N_EXPERTS, 1),
              sgu, shared_w_down[l].astype(BF16))
    x1s, h2s, kmap, wc, cnt, wg, wu, wd = _tok(x, mods, consts,
                                               (exp_w_gate[l], exp_w_up[l], exp_w_down[l]))

    off = jnp.cumsum(cnt, axis=1) - cnt
    dst, w_k = _slots(kmap, wc, off)
    as_ints = lambda a: a.astype(I32).reshape(-1)
    per_block = lambda a: jnp.swapaxes(a.reshape(TOP_K, MOE_BLOCKS, MOE_BLOCK), 0, 1).reshape(
        MOE_BLOCKS, 1, TOP_K * MOE_BLOCK)

    order = jnp.argsort(cnt[:, :, 0], axis=1)
    routed = _moe(as_ints(cnt), as_ints(off), as_ints(order), per_block(dst), per_block(w_k), h2s,
                  wg, wu, wd)
    return _fin(x1s, routed, mods[5], row(final_norm_g))
```

```python
import jax
import jax.numpy as jnp
from jax import lax
from jax.experimental import pallas as pl
from jax.experimental.pallas import tpu as pltpu

D_MODEL = 1024
BATCH = 8
SEQ = 4096
N_TOK = BATCH * SEQ
S5_WIDTH = 512
S5_GROUP = 16
S5_GROUPS = 32
S5_STATE = 64
N_STATE = S5_GROUPS * S5_STATE
CONV_WIDTH = 512
CONV_KERNEL = 31
N_EXPERTS = 64
TOP_K = 8
EXPERT_HIDDEN = 256
SHARED_HIDDEN = 256
ROUTED_SCALE = 2.5
N_MOD = 6
EPS = 1e-6

LANES = 128
SUBLANES = 8
VMEM_LIMIT = 56 * 1024 * 1024

TILE_STEPS = 64
TILE_ROWS = TILE_STEPS * BATCH
FIN_STEPS = 128
SLOT_ROWS = 1024
CONV_HALO = (CONV_KERNEL - 1) * BATCH
CONV_CHUNK = 64
SCAN_LANES = 512
MOE_BLOCK = 2048
MOE_BLOCKS = N_TOK // MOE_BLOCK
MOE_CHUNK = 512
MOE_GRAIN = 64
MOE_EXPERTS_PER_STEP = 2
MOE_EXPERT_STEPS = N_EXPERTS // MOE_EXPERTS_PER_STEP
MOE_UNROLL = 8
SLAB = 4
STAGE_ROWS = TOP_K * MOE_BLOCK + MOE_CHUNK

F32 = jnp.float32
BF16 = jnp.bfloat16
U32 = jnp.uint32
I32 = jnp.int32


_X_SPEC = pl.BlockSpec((BATCH, TILE_STEPS, D_MODEL), lambda i: (0, i, 0))


def _time_major(a):
    return jnp.swapaxes(a, 0, 1).reshape(a.shape[0] * a.shape[1], a.shape[2])


def _batch_major(a):
    return jnp.swapaxes(a.reshape(a.shape[0] // BATCH, BATCH, a.shape[1]), 0, 1)


def _per_batch(rows, mod, op):
    r3 = rows.reshape(rows.shape[0] // BATCH, BATCH, rows.shape[1])
    return op(r3, mod).reshape(rows.shape)


def _pack_rows(x):
    half = D_MODEL // 2
    return pltpu.pack_elementwise([x[:, :half], x[:, half:]], packed_dtype=BF16)


def _unpack_words(w):
    lo = pltpu.unpack_elementwise(w, index=0, packed_dtype=BF16, unpacked_dtype=F32)
    hi = pltpu.unpack_elementwise(w, index=1, packed_dtype=BF16, unpacked_dtype=F32)
    return lo, hi


def _store_slabs(ref, base, words):
    rows = words.shape[0]
    for s in range(SLAB):
        ref[pl.ds(base + s, rows, stride=SLAB), :] = words[:, s * LANES:(s + 1) * LANES]


def _load_slabs(ref, base, rows):
    return [ref[pl.ds(base + s, rows, stride=SLAB), :] for s in range(SLAB)]


def _sigmoid(x):
    return jax.nn.sigmoid(x)


def _const_spec(shape):
    zeros = (0,) * len(shape)
    return pl.BlockSpec(shape, lambda *_: zeros, pipeline_mode=pl.Buffered(1))


def _ada_kernel(c_ref, w_ref, b_ref, o_ref):
    c = c_ref[...]
    c_act = c * _sigmoid(c)
    o_ref[...] = jnp.dot(c_act, w_ref[...], preferred_element_type=F32,
                         precision=lax.Precision.HIGHEST) + b_ref[...]


def _ada(c, w_ada, b_ada):
    n_out = N_MOD * D_MODEL
    blk = 1536
    return pl.pallas_call(
        _ada_kernel,
        out_shape=jax.ShapeDtypeStruct((BATCH, n_out), F32),
        grid=(n_out // blk,),
        in_specs=[pl.BlockSpec((BATCH, D_MODEL), lambda j: (0, 0)),
                  pl.BlockSpec((D_MODEL, blk), lambda j: (0, j)),
                  pl.BlockSpec((1, blk), lambda j: (0, j))],
        out_specs=pl.BlockSpec((BATCH, blk), lambda j: (0, j)),
        compiler_params=pltpu.CompilerParams(vmem_limit_bytes=VMEM_LIMIT),
        name="ada",
    )(c, w_ada, b_ada.reshape(1, n_out))


def _s5prep_kernel(lr_ref, li_ref, ldt_ref, br_ref, bi_ref, ar_ref, ai_ref, bbr_ref, bbi_ref):
    lr = lr_ref[...]
    li = li_ref[...]
    dt = jnp.exp(ldt_ref[...])
    mag = jnp.exp(lr * dt)
    abar_r = mag * jnp.cos(li * dt)
    abar_i = mag * jnp.sin(li * dt)
    den = lr * lr + li * li
    nr = abar_r - 1.0
    ni = abar_i
    k_r = (nr * lr + ni * li) / den
    k_i = (ni * lr - nr * li) / den
    br = br_ref[...]
    bi = bi_ref[...]
    ar_ref[...] = abar_r
    ai_ref[...] = abar_i
    bbr_ref[...] = k_r * br - k_i * bi
    bbi_ref[...] = k_r * bi + k_i * br


def _s5prep(lam_re, lam_im, log_dt, b_re, b_im):
    rows = S5_GROUPS * S5_GROUP
    rep = lambda a: jnp.broadcast_to(a[:, None, :], (S5_GROUPS, S5_GROUP, S5_STATE)).reshape(rows, S5_STATE)
    lr = rep(lam_re)
    li = rep(lam_im)
    ldt = rep(jnp.broadcast_to(log_dt[:, None], (S5_GROUPS, S5_STATE)))
    br = jnp.transpose(b_re, (0, 2, 1)).reshape(rows, S5_STATE)
    bi = jnp.transpose(b_im, (0, 2, 1)).reshape(rows, S5_STATE)
    shp = jax.ShapeDtypeStruct((rows, S5_STATE), F32)
    return pl.pallas_call(_s5prep_kernel, out_shape=(shp, shp, shp, shp), name="s5prep")(lr, li, ldt, br, bi)


def _gelu_tanh(x):
    sqrt_2_over_pi = 0.7978845608028654
    cdf = 0.5 * (1.0 + jnp.tanh(sqrt_2_over_pi * (x + 0.044715 * (x * x * x))))
    return x * cdf


def _rms_mod(x, g, scale, shift):
    ms = jnp.mean(x * x, axis=-1, keepdims=True)
    h = x * lax.rsqrt(ms + EPS) * g
    return _per_batch(h, (scale, shift), lambda r, m: r * (1.0 + m[0]) + m[1])


def _tok_kernel(x_ref, sh1_ref, sc1_ref, gt1_ref, sh2_ref, sc2_ref, gt2_ref, g1_ref, g2_ref,
                wu_ref, wcv_ref, wcg_ref, wa_ref, wb_ref,
                bsp_ref, csr_ref, csi_ref, ar_ref, ai_ref, d_ref, wv_ref, wg_ref,
                dww_ref, dwb_ref, lng_ref, lnb_ref, pw_ref, wout_ref,
                rw_ref, rb_ref, sgu_ref, sd_ref, ew_gate_ref, ew_up_ref, ew_down_ref,
                x1s_ref, h2_ref, kmap_ref, wc_ref, cnt_ref, eb_gate_ref, eb_up_ref, eb_down_ref,
                bu_ref, sb_ref, sr_ref, si_ref, cbuf_ref, cv_ref):
    step = pl.program_id(0)

    @pl.when(step == 0)
    def _():
        sr_ref[...] = jnp.zeros_like(sr_ref)
        si_ref[...] = jnp.zeros_like(si_ref)
        cbuf_ref[0:CONV_HALO, :] = jnp.zeros((CONV_HALO, CONV_WIDTH), F32)

    x = _time_major(x_ref[...])
    h = _rms_mod(x, g1_ref[...], sc1_ref[...], sh1_ref[...]).astype(BF16)
    u = jnp.dot(h, wu_ref[...], preferred_element_type=F32)
    u_b = u.astype(BF16)
    zc = (jnp.dot(h, wcv_ref[...], preferred_element_type=F32)
          * _sigmoid(jnp.dot(h, wcg_ref[...], preferred_element_type=F32)))

    for j in range(N_STATE // LANES):
        kb = (j * LANES // S5_STATE * S5_GROUP) // LANES
        res = jnp.dot(u_b[:, kb * LANES:(kb + 1) * LANES], bsp_ref[j], preferred_element_type=F32)
        bu_ref[:, j * LANES:(j + 1) * LANES] = res[:, :LANES]
        bu_ref[:, N_STATE + j * LANES:N_STATE + (j + 1) * LANES] = res[:, LANES:]

    gate_a = _sigmoid(jnp.dot(h, wa_ref[...], preferred_element_type=F32))
    gate_b = _sigmoid(jnp.dot(h, wb_ref[...], preferred_element_type=F32))

    for lg in range(N_STATE // SCAN_LANES):
        lo = lg * SCAN_LANES
        re = slice(lo, lo + SCAN_LANES)
        im = slice(N_STATE + lo, N_STATE + lo + SCAN_LANES)
        a_r = jnp.broadcast_to(ar_ref[:, re], (SUBLANES, SCAN_LANES))
        a_i = jnp.broadcast_to(ai_ref[:, re], (SUBLANES, SCAN_LANES))
        s_r = sr_ref[:, re]
        s_i = si_ref[:, re]
        for t in range(0, TILE_STEPS, 2):
            pair_r, pair_i = [], []
            for tt in (t, t + 1):
                rows = slice(tt * SUBLANES, (tt + 1) * SUBLANES)
                s_r, s_i = (a_r * s_r - a_i * s_i + bu_ref[rows, re],
                            a_r * s_i + a_i * s_r + bu_ref[rows, im])
                pair_r.append(s_r)
                pair_i.append(s_i)
            rows2 = slice(t * SUBLANES, (t + 2) * SUBLANES)
            sb_ref[rows2, re] = jnp.concatenate(pair_r, axis=0).astype(BF16)
            sb_ref[rows2, im] = jnp.concatenate(pair_i, axis=0).astype(BF16)
        sr_ref[:, re] = s_r
        si_ref[:, re] = s_i

    half_s = N_STATE // 2
    ys = []
    for hf in range(2):
        s_re = sb_ref[:, hf * half_s:(hf + 1) * half_s]
        s_im = sb_ref[:, N_STATE + hf * half_s:N_STATE + (hf + 1) * half_s]
        ys.append(jnp.dot(s_re, csr_ref[hf], preferred_element_type=F32)
                  - jnp.dot(s_im, csi_ref[hf], preferred_element_type=F32))
    y = jnp.concatenate(ys, axis=-1) + d_ref[...] * u
    z = _gelu_tanh(y).astype(BF16)
    y_a = (jnp.dot(z, wv_ref[...], preferred_element_type=F32)
           * _sigmoid(jnp.dot(z, wg_ref[...], preferred_element_type=F32)))

    cbuf_ref[CONV_HALO:CONV_HALO + TILE_ROWS, :] = zc
    for ci in range(TILE_ROWS // CONV_CHUNK):
        r0 = ci * CONV_CHUNK
        for lt in range(CONV_WIDTH // LANES):
            ls = slice(lt * LANES, (lt + 1) * LANES)
            win = cbuf_ref[r0:r0 + CONV_CHUNK + CONV_HALO, ls]
            acc = jnp.broadcast_to(dwb_ref[:, ls], (CONV_CHUNK, LANES))
            for k in range(CONV_KERNEL):
                acc = acc + dww_ref[k:k + 1, ls] * win[k * BATCH:k * BATCH + CONV_CHUNK]
            cv_ref[r0:r0 + CONV_CHUNK, ls] = acc
    cbuf_ref[0:CONV_HALO, :] = cbuf_ref[TILE_ROWS:TILE_ROWS + CONV_HALO, :]

    cv = cv_ref[...]
    mu = jnp.mean(cv, axis=-1, keepdims=True)
    var = jnp.mean(jnp.square(cv - mu), axis=-1, keepdims=True)
    ln = (cv - mu) * lax.rsqrt(var + EPS) * lng_ref[...] + lnb_ref[...]
    zb = (ln * _sigmoid(ln)).astype(BF16)
    y_b = jnp.dot(zb, pw_ref[...], preferred_element_type=F32)

    m = gate_a * y_a + gate_b * y_b
    mixed = jnp.dot(m.astype(BF16), wout_ref[...], preferred_element_type=F32)

    x1 = x + _per_batch(mixed, gt1_ref[...], lambda r, m: r * m)
    hb = _rms_mod(x1, g2_ref[...], sc2_ref[...], sh2_ref[...]).astype(BF16)
    _store_slabs(h2_ref, 0, _pack_rows(hb.astype(F32)))

    scores = _sigmoid(lax.dot_general(rw_ref[...], hb, (((1,), (1,)), ((), ())),
                                      preferred_element_type=F32))
    expert = lax.broadcasted_iota(I32, scores.shape, 0)
    neg_inf = jnp.float32(-jnp.inf)
    work = scores + rb_ref[...]
    kmap = jnp.zeros(scores.shape, I32)
    for k in range(TOP_K):
        top = jnp.max(work, axis=0, keepdims=True)
        idx = jnp.min(jnp.where(work == top, expert, N_EXPERTS), axis=0, keepdims=True)
        hit = expert == idx
        kmap = jnp.where(hit, k + 1, kmap)
        work = jnp.where(hit, neg_inf, work)
    sel = kmap > 0
    w_sel = jnp.where(sel, scores, 0.0)
    kmap_ref[...] = kmap
    wc_ref[...] = w_sel / jnp.sum(w_sel, axis=0, keepdims=True) * ROUTED_SCALE

    @pl.when(step % (MOE_BLOCK // TILE_ROWS) == 0)
    def _():
        cnt_ref[...] = jnp.zeros_like(cnt_ref)

    cnt_ref[...] += jnp.sum(sel.astype(F32), axis=1, keepdims=True)

    gu = jnp.dot(hb, sgu_ref[...], preferred_element_type=F32)
    g = gu[:, :SHARED_HIDDEN]
    act = (g * _sigmoid(g)) * gu[:, SHARED_HIDDEN:]
    shared = jnp.dot(act.astype(BF16), sd_ref[...], preferred_element_type=F32)
    x1s_ref[...] = x1 + _per_batch(shared, gt2_ref[...], lambda r, m: r * m)

    eb_gate_ref[...] = ew_gate_ref[...].astype(BF16)
    eb_up_ref[...] = ew_up_ref[...].astype(BF16)
    eb_down_ref[...] = ew_down_ref[...].astype(BF16)


def _s5_tiles(bb_r, bb_i, c_re, c_im):
    n_tiles = N_STATE // LANES
    per_tile = LANES // S5_STATE
    per_k = LANES // S5_GROUP
    group = jnp.arange(n_tiles)[:, None] * per_tile + jnp.arange(per_tile)[None, :]
    place = (group[:, :, None] % per_k == jnp.arange(per_k)[None, None, :]).astype(F32)
    bb = jnp.stack([bb_r, bb_i]).reshape(2, n_tiles, per_tile, S5_GROUP, S5_STATE)
    bsp = jnp.einsum("jql,rjqhp->jlhrqp", place, bb).reshape(n_tiles, LANES, 2 * LANES)
    half = S5_GROUPS // 2
    eye = jnp.eye(half, dtype=F32)
    halves = lambda c: jnp.einsum("fghp,gk->fgpkh", c.reshape(2, half, S5_GROUP, S5_STATE), eye).reshape(
        2, half * S5_STATE, half * S5_GROUP)
    return bsp.astype(BF16), halves(c_re).astype(BF16), halves(c_im).astype(BF16)


def _tok(x, mods, consts, expert_weights):
    ts = TILE_ROWS
    steps = SEQ // TILE_STEPS
    tiles_per_block = MOE_BLOCK // ts
    row_spec = lambda rows, c: pl.BlockSpec((rows, c), lambda i: (i, 0))
    route_spec = pl.BlockSpec((N_EXPERTS, ts), lambda i: (0, i))
    per_step = N_EXPERTS // steps
    assert per_step * steps == N_EXPERTS
    cast_specs = [pl.BlockSpec((per_step,) + a.shape[1:], lambda i: (i, 0, 0)) for a in expert_weights]
    return pl.pallas_call(
        _tok_kernel,
        out_shape=(jax.ShapeDtypeStruct((N_TOK, D_MODEL), F32),
                   jax.ShapeDtypeStruct((N_TOK * SLAB, LANES), U32),
                   jax.ShapeDtypeStruct((N_EXPERTS, N_TOK), I32),
                   jax.ShapeDtypeStruct((N_EXPERTS, N_TOK), F32),
                   jax.ShapeDtypeStruct((MOE_BLOCKS, N_EXPERTS, 1), F32))
                  + tuple(jax.ShapeDtypeStruct(a.shape, BF16) for a in expert_weights),
        grid=(steps,),
        in_specs=[_X_SPEC] + [_const_spec(a.shape) for a in tuple(mods) + tuple(consts)] + cast_specs,
        out_specs=(row_spec(ts, D_MODEL), row_spec(ts * SLAB, LANES), route_spec, route_spec,
                   pl.BlockSpec((None, N_EXPERTS, 1), lambda i: (i // tiles_per_block, 0, 0)))
                  + tuple(cast_specs),
        scratch_shapes=[pltpu.VMEM((ts, 2 * N_STATE), F32),
                        pltpu.VMEM((ts, 2 * N_STATE), BF16),
                        pltpu.VMEM((SUBLANES, N_STATE), F32),
                        pltpu.VMEM((SUBLANES, N_STATE), F32),
                        pltpu.VMEM((CONV_HALO + ts, CONV_WIDTH), F32),
                        pltpu.VMEM((ts, CONV_WIDTH), F32)],
        compiler_params=pltpu.CompilerParams(
            dimension_semantics=("arbitrary",), vmem_limit_bytes=VMEM_LIMIT),
        name="tok",
    )(x, *mods, *consts, *expert_weights)


def _slots_kernel(kmap_ref, wc_ref, off_ref, tri_ref, dst_ref, wk_ref, seen_ref):
    @pl.when(pl.program_id(0) % (MOE_BLOCK // SLOT_ROWS) == 0)
    def _():
        seen_ref[...] = jnp.zeros_like(seen_ref)

    kmap = kmap_ref[...]
    wc = wc_ref[...]
    sel = (kmap > 0).astype(F32)
    seen = seen_ref[...]
    slot = jnp.dot(sel.astype(BF16), tri_ref[...], preferred_element_type=F32) + (seen + off_ref[...])
    seen_ref[...] = seen + jnp.sum(sel, axis=1, keepdims=True)
    dst, w_k = [], []
    for k in range(TOP_K):
        hit = kmap == k + 1
        dst.append(jnp.sum(jnp.where(hit, slot, 0.0), axis=0, keepdims=True))
        w_k.append(jnp.sum(jnp.where(hit, wc, 0.0), axis=0, keepdims=True))
    dst_ref[...] = jnp.concatenate(dst, axis=0).astype(I32) * SLAB
    wk_ref[...] = jnp.concatenate(w_k, axis=0)


def _slots(kmap, wc, off):
    ts = SLOT_ROWS
    tiles_per_block = MOE_BLOCK // ts
    route_spec = pl.BlockSpec((N_EXPERTS, ts), lambda i: (0, i))
    slot_spec = pl.BlockSpec((TOP_K, ts), lambda i: (0, i))
    tri = (lax.broadcasted_iota(I32, (ts, ts), 0) < lax.broadcasted_iota(I32, (ts, ts), 1)).astype(BF16)
    return pl.pallas_call(
        _slots_kernel,
        out_shape=(jax.ShapeDtypeStruct((TOP_K, N_TOK), I32), jax.ShapeDtypeStruct((TOP_K, N_TOK), F32)),
        grid=(N_TOK // ts,),
        in_specs=[route_spec, route_spec,
                  pl.BlockSpec((None, N_EXPERTS, 1), lambda i: (i // tiles_per_block, 0, 0)),
                  _const_spec(tri.shape)],
        out_specs=(slot_spec, slot_spec),
        scratch_shapes=[pltpu.VMEM((N_EXPERTS, 1), F32)],
        compiler_params=pltpu.CompilerParams(
            dimension_semantics=("arbitrary",), vmem_limit_bytes=VMEM_LIMIT),
        name="slots",
    )(kmap, wc, off, tri)


def _moe_kernel(cnt_ref, off_ref, order_ref, dst_ref, w_ref, h_ref, wg_hbm, wu_hbm, wd_hbm, o_ref,
                stage_ref, wg_buf, wu_buf, wd_buf, sem):
    blk = pl.program_id(0)

    def step_experts(step):
        a = order_ref[blk * N_EXPERTS + 2 * step]
        b = order_ref[blk * N_EXPERTS + 2 * step + 1]
        return jnp.minimum(a, b), jnp.maximum(a, b)

    def weight_copies(step, slot):
        pairs = ((wg_hbm, wg_buf), (wu_hbm, wu_buf), (wd_hbm, wd_buf))
        return [pltpu.make_async_copy(hbm.at[pl.ds(e, 1)], buf.at[slot, pl.ds(j, 1)], sem.at[i, slot, j])
                for i, (hbm, buf) in enumerate(pairs) for j, e in enumerate(step_experts(step))]

    def fetch(step, slot):
        for copy in weight_copies(step, slot):
            copy.start()

    def arrived(step, slot):
        for copy in weight_copies(step, slot):
            copy.wait()

    fetch(0, 0)

    @pl.when(blk == 0)
    def _():
        stage_ref[TOP_K * MOE_BLOCK * SLAB:, :] = jnp.zeros((MOE_CHUNK * SLAB, LANES), U32)

    def slab_at(ref, sublane_row):
        return ref.at[pl.ds(pl.multiple_of(sublane_row, SLAB), SLAB), :]

    def selections(ref, i):
        first = pl.multiple_of(i * MOE_UNROLL, MOE_UNROLL)
        return [ref.at[0, pl.ds(k * MOE_BLOCK + first, MOE_UNROLL)] for k in range(TOP_K)]

    def dispatch(i, carry):
        dst = selections(dst_ref, i)
        for j in range(MOE_UNROLL):
            slab = slab_at(h_ref, (i * MOE_UNROLL + j) * SLAB)[...]
            for k in range(TOP_K):
                slab_at(stage_ref, dst[k][j])[...] = slab
        return carry

    lax.fori_loop(0, MOE_BLOCK // MOE_UNROLL, dispatch, 0)

    def run_chunks(slot, firsts, live_rows, rows):
        words, outs = [], []
        for j, first in enumerate(firsts):
            words.append(_load_slabs(stage_ref, first * SLAB, rows))
        for j in range(len(firsts)):
            halves = [_unpack_words(w) for w in words[j]]
            xs = jnp.concatenate([lo for lo, _ in halves] + [hi for _, hi in halves], axis=-1)
            xs = xs.astype(BF16)
            g = jnp.dot(xs, wg_buf[slot, j], preferred_element_type=F32)
            act = (g * _sigmoid(g)) * jnp.dot(xs, wu_buf[slot, j], preferred_element_type=F32)
            outs.append(_pack_rows(jnp.dot(act.astype(BF16), wd_buf[slot, j], preferred_element_type=F32)))
        row = lax.broadcasted_iota(I32, (rows, LANES), 0)
        for j, first in enumerate(firsts):
            live = row < live_rows[j]
            for s in range(SLAB):
                merged = jnp.where(live, outs[j][:, s * LANES:(s + 1) * LANES], words[j][s])
                stage_ref[pl.ds(first * SLAB + s, rows, stride=SLAB), :] = merged

    def expert_step(step, carry):
        slot = step % 2

        @pl.when(step + 1 < MOE_EXPERT_STEPS)
        def _():
            fetch(step + 1, 1 - slot)

        arrived(step, slot)
        experts = step_experts(step)
        firsts = [off_ref[blk * N_EXPERTS + e] for e in experts]
        counts = [cnt_ref[blk * N_EXPERTS + e] for e in experts]
        rests = []
        for j in range(MOE_EXPERTS_PER_STEP):
            n_whole = jnp.maximum(counts[j] - 1, 0) // MOE_CHUNK

            def whole(c, carry, j=j):
                run_chunks_single(slot, j, firsts[j] + c * MOE_CHUNK)
                return carry

            lax.fori_loop(0, n_whole, whole, 0)
            firsts[j] = firsts[j] + n_whole * MOE_CHUNK
            rests.append(counts[j] - n_whole * MOE_CHUNK)
        size_class = (jnp.maximum(rests[0], rests[1]) + MOE_GRAIN - 1) // MOE_GRAIN
        for cls in range(1, MOE_CHUNK // MOE_GRAIN + 1):
            @pl.when(size_class == cls)
            def _(cls=cls):
                run_chunks(slot, firsts, rests, cls * MOE_GRAIN)
        return carry

    def run_chunks_single(slot, j, first):
        words = _load_slabs(stage_ref, first * SLAB, MOE_CHUNK)
        halves = [_unpack_words(w) for w in words]
        xs = jnp.concatenate([lo for lo, _ in halves] + [hi for _, hi in halves], axis=-1).astype(BF16)
        g = jnp.dot(xs, wg_buf[slot, j], preferred_element_type=F32)
        act = (g * _sigmoid(g)) * jnp.dot(xs, wu_buf[slot, j], preferred_element_type=F32)
        _store_slabs(stage_ref, first * SLAB,
                     _pack_rows(jnp.dot(act.astype(BF16), wd_buf[slot, j], preferred_element_type=F32)))

    lax.fori_loop(0, MOE_EXPERT_STEPS, expert_step, 0)

    def combine(i, carry):
        dst = selections(dst_ref, i)
        wts = selections(w_ref, i)
        for j in range(MOE_UNROLL):
            t = i * MOE_UNROLL + j
            terms = []
            for k in range(TOP_K):
                w = wts[k][j]
                lo, hi = _unpack_words(slab_at(stage_ref, dst[k][j])[...])
                terms.append((lo * w, hi * w))
            while len(terms) > 1:
                terms = [(a[0] + b[0], a[1] + b[1]) for a, b in zip(terms[0::2], terms[1::2])]
            slab_at(o_ref, t * SLAB)[...] = pltpu.pack_elementwise(list(terms[0]), packed_dtype=BF16)
        return carry

    lax.fori_loop(0, MOE_BLOCK // MOE_UNROLL, combine, 0)


def _moe(cnt, off, order, dst, w, h2s, wg, wu, wd):
    smem_spec = pl.BlockSpec((None, 1, TOP_K * MOE_BLOCK), lambda b, *_: (b, 0, 0), memory_space=pltpu.SMEM)
    hbm_spec = pl.BlockSpec(memory_space=pl.ANY)
    weight_buf = lambda rows, cols: pltpu.VMEM((2, MOE_EXPERTS_PER_STEP, rows, cols), BF16)
    return pl.pallas_call(
        _moe_kernel,
        out_shape=jax.ShapeDtypeStruct((N_TOK * SLAB, LANES), U32),
        grid_spec=pltpu.PrefetchScalarGridSpec(
            num_scalar_prefetch=3,
            grid=(MOE_BLOCKS,),
            in_specs=[smem_spec, smem_spec,
                      pl.BlockSpec((MOE_BLOCK * SLAB, LANES), lambda b, *_: (b, 0),
                                   pipeline_mode=pl.Buffered(1)),
                      hbm_spec, hbm_spec, hbm_spec],
            out_specs=pl.BlockSpec((MOE_BLOCK * SLAB, LANES), lambda b, *_: (b, 0)),
            scratch_shapes=[pltpu.VMEM((STAGE_ROWS * SLAB, LANES), U32),
                            weight_buf(D_MODEL, EXPERT_HIDDEN), weight_buf(D_MODEL, EXPERT_HIDDEN),
                            weight_buf(EXPERT_HIDDEN, D_MODEL),
                            pltpu.SemaphoreType.DMA((3, 2, MOE_EXPERTS_PER_STEP))]),
        compiler_params=pltpu.CompilerParams(
            dimension_semantics=("arbitrary",), vmem_limit_bytes=VMEM_LIMIT),
        name="moe",
    )(cnt, off, order, dst, w, h2s, wg, wu, wd)


def _fin_kernel(x1s_ref, r_ref, gt2_ref, fg_ref, o_ref):
    halves = [_unpack_words(w) for w in _load_slabs(r_ref, 0, x1s_ref.shape[0])]
    routed = jnp.concatenate([lo for lo, _ in halves] + [hi for _, hi in halves], axis=-1)
    x2 = x1s_ref[...] + _per_batch(routed, gt2_ref[...], lambda r, m: r * m)
    ms = jnp.mean(x2 * x2, axis=-1, keepdims=True)
    o_ref[...] = _batch_major(x2 * lax.rsqrt(ms + EPS) * fg_ref[...])


def _fin(x1s, routed, gt2, fg):
    ts = FIN_STEPS * BATCH
    return pl.pallas_call(
        _fin_kernel,
        out_shape=jax.ShapeDtypeStruct((BATCH, SEQ, D_MODEL), F32),
        grid=(SEQ // FIN_STEPS,),
        in_specs=[pl.BlockSpec((ts, D_MODEL), lambda i: (i, 0)),
                  pl.BlockSpec((ts * SLAB, LANES), lambda i: (i, 0)),
                  _const_spec((1, BATCH, D_MODEL)),
                  _const_spec((1, D_MODEL))],
        out_specs=pl.BlockSpec((BATCH, FIN_STEPS, D_MODEL), lambda i: (0, i, 0)),
        compiler_params=pltpu.CompilerParams(
            dimension_semantics=("parallel",), vmem_limit_bytes=VMEM_LIMIT),
        name="fin",
    )(x1s, routed, gt2, fg)


def kernel(x, c, w_ada, b_ada, norm1_g, w_in, s5_lambda_re, s5_lambda_im, s5_log_dt, s5_b_re, s5_b_im, s5_c_re, s5_c_im, s5_d, s5_glu_wv, s5_glu_wg, conv_dw_w, conv_dw_b, conv_ln_g, conv_ln_b, conv_pw_w, w_out, norm2_g, router_w, router_bias, exp_w_gate, exp_w_up, exp_w_down, shared_w_gate, shared_w_up, shared_w_down, final_norm_g):
    l = 0
    row = lambda a: a.reshape(1, -1)

    mod = _ada(c, w_ada[l], b_ada[l])
    mods = [m.reshape(1, BATCH, D_MODEL) for m in jnp.split(mod, N_MOD, axis=-1)]

    abar_r, abar_i, bbar_r, bbar_i = _s5prep(s5_lambda_re[l], s5_lambda_im[l], s5_log_dt[l],
                                             s5_b_re[l], s5_b_im[l])
    per_group = lambda a: a.reshape(S5_GROUPS, S5_GROUP, S5_STATE)
    a_r = per_group(abar_r)[:, 0, :].reshape(1, N_STATE)
    a_i = per_group(abar_i)[:, 0, :].reshape(1, N_STATE)

    o1 = S5_WIDTH
    o2 = o1 + CONV_WIDTH
    o3 = o2 + CONV_WIDTH
    o4 = o3 + D_MODEL
    w = w_in[l].astype(BF16)
    sgu = jnp.concatenate([shared_w_gate[l], shared_w_up[l]], axis=1).astype(BF16)
    consts = (row(norm1_g[l]), row(norm2_g[l]),
              w[:, :o1], w[:, o1:o2], w[:, o2:o3], w[:, o3:o4], w[:, o4:],
              *_s5_tiles(per_group(bbar_r), per_group(bbar_i), s5_c_re[l], s5_c_im[l]),
              a_r, a_i, row(s5_d[l]),
              s5_glu_wv[l].astype(BF16), s5_glu_wg[l].astype(BF16),
              conv_dw_w[l], row(conv_dw_b[l]), row(conv_ln_g[l]), row(conv_ln_b[l]),
              conv_pw_w[l].astype(BF16), w_out[l].astype(BF16),
              router_w[l].T.astype(BF16), router_bias[l].reshape(N_EXPERTS, 1),
              sgu, shared_w_down[l].astype(BF16))
    x1s, h2s, kmap, wc, cnt, wg, wu, wd = _tok(x, mods, consts,
                                               (exp_w_gate[l], exp_w_up[l], exp_w_down[l]))

    off = jnp.cumsum(cnt, axis=1) - cnt
    dst, w_k = _slots(kmap, wc, off)
    as_ints = lambda a: a.astype(I32).reshape(-1)
    per_block = lambda a: jnp.swapaxes(a.reshape(TOP_K, MOE_BLOCKS, MOE_BLOCK), 0, 1).reshape(
        MOE_BLOCKS, 1, TOP_K * MOE_BLOCK)

    order = jnp.argsort(cnt[:, :, 0], axis=1)
    routed = _moe(as_ints(cnt), as_ints(off), as_ints(order), per_block(dst), per_block(w_k), h2s,
                  wg, wu, wd)
    return _fin(x1s, routed, mods[5], row(final_norm_g))
```

```python
import jax
import jax.numpy as jnp
from jax import lax
from jax.experimental import pallas as pl
from jax.experimental.pallas import tpu as pltpu

D_MODEL = 1024
BATCH = 8
SEQ = 4096
N_TOK = BATCH * SEQ
S5_WIDTH = 512
S5_GROUP = 16
S5_GROUPS = 32
S5_STATE = 64
N_STATE = S5_GROUPS * S5_STATE
CONV_WIDTH = 512
CONV_KERNEL = 31
N_EXPERTS = 64
TOP_K = 8
EXPERT_HIDDEN = 256
SHARED_HIDDEN = 256
ROUTED_SCALE = 2.5
N_MOD = 6
EPS = 1e-6

LANES = 128
SUBLANES = 8
VMEM_LIMIT = 56 * 1024 * 1024

TILE_STEPS = 64
TILE_ROWS = TILE_STEPS * BATCH
FIN_STEPS = 128
SLOT_ROWS = 1024
CONV_HALO = (CONV_KERNEL - 1) * BATCH
CONV_CHUNK = 64
CONV_CHUNKS = TILE_ROWS // CONV_CHUNK
GATE_COLS = 2 * D_MODEL // CONV_CHUNKS
SCAN_LANES = 512
MOE_BLOCK = 2048
MOE_BLOCKS = N_TOK // MOE_BLOCK
MOE_CHUNK = 512
MOE_GRAIN = 64
MOE_EXPERTS_PER_STEP = 2
MOE_EXPERT_STEPS = N_EXPERTS // MOE_EXPERTS_PER_STEP
MOE_UNROLL = 8
SLAB = 4
STAGE_ROWS = TOP_K * MOE_BLOCK + MOE_CHUNK

F32 = jnp.float32
BF16 = jnp.bfloat16
U32 = jnp.uint32
I32 = jnp.int32


_X_SPEC = pl.BlockSpec((BATCH, TILE_STEPS, D_MODEL), lambda i: (0, i, 0))


def _time_major(a):
    return jnp.swapaxes(a, 0, 1).reshape(a.shape[0] * a.shape[1], a.shape[2])


def _batch_major(a):
    return jnp.swapaxes(a.reshape(a.shape[0] // BATCH, BATCH, a.shape[1]), 0, 1)


def _per_batch(rows, mod, op):
    r3 = rows.reshape(rows.shape[0] // BATCH, BATCH, rows.shape[1])
    return op(r3, mod).reshape(rows.shape)


def _pack_rows(x):
    half = D_MODEL // 2
    return pltpu.pack_elementwise([x[:, :half], x[:, half:]], packed_dtype=BF16)


def _unpack_words(w):
    lo = pltpu.unpack_elementwise(w, index=0, packed_dtype=BF16, unpacked_dtype=F32)
    hi = pltpu.unpack_elementwise(w, index=1, packed_dtype=BF16, unpacked_dtype=F32)
    return lo, hi


def _store_slabs(ref, base, words):
    rows = words.shape[0]
    for s in range(SLAB):
        ref[pl.ds(base + s, rows, stride=SLAB), :] = words[:, s * LANES:(s + 1) * LANES]


def _load_slabs(ref, base, rows):
    return [ref[pl.ds(base + s, rows, stride=SLAB), :] for s in range(SLAB)]


def _sigmoid(x):
    return jax.nn.sigmoid(x)


def _const_spec(shape):
    zeros = (0,) * len(shape)
    return pl.BlockSpec(shape, lambda *_: zeros, pipeline_mode=pl.Buffered(1))


def _ada_kernel(c_ref, w_ref, b_ref, o_ref):
    c = c_ref[...]
    c_act = c * _sigmoid(c)
    o_ref[...] = jnp.dot(c_act, w_ref[...], preferred_element_type=F32,
                         precision=lax.Precision.HIGHEST) + b_ref[...]


def _ada(c, w_ada, b_ada):
    n_out = N_MOD * D_MODEL
    blk = 1536
    return pl.pallas_call(
        _ada_kernel,
        out_shape=jax.ShapeDtypeStruct((BATCH, n_out), F32),
        grid=(n_out // blk,),
        in_specs=[pl.BlockSpec((BATCH, D_MODEL), lambda j: (0, 0)),
                  pl.BlockSpec((D_MODEL, blk), lambda j: (0, j)),
                  pl.BlockSpec((1, blk), lambda j: (0, j))],
        out_specs=pl.BlockSpec((BATCH, blk), lambda j: (0, j)),
        compiler_params=pltpu.CompilerParams(vmem_limit_bytes=VMEM_LIMIT),
        name="ada",
    )(c, w_ada, b_ada.reshape(1, n_out))


def _s5prep_kernel(lr_ref, li_ref, ldt_ref, br_ref, bi_ref, ar_ref, ai_ref, bbr_ref, bbi_ref):
    lr = lr_ref[...]
    li = li_ref[...]
    dt = jnp.exp(ldt_ref[...])
    mag = jnp.exp(lr * dt)
    abar_r = mag * jnp.cos(li * dt)
    abar_i = mag * jnp.sin(li * dt)
    den = lr * lr + li * li
    nr = abar_r - 1.0
    ni = abar_i
    k_r = (nr * lr + ni * li) / den
    k_i = (ni * lr - nr * li) / den
    br = br_ref[...]
    bi = bi_ref[...]
    ar_ref[...] = abar_r
    ai_ref[...] = abar_i
    bbr_ref[...] = k_r * br - k_i * bi
    bbi_ref[...] = k_r * bi + k_i * br


def _s5prep(lam_re, lam_im, log_dt, b_re, b_im):
    rows = S5_GROUPS * S5_GROUP
    rep = lambda a: jnp.broadcast_to(a[:, None, :], (S5_GROUPS, S5_GROUP, S5_STATE)).reshape(rows, S5_STATE)
    lr = rep(lam_re)
    li = rep(lam_im)
    ldt = rep(jnp.broadcast_to(log_dt[:, None], (S5_GROUPS, S5_STATE)))
    br = jnp.transpose(b_re, (0, 2, 1)).reshape(rows, S5_STATE)
    bi = jnp.transpose(b_im, (0, 2, 1)).reshape(rows, S5_STATE)
    shp = jax.ShapeDtypeStruct((rows, S5_STATE), F32)
    return pl.pallas_call(_s5prep_kernel, out_shape=(shp, shp, shp, shp), name="s5prep")(lr, li, ldt, br, bi)


def _gelu_tanh(x):
    sqrt_2_over_pi = 0.7978845608028654
    cdf = 0.5 * (1.0 + jnp.tanh(sqrt_2_over_pi * (x + 0.044715 * (x * x * x))))
    return x * cdf


def _rms_mod(x, g, scale, shift):
    ms = jnp.mean(x * x, axis=-1, keepdims=True)
    h = x * lax.rsqrt(ms + EPS) * g
    return _per_batch(h, (scale, shift), lambda r, m: r * (1.0 + m[0]) + m[1])


def _tok_kernel(x_ref, sh1_ref, sc1_ref, gt1_ref, sh2_ref, sc2_ref, gt2_ref, g1_ref, g2_ref,
                wu_ref, wcv_ref, wcg_ref, wab_ref,
                bsp_ref, csr_ref, csi_ref, ar_ref, ai_ref, d_ref, wv_ref, wg_ref,
                dww_ref, dwb_ref, lng_ref, lnb_ref, pw_ref, wout_ref,
                rw_ref, rb_ref, sgu_ref, sd_ref, ew_gate_ref, ew_up_ref, ew_down_ref,
                x1s_ref, h2_ref, kmap_ref, wc_ref, cnt_ref, eb_gate_ref, eb_up_ref, eb_down_ref,
                bu_ref, sb_ref, sr_ref, si_ref, cbuf_ref, cv_ref, hs_ref, gates_ref):
    step = pl.program_id(0)

    @pl.when(step == 0)
    def _():
        sr_ref[...] = jnp.zeros_like(sr_ref)
        si_ref[...] = jnp.zeros_like(si_ref)
        cbuf_ref[0:CONV_HALO, :] = jnp.zeros((CONV_HALO, CONV_WIDTH), F32)

    x = _time_major(x_ref[...])
    h = _rms_mod(x, g1_ref[...], sc1_ref[...], sh1_ref[...]).astype(BF16)
    hs_ref[...] = h
    u = jnp.dot(h, wu_ref[...], preferred_element_type=F32)
    u_b = u.astype(BF16)
    zc = (jnp.dot(h, wcv_ref[...], preferred_element_type=F32)
          * _sigmoid(jnp.dot(h, wcg_ref[...], preferred_element_type=F32)))

    for j in range(N_STATE // LANES):
        kb = (j * LANES // S5_STATE * S5_GROUP) // LANES
        res = jnp.dot(u_b[:, kb * LANES:(kb + 1) * LANES], bsp_ref[j], preferred_element_type=F32)
        bu_ref[:, j * LANES:(j + 1) * LANES] = res[:, :LANES]
        bu_ref[:, N_STATE + j * LANES:N_STATE + (j + 1) * LANES] = res[:, LANES:]

    for lg in range(N_STATE // SCAN_LANES):
        lo = lg * SCAN_LANES
        re = slice(lo, lo + SCAN_LANES)
        im = slice(N_STATE + lo, N_STATE + lo + SCAN_LANES)
        a_r = jnp.broadcast_to(ar_ref[:, re], (SUBLANES, SCAN_LANES))
        a_i = jnp.broadcast_to(ai_ref[:, re], (SUBLANES, SCAN_LANES))
        s_r = sr_ref[:, re]
        s_i = si_ref[:, re]
        for t in range(0, TILE_STEPS, 2):
            pair_r, pair_i = [], []
            for tt in (t, t + 1):
                rows = slice(tt * SUBLANES, (tt + 1) * SUBLANES)
                s_r, s_i = (a_r * s_r - a_i * s_i + bu_ref[rows, re],
                            a_r * s_i + a_i * s_r + bu_ref[rows, im])
                pair_r.append(s_r)
                pair_i.append(s_i)
            rows2 = slice(t * SUBLANES, (t + 2) * SUBLANES)
            sb_ref[rows2, re] = jnp.concatenate(pair_r, axis=0).astype(BF16)
            sb_ref[rows2, im] = jnp.concatenate(pair_i, axis=0).astype(BF16)
        sr_ref[:, re] = s_r
        si_ref[:, re] = s_i

    half_s = N_STATE // 2
    ys = []
    for hf in range(2):
        s_re = sb_ref[:, hf * half_s:(hf + 1) * half_s]
        s_im = sb_ref[:, N_STATE + hf * half_s:N_STATE + (hf + 1) * half_s]
        ys.append(jnp.dot(s_re, csr_ref[hf], preferred_element_type=F32)
                  - jnp.dot(s_im, csi_ref[hf], preferred_element_type=F32))
    y = jnp.concatenate(ys, axis=-1) + d_ref[...] * u
    z = _gelu_tanh(y).astype(BF16)
    y_a = (jnp.dot(z, wv_ref[...], preferred_element_type=F32)
           * _sigmoid(jnp.dot(z, wg_ref[...], preferred_element_type=F32)))

    cbuf_ref[CONV_HALO:CONV_HALO + TILE_ROWS, :] = zc

    def conv_chunk(ci, carry):
        gates_ref[ci] = jnp.dot(hs_ref[...], wab_ref[ci], preferred_element_type=F32)
        r0 = pl.multiple_of(ci * CONV_CHUNK, CONV_CHUNK)
        for lt in range(CONV_WIDTH // LANES):
            ls = slice(lt * LANES, (lt + 1) * LANES)
            win = cbuf_ref[pl.ds(r0, CONV_CHUNK + CONV_HALO), ls]
            acc = jnp.broadcast_to(dwb_ref[:, ls], (CONV_CHUNK, LANES))
            for k in range(CONV_KERNEL):
                acc = acc + dww_ref[k:k + 1, ls] * win[k * BATCH:k * BATCH + CONV_CHUNK]
            cv_ref[pl.ds(r0, CONV_CHUNK), ls] = acc
        return carry

    lax.fori_loop(0, CONV_CHUNKS, conv_chunk, 0)
    cbuf_ref[0:CONV_HALO, :] = cbuf_ref[TILE_ROWS:TILE_ROWS + CONV_HALO, :]

    cv = cv_ref[...]
    mu = jnp.mean(cv, axis=-1, keepdims=True)
    var = jnp.mean(jnp.square(cv - mu), axis=-1, keepdims=True)
    ln = (cv - mu) * lax.rsqrt(var + EPS) * lng_ref[...] + lnb_ref[...]
    zb = (ln * _sigmoid(ln)).astype(BF16)
    y_b = jnp.dot(zb, pw_ref[...], preferred_element_type=F32)

    gate = lambda g: _sigmoid(jnp.concatenate(
        [gates_ref[g * CONV_CHUNKS // 2 + i] for i in range(CONV_CHUNKS // 2)], axis=-1))
    m = gate(0) * y_a + gate(1) * y_b
    mixed = jnp.dot(m.astype(BF16), wout_ref[...], preferred_element_type=F32)

    x1 = x + _per_batch(mixed, gt1_ref[...], lambda r, m: r * m)
    hb = _rms_mod(x1, g2_ref[...], sc2_ref[...], sh2_ref[...]).astype(BF16)
    _store_slabs(h2_ref, 0, _pack_rows(hb.astype(F32)))

    scores = _sigmoid(lax.dot_general(rw_ref[...], hb, (((1,), (1,)), ((), ())),
                                      preferred_element_type=F32))
    expert = lax.broadcasted_iota(I32, scores.shape, 0)
    neg_inf = jnp.float32(-jnp.inf)
    work = scores + rb_ref[...]
    kmap = jnp.zeros(scores.shape, I32)
    for k in range(TOP_K):
        top = jnp.max(work, axis=0, keepdims=True)
        idx = jnp.min(jnp.where(work == top, expert, N_EXPERTS), axis=0, keepdims=True)
        hit = expert == idx
        kmap = jnp.where(hit, k + 1, kmap)
        work = jnp.where(hit, neg_inf, work)
    sel = kmap > 0
    w_sel = jnp.where(sel, scores, 0.0)
    kmap_ref[...] = kmap
    wc_ref[...] = w_sel / jnp.sum(w_sel, axis=0, keepdims=True) * ROUTED_SCALE

    @pl.when(step % (MOE_BLOCK // TILE_ROWS) == 0)
    def _():
        cnt_ref[...] = jnp.zeros_like(cnt_ref)

    cnt_ref[...] += jnp.sum(sel.astype(F32), axis=1, keepdims=True)

    gu = jnp.dot(hb, sgu_ref[...], preferred_element_type=F32)
    g = gu[:, :SHARED_HIDDEN]
    act = (g * _sigmoid(g)) * gu[:, SHARED_HIDDEN:]
    shared = jnp.dot(act.astype(BF16), sd_ref[...], preferred_element_type=F32)
    x1s_ref[...] = x1 + _per_batch(shared, gt2_ref[...], lambda r, m: r * m)

    eb_gate_ref[...] = ew_gate_ref[...].astype(BF16)
    eb_up_ref[...] = ew_up_ref[...].astype(BF16)
    eb_down_ref[...] = ew_down_ref[...].astype(BF16)


def _s5_tiles(bb_r, bb_i, c_re, c_im):
    n_tiles = N_STATE // LANES
    per_tile = LANES // S5_STATE
    per_k = LANES // S5_GROUP
    group = jnp.arange(n_tiles)[:, None] * per_tile + jnp.arange(per_tile)[None, :]
    place = (group[:, :, None] % per_k == jnp.arange(per_k)[None, None, :]).astype(F32)
    bb = jnp.stack([bb_r, bb_i]).reshape(2, n_tiles, per_tile, S5_GROUP, S5_STATE)
    bsp = jnp.einsum("jql,rjqhp->jlhrqp", place, bb).reshape(n_tiles, LANES, 2 * LANES)
    half = S5_GROUPS // 2
    eye = jnp.eye(half, dtype=F32)
    halves = lambda c: jnp.einsum("fghp,gk->fgpkh", c.reshape(2, half, S5_GROUP, S5_STATE), eye).reshape(
        2, half * S5_STATE, half * S5_GROUP)
    return bsp.astype(BF16), halves(c_re).astype(BF16), halves(c_im).astype(BF16)


def _tok(x, mods, consts, expert_weights):
    ts = TILE_ROWS
    steps = SEQ // TILE_STEPS
    tiles_per_block = MOE_BLOCK // ts
    row_spec = lambda rows, c: pl.BlockSpec((rows, c), lambda i: (i, 0))
    route_spec = pl.BlockSpec((N_EXPERTS, ts), lambda i: (0, i))
    per_step = N_EXPERTS // steps
    assert per_step * steps == N_EXPERTS
    cast_specs = [pl.BlockSpec((per_step,) + a.shape[1:], lambda i: (i, 0, 0)) for a in expert_weights]
    return pl.pallas_call(
        _tok_kernel,
        out_shape=(jax.ShapeDtypeStruct((N_TOK, D_MODEL), F32),
                   jax.ShapeDtypeStruct((N_TOK * SLAB, LANES), U32),
                   jax.ShapeDtypeStruct((N_EXPERTS, N_TOK), I32),
                   jax.ShapeDtypeStruct((N_EXPERTS, N_TOK), F32),
                   jax.ShapeDtypeStruct((MOE_BLOCKS, N_EXPERTS, 1), F32))
                  + tuple(jax.ShapeDtypeStruct(a.shape, BF16) for a in expert_weights),
        grid=(steps,),
        in_specs=[_X_SPEC] + [_const_spec(a.shape) for a in tuple(mods) + tuple(consts)] + cast_specs,
        out_specs=(row_spec(ts, D_MODEL), row_spec(ts * SLAB, LANES), route_spec, route_spec,
                   pl.BlockSpec((None, N_EXPERTS, 1), lambda i: (i // tiles_per_block, 0, 0)))
                  + tuple(cast_specs),
        scratch_shapes=[pltpu.VMEM((ts, 2 * N_STATE), F32),
                        pltpu.VMEM((ts, 2 * N_STATE), BF16),
                        pltpu.VMEM((SUBLANES, N_STATE), F32),
                        pltpu.VMEM((SUBLANES, N_STATE), F32),
                        pltpu.VMEM((CONV_HALO + ts, CONV_WIDTH), F32),
                        pltpu.VMEM((ts, CONV_WIDTH), F32),
                        pltpu.VMEM((ts, D_MODEL), BF16),
                        pltpu.VMEM((CONV_CHUNKS, ts, GATE_COLS), F32)],
        compiler_params=pltpu.CompilerParams(
            dimension_semantics=("arbitrary",), vmem_limit_bytes=VMEM_LIMIT),
        name="tok",
    )(x, *mods, *consts, *expert_weights)


def _slots_kernel(kmap_ref, wc_ref, off_ref, tri_ref, dst_ref, wk_ref, seen_ref):
    @pl.when(pl.program_id(0) % (MOE_BLOCK // SLOT_ROWS) == 0)
    def _():
        seen_ref[...] = jnp.zeros_like(seen_ref)

    kmap = kmap_ref[...]
    wc = wc_ref[...]
    sel = (kmap > 0).astype(F32)
    seen = seen_ref[...]
    slot = jnp.dot(sel.astype(BF16), tri_ref[...], preferred_element_type=F32) + (seen + off_ref[...])
    seen_ref[...] = seen + jnp.sum(sel, axis=1, keepdims=True)
    dst, w_k = [], []
    for k in range(TOP_K):
        hit = kmap == k + 1
        dst.append(jnp.sum(jnp.where(hit, slot, 0.0), axis=0, keepdims=True))
        w_k.append(jnp.sum(jnp.where(hit, wc, 0.0), axis=0, keepdims=True))
    dst_ref[...] = jnp.concatenate(dst, axis=0).astype(I32) * SLAB
    wk_ref[...] = jnp.concatenate(w_k, axis=0)


def _slots(kmap, wc, off):
    ts = SLOT_ROWS
    tiles_per_block = MOE_BLOCK // ts
    route_spec = pl.BlockSpec((N_EXPERTS, ts), lambda i: (0, i))
    slot_spec = pl.BlockSpec((TOP_K, ts), lambda i: (0, i))
    tri = (lax.broadcasted_iota(I32, (ts, ts), 0) < lax.broadcasted_iota(I32, (ts, ts), 1)).astype(BF16)
    return pl.pallas_call(
        _slots_kernel,
        out_shape=(jax.ShapeDtypeStruct((TOP_K, N_TOK), I32), jax.ShapeDtypeStruct((TOP_K, N_TOK), F32)),
        grid=(N_TOK // ts,),
        in_specs=[route_spec, route_spec,
                  pl.BlockSpec((None, N_EXPERTS, 1), lambda i: (i // tiles_per_block, 0, 0)),
                  _const_spec(tri.shape)],
        out_specs=(slot_spec, slot_spec),
        scratch_shapes=[pltpu.VMEM((N_EXPERTS, 1), F32)],
        compiler_params=pltpu.CompilerParams(
            dimension_semantics=("arbitrary",), vmem_limit_bytes=VMEM_LIMIT),
        name="slots",
    )(kmap, wc, off, tri)


def _moe_kernel(cnt_ref, off_ref, order_ref, dst_ref, w_ref, h_ref, wg_hbm, wu_hbm, wd_hbm, o_ref,
                stage_ref, wg_buf, wu_buf, wd_buf, sem):
    blk = pl.program_id(0)

    def step_experts(step):
        a = order_ref[blk * N_EXPERTS + 2 * step]
        b = order_ref[blk * N_EXPERTS + 2 * step + 1]
        return jnp.minimum(a, b), jnp.maximum(a, b)

    def weight_copies(step, slot):
        pairs = ((wg_hbm, wg_buf), (wu_hbm, wu_buf), (wd_hbm, wd_buf))
        return [pltpu.make_async_copy(hbm.at[pl.ds(e, 1)], buf.at[slot, pl.ds(j, 1)], sem.at[i, slot, j])
                for i, (hbm, buf) in enumerate(pairs) for j, e in enumerate(step_experts(step))]

    def fetch(step, slot):
        for copy in weight_copies(step, slot):
            copy.start()

    def arrived(step, slot):
        for copy in weight_copies(step, slot):
            copy.wait()

    fetch(0, 0)

    @pl.when(blk == 0)
    def _():
        stage_ref[TOP_K * MOE_BLOCK * SLAB:, :] = jnp.zeros((MOE_CHUNK * SLAB, LANES), U32)

    def slab_at(ref, sublane_row):
        return ref.at[pl.ds(pl.multiple_of(sublane_row, SLAB), SLAB), :]

    def selections(ref, i):
        first = pl.multiple_of(i * MOE_UNROLL, MOE_UNROLL)
        return [ref.at[0, pl.ds(k * MOE_BLOCK + first, MOE_UNROLL)] for k in range(TOP_K)]

    def dispatch(i, carry):
        dst = selections(dst_ref, i)
        for j in range(MOE_UNROLL):
            slab = slab_at(h_ref, (i * MOE_UNROLL + j) * SLAB)[...]
            for k in range(TOP_K):
                slab_at(stage_ref, dst[k][j])[...] = slab
        return carry

    lax.fori_loop(0, MOE_BLOCK // MOE_UNROLL, dispatch, 0)

    def run_chunks(slot, firsts, live_rows, rows):
        words, outs = [], []
        for j, first in enumerate(firsts):
            words.append(_load_slabs(stage_ref, first * SLAB, rows))
        for j in range(len(firsts)):
            halves = [_unpack_words(w) for w in words[j]]
            xs = jnp.concatenate([lo for lo, _ in halves] + [hi for _, hi in halves], axis=-1)
            xs = xs.astype(BF16)
            g = jnp.dot(xs, wg_buf[slot, j], preferred_element_type=F32)
            act = (g * _sigmoid(g)) * jnp.dot(xs, wu_buf[slot, j], preferred_element_type=F32)
            outs.append(_pack_rows(jnp.dot(act.astype(BF16), wd_buf[slot, j], preferred_element_type=F32)))
        row = lax.broadcasted_iota(I32, (rows, LANES), 0)
        for j, first in enumerate(firsts):
            live = row < live_rows[j]
            for s in range(SLAB):
                merged = jnp.where(live, outs[j][:, s * LANES:(s + 1) * LANES], words[j][s])
                stage_ref[pl.ds(first * SLAB + s, rows, stride=SLAB), :] = merged

    def expert_step(step, carry):
        slot = step % 2

        @pl.when(step + 1 < MOE_EXPERT_STEPS)
        def _():
            fetch(step + 1, 1 - slot)

        arrived(step, slot)
        experts = step_experts(step)
        firsts = [off_ref[blk * N_EXPERTS + e] for e in experts]
        counts = [cnt_ref[blk * N_EXPERTS + e] for e in experts]
        rests = []
        for j in range(MOE_EXPERTS_PER_STEP):
            n_whole = jnp.maximum(counts[j] - 1, 0) // MOE_CHUNK

            def whole(c, carry, j=j):
                run_chunks_single(slot, j, firsts[j] + c * MOE_CHUNK)
                return carry

            lax.fori_loop(0, n_whole, whole, 0)
            firsts[j] = firsts[j] + n_whole * MOE_CHUNK
            rests.append(counts[j] - n_whole * MOE_CHUNK)
        size_class = (jnp.maximum(rests[0], rests[1]) + MOE_GRAIN - 1) // MOE_GRAIN
        for cls in range(1, MOE_CHUNK // MOE_GRAIN + 1):
            @pl.when(size_class == cls)
            def _(cls=cls):
                run_chunks(slot, firsts, rests, cls * MOE_GRAIN)
        return carry

    def run_chunks_single(slot, j, first):
        words = _load_slabs(stage_ref, first * SLAB, MOE_CHUNK)
        halves = [_unpack_words(w) for w in words]
        xs = jnp.concatenate([lo for lo, _ in halves] + [hi for _, hi in halves], axis=-1).astype(BF16)
        g = jnp.dot(xs, wg_buf[slot, j], preferred_element_type=F32)
        act = (g * _sigmoid(g)) * jnp.dot(xs, wu_buf[slot, j], preferred_element_type=F32)
        _store_slabs(stage_ref, first * SLAB,
                     _pack_rows(jnp.dot(act.astype(BF16), wd_buf[slot, j], preferred_element_type=F32)))

    lax.fori_loop(0, MOE_EXPERT_STEPS, expert_step, 0)

    def combine(i, carry):
        dst = selections(dst_ref, i)
        wts = selections(w_ref, i)
        for j in range(MOE_UNROLL):
            t = i * MOE_UNROLL + j
            terms = []
            for k in range(TOP_K):
                w = wts[k][j]
                lo, hi = _unpack_words(slab_at(stage_ref, dst[k][j])[...])
                terms.append((lo * w, hi * w))
            while len(terms) > 1:
                terms = [(a[0] + b[0], a[1] + b[1]) for a, b in zip(terms[0::2], terms[1::2])]
            slab_at(o_ref, t * SLAB)[...] = pltpu.pack_elementwise(list(terms[0]), packed_dtype=BF16)
        return carry

    lax.fori_loop(0, MOE_BLOCK // MOE_UNROLL, combine, 0)


def _moe(cnt, off, order, dst, w, h2s, wg, wu, wd):
    smem_spec = pl.BlockSpec((None, 1, TOP_K * MOE_BLOCK), lambda b, *_: (b, 0, 0), memory_space=pltpu.SMEM)
    hbm_spec = pl.BlockSpec(memory_space=pl.ANY)
    weight_buf = lambda rows, cols: pltpu.VMEM((2, MOE_EXPERTS_PER_STEP, rows, cols), BF16)
    return pl.pallas_call(
        _moe_kernel,
        out_shape=jax.ShapeDtypeStruct((N_TOK * SLAB, LANES), U32),
        grid_spec=pltpu.PrefetchScalarGridSpec(
            num_scalar_prefetch=3,
            grid=(MOE_BLOCKS,),
            in_specs=[smem_spec, smem_spec,
                      pl.BlockSpec((MOE_BLOCK * SLAB, LANES), lambda b, *_: (b, 0),
                                   pipeline_mode=pl.Buffered(1)),
                      hbm_spec, hbm_spec, hbm_spec],
            out_specs=pl.BlockSpec((MOE_BLOCK * SLAB, LANES), lambda b, *_: (b, 0)),
            scratch_shapes=[pltpu.VMEM((STAGE_ROWS * SLAB, LANES), U32),
                            weight_buf(D_MODEL, EXPERT_HIDDEN), weight_buf(D_MODEL, EXPERT_HIDDEN),
                            weight_buf(EXPERT_HIDDEN, D_MODEL),
                            pltpu.SemaphoreType.DMA((3, 2, MOE_EXPERTS_PER_STEP))]),
        compiler_params=pltpu.CompilerParams(
            dimension_semantics=("arbitrary",), vmem_limit_bytes=VMEM_LIMIT),
        name="moe",
    )(cnt, off, order, dst, w, h2s, wg, wu, wd)


def _fin_kernel(x1s_ref, r_ref, gt2_ref, fg_ref, o_ref):
    halves = [_unpack_words(w) for w in _load_slabs(r_ref, 0, x1s_ref.shape[0])]
    routed = jnp.concatenate([lo for lo, _ in halves] + [hi for _, hi in halves], axis=-1)
    x2 = x1s_ref[...] + _per_batch(routed, gt2_ref[...], lambda r, m: r * m)
    ms = jnp.mean(x2 * x2, axis=-1, keepdims=True)
    o_ref[...] = _batch_major(x2 * lax.rsqrt(ms + EPS) * fg_ref[...])


def _fin(x1s, routed, gt2, fg):
    ts = FIN_STEPS * BATCH
    return pl.pallas_call(
        _fin_kernel,
        out_shape=jax.ShapeDtypeStruct((BATCH, SEQ, D_MODEL), F32),
        grid=(SEQ // FIN_STEPS,),
        in_specs=[pl.BlockSpec((ts, D_MODEL), lambda i: (i, 0)),
                  pl.BlockSpec((ts * SLAB, LANES), lambda i: (i, 0)),
                  _const_spec((1, BATCH, D_MODEL)),
                  _const_spec((1, D_MODEL))],
        out_specs=pl.BlockSpec((BATCH, FIN_STEPS, D_MODEL), lambda i: (0, i, 0)),
        compiler_params=pltpu.CompilerParams(
            dimension_semantics=("parallel",), vmem_limit_bytes=VMEM_LIMIT),
        name="fin",
    )(x1s, routed, gt2, fg)


def kernel(x, c, w_ada, b_ada, norm1_g, w_in, s5_lambda_re, s5_lambda_im, s5_log_dt, s5_b_re, s5_b_im, s5_c_re, s5_c_im, s5_d, s5_glu_wv, s5_glu_wg, conv_dw_w, conv_dw_b, conv_ln_g, conv_ln_b, conv_pw_w, w_out, norm2_g, router_w, router_bias, exp_w_gate, exp_w_up, exp_w_down, shared_w_gate, shared_w_up, shared_w_down, final_norm_g):
    l = 0
    row = lambda a: a.reshape(1, -1)

    mod = _ada(c, w_ada[l], b_ada[l])
    mods = [m.reshape(1, BATCH, D_MODEL) for m in jnp.split(mod, N_MOD, axis=-1)]

    abar_r, abar_i, bbar_r, bbar_i = _s5prep(s5_lambda_re[l], s5_lambda_im[l], s5_log_dt[l],
                                             s5_b_re[l], s5_b_im[l])
    per_group = lambda a: a.reshape(S5_GROUPS, S5_GROUP, S5_STATE)
    a_r = per_group(abar_r)[:, 0, :].reshape(1, N_STATE)
    a_i = per_group(abar_i)[:, 0, :].reshape(1, N_STATE)

    o1 = S5_WIDTH
    o2 = o1 + CONV_WIDTH
    o3 = o2 + CONV_WIDTH
    o4 = o3 + D_MODEL
    w = w_in[l].astype(BF16)
    sgu = jnp.concatenate([shared_w_gate[l], shared_w_up[l]], axis=1).astype(BF16)
    consts = (row(norm1_g[l]), row(norm2_g[l]),
              w[:, :o1], w[:, o1:o2], w[:, o2:o3],
              jnp.swapaxes(w[:, o3:].reshape(D_MODEL, CONV_CHUNKS, GATE_COLS), 0, 1),
              *_s5_tiles(per_group(bbar_r), per_group(bbar_i), s5_c_re[l], s5_c_im[l]),
              a_r, a_i, row(s5_d[l]),
              s5_glu_wv[l].astype(BF16), s5_glu_wg[l].astype(BF16),
              conv_dw_w[l], row(conv_dw_b[l]), row(conv_ln_g[l]), row(conv_ln_b[l]),
              conv_pw_w[l].astype(BF16), w_out[l].astype(BF16),
              router_w[l].T.astype(BF16), router_bias[l].reshape(N_EXPERTS, 1),
              sgu, shared_w_down[l].astype(BF16))
    x1s, h2s, kmap, wc, cnt, wg, wu, wd = _tok(x, mods, consts,
                                               (exp_w_gate[l], exp_w_up[l], exp_w_down[l]))

    off = jnp.cumsum(cnt, axis=1) - cnt
    dst, w_k = _slots(kmap, wc, off)
    as_ints = lambda a: a.astype(I32).reshape(-1)
    per_block = lambda a: jnp.swapaxes(a.reshape(TOP_K, MOE_BLOCKS, MOE_BLOCK), 0, 1).reshape(
        MOE_BLOCKS, 1, TOP_K * MOE_BLOCK)

    order = jnp.argsort(cnt[:, :, 0], axis=1)
    routed = _moe(as_ints(cnt), as_ints(off), as_ints(order), per_block(dst), per_block(w_k), h2s,
                  wg, wu, wd)
    return _fin(x1s, routed, mods[5], row(final_norm_g))
```

```python
import jax
import jax.numpy as jnp
from jax import lax
from jax.experimental import pallas as pl
from jax.experimental.pallas import tpu as pltpu

D_MODEL = 1024
BATCH = 8
SEQ = 4096
N_TOK = BATCH * SEQ
S5_WIDTH = 512
S5_GROUP = 16
S5_GROUPS = 32
S5_STATE = 64
N_STATE = S5_GROUPS * S5_STATE
CONV_WIDTH = 512
CONV_KERNEL = 31
N_EXPERTS = 64
TOP_K = 8
EXPERT_HIDDEN = 256
SHARED_HIDDEN = 256
ROUTED_SCALE = 2.5
N_MOD = 6
EPS = 1e-6

LANES = 128
SUBLANES = 8
VMEM_LIMIT = 56 * 1024 * 1024

TILE_STEPS = 64
TILE_ROWS = TILE_STEPS * BATCH
FIN_STEPS = 128
SLOT_ROWS = 1024
CONV_HALO = (CONV_KERNEL - 1) * BATCH
CONV_CHUNK = 64
SCAN_LANES = 512
MOE_BLOCK = 2048
MOE_BLOCKS = N_TOK // MOE_BLOCK
MOE_CHUNK = 512
MOE_GRAIN = 64
MOE_EXPERTS_PER_STEP = 2
MOE_EXPERT_STEPS = N_EXPERTS // MOE_EXPERTS_PER_STEP
MOE_UNROLL = 8
SLAB = 4
STAGE_ROWS = TOP_K * MOE_BLOCK + MOE_CHUNK

F32 = jnp.float32
BF16 = jnp.bfloat16
U32 = jnp.uint32
I32 = jnp.int32


_X_SPEC = pl.BlockSpec((BATCH, TILE_STEPS, D_MODEL), lambda i: (0, i, 0))


def _time_major(a):
    return jnp.swapaxes(a, 0, 1).reshape(a.shape[0] * a.shape[1], a.shape[2])


def _batch_major(a):
    return jnp.swapaxes(a.reshape(a.shape[0] // BATCH, BATCH, a.shape[1]), 0, 1)


def _per_batch(rows, mod, op):
    r3 = rows.reshape(rows.shape[0] // BATCH, BATCH, rows.shape[1])
    return op(r3, mod).reshape(rows.shape)


def _pack_rows(x):
    half = D_MODEL // 2
    return pltpu.pack_elementwise([x[:, :half], x[:, half:]], packed_dtype=BF16)


def _unpack_words(w):
    lo = pltpu.unpack_elementwise(w, index=0, packed_dtype=BF16, unpacked_dtype=F32)
    hi = pltpu.unpack_elementwise(w, index=1, packed_dtype=BF16, unpacked_dtype=F32)
    return lo, hi


def _store_slabs(ref, base, words):
    rows = words.shape[0]
    for s in range(SLAB):
        ref[pl.ds(base + s, rows, stride=SLAB), :] = words[:, s * LANES:(s + 1) * LANES]


def _load_slabs(ref, base, rows):
    return [ref[pl.ds(base + s, rows, stride=SLAB), :] for s in range(SLAB)]


def _sigmoid(x):
    return jax.nn.sigmoid(x)


def _const_spec(shape):
    zeros = (0,) * len(shape)
    return pl.BlockSpec(shape, lambda *_: zeros, pipeline_mode=pl.Buffered(1))


def _ada_kernel(c_ref, w_ref, b_ref, o_ref):
    c = c_ref[...]
    c_act = c * _sigmoid(c)
    o_ref[...] = jnp.dot(c_act, w_ref[...], preferred_element_type=F32,
                         precision=lax.Precision.HIGHEST) + b_ref[...]


def _ada(c, w_ada, b_ada):
    n_out = N_MOD * D_MODEL
    blk = 1536
    return pl.pallas_call(
        _ada_kernel,
        out_shape=jax.ShapeDtypeStruct((BATCH, n_out), F32),
        grid=(n_out // blk,),
        in_specs=[pl.BlockSpec((BATCH, D_MODEL), lambda j: (0, 0)),
                  pl.BlockSpec((D_MODEL, blk), lambda j: (0, j)),
                  pl.BlockSpec((1, blk), lambda j: (0, j))],
        out_specs=pl.BlockSpec((BATCH, blk), lambda j: (0, j)),
        compiler_params=pltpu.CompilerParams(vmem_limit_bytes=VMEM_LIMIT),
        name="ada",
    )(c, w_ada, b_ada.reshape(1, n_out))


def _s5prep_kernel(lr_ref, li_ref, ldt_ref, br_ref, bi_ref, ar_ref, ai_ref, bbr_ref, bbi_ref):
    lr = lr_ref[...]
    li = li_ref[...]
    dt = jnp.exp(ldt_ref[...])
    mag = jnp.exp(lr * dt)
    abar_r = mag * jnp.cos(li * dt)
    abar_i = mag * jnp.sin(li * dt)
    den = lr * lr + li * li
    nr = abar_r - 1.0
    ni = abar_i
    k_r = (nr * lr + ni * li) / den
    k_i = (ni * lr - nr * li) / den
    br = br_ref[...]
    bi = bi_ref[...]
    ar_ref[...] = abar_r
    ai_ref[...] = abar_i
    bbr_ref[...] = k_r * br - k_i * bi
    bbi_ref[...] = k_r * bi + k_i * br


def _s5prep(lam_re, lam_im, log_dt, b_re, b_im):
    rows = S5_GROUPS * S5_GROUP
    rep = lambda a: jnp.broadcast_to(a[:, None, :], (S5_GROUPS, S5_GROUP, S5_STATE)).reshape(rows, S5_STATE)
    lr = rep(lam_re)
    li = rep(lam_im)
    ldt = rep(jnp.broadcast_to(log_dt[:, None], (S5_GROUPS, S5_STATE)))
    br = jnp.transpose(b_re, (0, 2, 1)).reshape(rows, S5_STATE)
    bi = jnp.transpose(b_im, (0, 2, 1)).reshape(rows, S5_STATE)
    shp = jax.ShapeDtypeStruct((rows, S5_STATE), F32)
    return pl.pallas_call(_s5prep_kernel, out_shape=(shp, shp, shp, shp), name="s5prep")(lr, li, ldt, br, bi)


def _gelu_tanh(x):
    sqrt_2_over_pi = 0.7978845608028654
    cdf = 0.5 * (1.0 + jnp.tanh(sqrt_2_over_pi * (x + 0.044715 * (x * x * x))))
    return x * cdf


def _rms_mod(x, g, scale, shift):
    ms = jnp.mean(x * x, axis=-1, keepdims=True)
    h = x * lax.rsqrt(ms + EPS) * g
    return _per_batch(h, (scale, shift), lambda r, m: r * (1.0 + m[0]) + m[1])


def _tok_kernel(x_ref, sh1_ref, sc1_ref, gt1_ref, sh2_ref, sc2_ref, gt2_ref, g1_ref, g2_ref,
                wu_ref, wcv_ref, wcg_ref, wa_ref, wb_ref,
                bsp_ref, csr_ref, csi_ref, ar_ref, ai_ref, d_ref, wv_ref, wg_ref,
                dww_ref, dwb_ref, lng_ref, lnb_ref, pw_ref, wout_ref,
                rw_ref, rb_ref, sgu_ref, sd_ref, ew_gate_ref, ew_up_ref, ew_down_ref,
                x1s_ref, h2_ref, kmap_ref, wc_ref, cnt_ref, eb_gate_ref, eb_up_ref, eb_down_ref,
                bu_ref, sb_ref, sr_ref, si_ref, cbuf_ref, cv_ref):
    step = pl.program_id(0)

    @pl.when(step == 0)
    def _():
        sr_ref[...] = jnp.zeros_like(sr_ref)
        si_ref[...] = jnp.zeros_like(si_ref)
        cbuf_ref[0:CONV_HALO, :] = jnp.zeros((CONV_HALO, CONV_WIDTH), F32)

    x = _time_major(x_ref[...])
    h = _rms_mod(x, g1_ref[...], sc1_ref[...], sh1_ref[...]).astype(BF16)
    u = jnp.dot(h, wu_ref[...], preferred_element_type=F32)
    u_b = u.astype(BF16)
    zc = (jnp.dot(h, wcv_ref[...], preferred_element_type=F32)
          * _sigmoid(jnp.dot(h, wcg_ref[...], preferred_element_type=F32)))

    for j in range(N_STATE // LANES):
        kb = (j * LANES // S5_STATE * S5_GROUP) // LANES
        res = jnp.dot(u_b[:, kb * LANES:(kb + 1) * LANES], bsp_ref[j], preferred_element_type=F32)
        bu_ref[:, j * LANES:(j + 1) * LANES] = res[:, :LANES]
        bu_ref[:, N_STATE + j * LANES:N_STATE + (j + 1) * LANES] = res[:, LANES:]

    gate_a = _sigmoid(jnp.dot(h, wa_ref[...], preferred_element_type=F32))
    gate_b = _sigmoid(jnp.dot(h, wb_ref[...], preferred_element_type=F32))

    for lg in range(N_STATE // SCAN_LANES):
        lo = lg * SCAN_LANES
        re = slice(lo, lo + SCAN_LANES)
        im = slice(N_STATE + lo, N_STATE + lo + SCAN_LANES)
        a_r = jnp.broadcast_to(ar_ref[:, re], (SUBLANES, SCAN_LANES))
        a_i = jnp.broadcast_to(ai_ref[:, re], (SUBLANES, SCAN_LANES))
        s_r = sr_ref[:, re]
        s_i = si_ref[:, re]
        for t in range(0, TILE_STEPS, 2):
            pair_r, pair_i = [], []
            for tt in (t, t + 1):
                rows = slice(tt * SUBLANES, (tt + 1) * SUBLANES)
                s_r, s_i = (a_r * s_r - a_i * s_i + bu_ref[rows, re],
                            a_r * s_i + a_i * s_r + bu_ref[rows, im])
                pair_r.append(s_r)
                pair_i.append(s_i)
            rows2 = slice(t * SUBLANES, (t + 2) * SUBLANES)
            sb_ref[rows2, re] = jnp.concatenate(pair_r, axis=0).astype(BF16)
            sb_ref[rows2, im] = jnp.concatenate(pair_i, axis=0).astype(BF16)
        sr_ref[:, re] = s_r
        si_ref[:, re] = s_i

    half_s = N_STATE // 2
    ys = []
    for hf in range(2):
        s_re = sb_ref[:, hf * half_s:(hf + 1) * half_s]
        s_im = sb_ref[:, N_STATE + hf * half_s:N_STATE + (hf + 1) * half_s]
        ys.append(jnp.dot(s_re, csr_ref[hf], preferred_element_type=F32)
                  - jnp.dot(s_im, csi_ref[hf], preferred_element_type=F32))
    y = jnp.concatenate(ys, axis=-1) + d_ref[...] * u
    z = _gelu_tanh(y).astype(BF16)
    y_a = (jnp.dot(z, wv_ref[...], preferred_element_type=F32)
           * _sigmoid(jnp.dot(z, wg_ref[...], preferred_element_type=F32)))

    cbuf_ref[CONV_HALO:CONV_HALO + TILE_ROWS, :] = zc

    def conv_chunk(ci, carry):
        r0 = pl.multiple_of(ci * CONV_CHUNK, CONV_CHUNK)
        for lt in range(CONV_WIDTH // LANES):
            ls = slice(lt * LANES, (lt + 1) * LANES)
            win = cbuf_ref[pl.ds(r0, CONV_CHUNK + CONV_HALO), ls]
            acc = jnp.broadcast_to(dwb_ref[:, ls], (CONV_CHUNK, LANES))
            for k in range(CONV_KERNEL):
                acc = acc + dww_ref[k:k + 1, ls] * win[k * BATCH:k * BATCH + CONV_CHUNK]
            cv_ref[pl.ds(r0, CONV_CHUNK), ls] = acc
        return carry

    lax.fori_loop(0, TILE_ROWS // CONV_CHUNK, conv_chunk, 0)
    cbuf_ref[0:CONV_HALO, :] = cbuf_ref[TILE_ROWS:TILE_ROWS + CONV_HALO, :]

    cv = cv_ref[...]
    mu = jnp.mean(cv, axis=-1, keepdims=True)
    var = jnp.mean(jnp.square(cv - mu), axis=-1, keepdims=True)
    ln = (cv - mu) * lax.rsqrt(var + EPS) * lng_ref[...] + lnb_ref[...]
    zb = (ln * _sigmoid(ln)).astype(BF16)
    y_b = jnp.dot(zb, pw_ref[...], preferred_element_type=F32)

    m = gate_a * y_a + gate_b * y_b
    mixed = jnp.dot(m.astype(BF16), wout_ref[...], preferred_element_type=F32)

    x1 = x + _per_batch(mixed, gt1_ref[...], lambda r, m: r * m)
    hb = _rms_mod(x1, g2_ref[...], sc2_ref[...], sh2_ref[...]).astype(BF16)
    _store_slabs(h2_ref, 0, _pack_rows(hb.astype(F32)))

    scores = _sigmoid(lax.dot_general(rw_ref[...], hb, (((1,), (1,)), ((), ())),
                                      preferred_element_type=F32))
    expert = lax.broadcasted_iota(I32, scores.shape, 0)
    neg_inf = jnp.float32(-jnp.inf)
    work = scores + rb_ref[...]
    kmap = jnp.zeros(scores.shape, I32)
    for k in range(TOP_K):
        top = jnp.max(work, axis=0, keepdims=True)
        idx = jnp.min(jnp.where(work == top, expert, N_EXPERTS), axis=0, keepdims=True)
        hit = expert == idx
        kmap = jnp.where(hit, k + 1, kmap)
        work = jnp.where(hit, neg_inf, work)
    sel = kmap > 0
    w_sel = jnp.where(sel, scores, 0.0)
    kmap_ref[...] = kmap
    wc_ref[...] = w_sel / jnp.sum(w_sel, axis=0, keepdims=True) * ROUTED_SCALE

    @pl.when(step % (MOE_BLOCK // TILE_ROWS) == 0)
    def _():
        cnt_ref[...] = jnp.zeros_like(cnt_ref)

    cnt_ref[...] += jnp.sum(sel.astype(F32), axis=1, keepdims=True)

    gu = jnp.dot(hb, sgu_ref[...], preferred_element_type=F32)
    g = gu[:, :SHARED_HIDDEN]
    act = (g * _sigmoid(g)) * gu[:, SHARED_HIDDEN:]
    shared = jnp.dot(act.astype(BF16), sd_ref[...], preferred_element_type=F32)
    x1s_ref[...] = x1 + _per_batch(shared, gt2_ref[...], lambda r, m: r * m)

    eb_gate_ref[...] = ew_gate_ref[...].astype(BF16)
    eb_up_ref[...] = ew_up_ref[...].astype(BF16)
    eb_down_ref[...] = ew_down_ref[...].astype(BF16)


def _s5_tiles(bb_r, bb_i, c_re, c_im):
    n_tiles = N_STATE // LANES
    per_tile = LANES // S5_STATE
    per_k = LANES // S5_GROUP
    group = jnp.arange(n_tiles)[:, None] * per_tile + jnp.arange(per_tile)[None, :]
    place = (group[:, :, None] % per_k == jnp.arange(per_k)[None, None, :]).astype(F32)
    bb = jnp.stack([bb_r, bb_i]).reshape(2, n_tiles, per_tile, S5_GROUP, S5_STATE)
    bsp = jnp.einsum("jql,rjqhp->jlhrqp", place, bb).reshape(n_tiles, LANES, 2 * LANES)
    half = S5_GROUPS // 2
    eye = jnp.eye(half, dtype=F32)
    halves = lambda c: jnp.einsum("fghp,gk->fgpkh", c.reshape(2, half, S5_GROUP, S5_STATE), eye).reshape(
        2, half * S5_STATE, half * S5_GROUP)
    return bsp.astype(BF16), halves(c_re).astype(BF16), halves(c_im).astype(BF16)


def _tok(x, mods, consts, expert_weights):
    ts = TILE_ROWS
    steps = SEQ // TILE_STEPS
    tiles_per_block = MOE_BLOCK // ts
    row_spec = lambda rows, c: pl.BlockSpec((rows, c), lambda i: (i, 0))
    route_spec = pl.BlockSpec((N_EXPERTS, ts), lambda i: (0, i))
    per_step = N_EXPERTS // steps
    assert per_step * steps == N_EXPERTS
    cast_specs = [pl.BlockSpec((per_step,) + a.shape[1:], lambda i: (i, 0, 0)) for a in expert_weights]
    return pl.pallas_call(
        _tok_kernel,
        out_shape=(jax.ShapeDtypeStruct((N_TOK, D_MODEL), F32),
                   jax.ShapeDtypeStruct((N_TOK * SLAB, LANES), U32),
                   jax.ShapeDtypeStruct((N_EXPERTS, N_TOK), I32),
                   jax.ShapeDtypeStruct((N_EXPERTS, N_TOK), F32),
                   jax.ShapeDtypeStruct((MOE_BLOCKS, N_EXPERTS, 1), F32))
                  + tuple(jax.ShapeDtypeStruct(a.shape, BF16) for a in expert_weights),
        grid=(steps,),
        in_specs=[_X_SPEC] + [_const_spec(a.shape) for a in tuple(mods) + tuple(consts)] + cast_specs,
        out_specs=(row_spec(ts, D_MODEL), row_spec(ts * SLAB, LANES), route_spec, route_spec,
                   pl.BlockSpec((None, N_EXPERTS, 1), lambda i: (i // tiles_per_block, 0, 0)))
                  + tuple(cast_specs),
        scratch_shapes=[pltpu.VMEM((ts, 2 * N_STATE), F32),
                        pltpu.VMEM((ts, 2 * N_STATE), BF16),
                        pltpu.VMEM((SUBLANES, N_STATE), F32),
                        pltpu.VMEM((SUBLANES, N_STATE), F32),
                        pltpu.VMEM((CONV_HALO + ts, CONV_WIDTH), F32),
                        pltpu.VMEM((ts, CONV_WIDTH), F32)],
        compiler_params=pltpu.CompilerParams(
            dimension_semantics=("arbitrary",), vmem_limit_bytes=VMEM_LIMIT),
        name="tok",
    )(x, *mods, *consts, *expert_weights)


def _slots_kernel(kmap_ref, wc_ref, off_ref, tri_ref, dst_ref, wk_ref, seen_ref):
    @pl.when(pl.program_id(0) % (MOE_BLOCK // SLOT_ROWS) == 0)
    def _():
        seen_ref[...] = jnp.zeros_like(seen_ref)

    kmap = kmap_ref[...]
    wc = wc_ref[...]
    sel = (kmap > 0).astype(F32)
    seen = seen_ref[...]
    slot = jnp.dot(sel.astype(BF16), tri_ref[...], preferred_element_type=F32) + (seen + off_ref[...])
    seen_ref[...] = seen + jnp.sum(sel, axis=1, keepdims=True)
    dst, w_k = [], []
    for k in range(TOP_K):
        hit = kmap == k + 1
        dst.append(jnp.sum(jnp.where(hit, slot, 0.0), axis=0, keepdims=True))
        w_k.append(jnp.sum(jnp.where(hit, wc, 0.0), axis=0, keepdims=True))
    dst_ref[...] = jnp.concatenate(dst, axis=0).astype(I32) * SLAB
    wk_ref[...] = jnp.concatenate(w_k, axis=0)


def _slots(kmap, wc, off):
    ts = SLOT_ROWS
    tiles_per_block = MOE_BLOCK // ts
    route_spec = pl.BlockSpec((N_EXPERTS, ts), lambda i: (0, i))
    slot_spec = pl.BlockSpec((TOP_K, ts), lambda i: (0, i))
    tri = (lax.broadcasted_iota(I32, (ts, ts), 0) < lax.broadcasted_iota(I32, (ts, ts), 1)).astype(BF16)
    return pl.pallas_call(
        _slots_kernel,
        out_shape=(jax.ShapeDtypeStruct((TOP_K, N_TOK), I32), jax.ShapeDtypeStruct((TOP_K, N_TOK), F32)),
        grid=(N_TOK // ts,),
        in_specs=[route_spec, route_spec,
                  pl.BlockSpec((None, N_EXPERTS, 1), lambda i: (i // tiles_per_block, 0, 0)),
                  _const_spec(tri.shape)],
        out_specs=(slot_spec, slot_spec),
        scratch_shapes=[pltpu.VMEM((N_EXPERTS, 1), F32)],
        compiler_params=pltpu.CompilerParams(
            dimension_semantics=("arbitrary",), vmem_limit_bytes=VMEM_LIMIT),
        name="slots",
    )(kmap, wc, off, tri)


def _moe_kernel(cnt_ref, off_ref, order_ref, dst_ref, w_ref, h_ref, wg_hbm, wu_hbm, wd_hbm, o_ref,
                stage_ref, wg_buf, wu_buf, wd_buf, sem):
    blk = pl.program_id(0)

    def step_experts(step):
        a = order_ref[blk * N_EXPERTS + 2 * step]
        b = order_ref[blk * N_EXPERTS + 2 * step + 1]
        return jnp.minimum(a, b), jnp.maximum(a, b)

    def weight_copies(step, slot):
        pairs = ((wg_hbm, wg_buf), (wu_hbm, wu_buf), (wd_hbm, wd_buf))
        return [pltpu.make_async_copy(hbm.at[pl.ds(e, 1)], buf.at[slot, pl.ds(j, 1)], sem.at[i, slot, j])
                for i, (hbm, buf) in enumerate(pairs) for j, e in enumerate(step_experts(step))]

    def fetch(step, slot):
        for copy in weight_copies(step, slot):
            copy.start()

    def arrived(step, slot):
        for copy in weight_copies(step, slot):
            copy.wait()

    fetch(0, 0)

    @pl.when(blk == 0)
    def _():
        stage_ref[TOP_K * MOE_BLOCK * SLAB:, :] = jnp.zeros((MOE_CHUNK * SLAB, LANES), U32)

    def slab_at(ref, sublane_row):
        return ref.at[pl.ds(pl.multiple_of(sublane_row, SLAB), SLAB), :]

    def selections(ref, i):
        first = pl.multiple_of(i * MOE_UNROLL, MOE_UNROLL)
        return [ref.at[0, pl.ds(k * MOE_BLOCK + first, MOE_UNROLL)] for k in range(TOP_K)]

    def dispatch(i, carry):
        dst = selections(dst_ref, i)
        for j in range(MOE_UNROLL):
            slab = slab_at(h_ref, (i * MOE_UNROLL + j) * SLAB)[...]
            for k in range(TOP_K):
                slab_at(stage_ref, dst[k][j])[...] = slab
        return carry

    lax.fori_loop(0, MOE_BLOCK // MOE_UNROLL, dispatch, 0)

    def run_chunks(slot, firsts, live_rows, rows):
        words, outs = [], []
        for j, first in enumerate(firsts):
            words.append(_load_slabs(stage_ref, first * SLAB, rows))
        for j in range(len(firsts)):
            halves = [_unpack_words(w) for w in words[j]]
            xs = jnp.concatenate([lo for lo, _ in halves] + [hi for _, hi in halves], axis=-1)
            xs = xs.astype(BF16)
            g = jnp.dot(xs, wg_buf[slot, j], preferred_element_type=F32)
            act = (g * _sigmoid(g)) * jnp.dot(xs, wu_buf[slot, j], preferred_element_type=F32)
            outs.append(_pack_rows(jnp.dot(act.astype(BF16), wd_buf[slot, j], preferred_element_type=F32)))
        row = lax.broadcasted_iota(I32, (rows, LANES), 0)
        for j, first in enumerate(firsts):
            live = row < live_rows[j]
            for s in range(SLAB):
                merged = jnp.where(live, outs[j][:, s * LANES:(s + 1) * LANES], words[j][s])
                stage_ref[pl.ds(first * SLAB + s, rows, stride=SLAB), :] = merged

    def expert_step(step, carry):
        slot = step % 2

        @pl.when(step + 1 < MOE_EXPERT_STEPS)
        def _():
            fetch(step + 1, 1 - slot)

        arrived(step, slot)
        experts = step_experts(step)
        firsts = [off_ref[blk * N_EXPERTS + e] for e in experts]
        counts = [cnt_ref[blk * N_EXPERTS + e] for e in experts]
        rests = []
        for j in range(MOE_EXPERTS_PER_STEP):
            n_whole = jnp.maximum(counts[j] - 1, 0) // MOE_CHUNK

            def whole(c, carry, j=j):
                run_chunks_single(slot, j, firsts[j] + c * MOE_CHUNK)
                return carry

            lax.fori_loop(0, n_whole, whole, 0)
            firsts[j] = firsts[j] + n_whole * MOE_CHUNK
            rests.append(counts[j] - n_whole * MOE_CHUNK)
        size_class = (jnp.maximum(rests[0], rests[1]) + MOE_GRAIN - 1) // MOE_GRAIN
        for cls in range(1, MOE_CHUNK // MOE_GRAIN + 1):
            @pl.when(size_class == cls)
            def _(cls=cls):
                run_chunks(slot, firsts, rests, cls * MOE_GRAIN)
        return carry

    def run_chunks_single(slot, j, first):
        words = _load_slabs(stage_ref, first * SLAB, MOE_CHUNK)
        halves = [_unpack_words(w) for w in words]
        xs = jnp.concatenate([lo for lo, _ in halves] + [hi for _, hi in halves], axis=-1).astype(BF16)
        g = jnp.dot(xs, wg_buf[slot, j], preferred_element_type=F32)
        act = (g * _sigmoid(g)) * jnp.dot(xs, wu_buf[slot, j], preferred_element_type=F32)
        _store_slabs(stage_ref, first * SLAB,
                     _pack_rows(jnp.dot(act.astype(BF16), wd_buf[slot, j], preferred_element_type=F32)))

    lax.fori_loop(0, MOE_EXPERT_STEPS, expert_step, 0)

    def combine(i, carry):
        dst = selections(dst_ref, i)
        wts = selections(w_ref, i)
        for j in range(MOE_UNROLL):
            t = i * MOE_UNROLL + j
            terms = []
            for k in range(TOP_K):
                w = wts[k][j]
                lo, hi = _unpack_words(slab_at(stage_ref, dst[k][j])[...])
                terms.append((lo * w, hi * w))
            while len(terms) > 1:
                terms = [(a[0] + b[0], a[1] + b[1]) for a, b in zip(terms[0::2], terms[1::2])]
            slab_at(o_ref, t * SLAB)[...] = pltpu.pack_elementwise(list(terms[0]), packed_dtype=BF16)
        return carry

    lax.fori_loop(0, MOE_BLOCK // MOE_UNROLL, combine, 0)


def _moe(cnt, off, order, dst, w, h2s, wg, wu, wd):
    smem_spec = pl.BlockSpec((None, 1, TOP_K * MOE_BLOCK), lambda b, *_: (b, 0, 0), memory_space=pltpu.SMEM)
    hbm_spec = pl.BlockSpec(memory_space=pl.ANY)
    weight_buf = lambda rows, cols: pltpu.VMEM((2, MOE_EXPERTS_PER_STEP, rows, cols), BF16)
    return pl.pallas_call(
        _moe_kernel,
        out_shape=jax.ShapeDtypeStruct((N_TOK * SLAB, LANES), U32),
        grid_spec=pltpu.PrefetchScalarGridSpec(
            num_scalar_prefetch=3,
            grid=(MOE_BLOCKS,),
            in_specs=[smem_spec, smem_spec,
                      pl.BlockSpec((MOE_BLOCK * SLAB, LANES), lambda b, *_: (b, 0),
                                   pipeline_mode=pl.Buffered(1)),
                      hbm_spec, hbm_spec, hbm_spec],
            out_specs=pl.BlockSpec((MOE_BLOCK * SLAB, LANES), lambda b, *_: (b, 0)),
            scratch_shapes=[pltpu.VMEM((STAGE_ROWS * SLAB, LANES), U32),
                            weight_buf(D_MODEL, EXPERT_HIDDEN), weight_buf(D_MODEL, EXPERT_HIDDEN),
                            weight_buf(EXPERT_HIDDEN, D_MODEL),
                            pltpu.SemaphoreType.DMA((3, 2, MOE_EXPERTS_PER_STEP))]),
        compiler_params=pltpu.CompilerParams(
            dimension_semantics=("arbitrary",), vmem_limit_bytes=VMEM_LIMIT),
        name="moe",
    )(cnt, off, order, dst, w, h2s, wg, wu, wd)


def _fin_kernel(x1s_ref, r_ref, gt2_ref, fg_ref, o_ref):
    halves = [_unpack_words(w) for w in _load_slabs(r_ref, 0, x1s_ref.shape[0])]
    routed = jnp.concatenate([lo for lo, _ in halves] + [hi for _, hi in halves], axis=-1)
    x2 = x1s_ref[...] + _per_batch(routed, gt2_ref[...], lambda r, m: r * m)
    ms = jnp.mean(x2 * x2, axis=-1, keepdims=True)
    o_ref[...] = _batch_major(x2 * lax.rsqrt(ms + EPS) * fg_ref[...])


def _fin(x1s, routed, gt2, fg):
    ts = FIN_STEPS * BATCH
    return pl.pallas_call(
        _fin_kernel,
        out_shape=jax.ShapeDtypeStruct((BATCH, SEQ, D_MODEL), F32),
        grid=(SEQ // FIN_STEPS,),
        in_specs=[pl.BlockSpec((ts, D_MODEL), lambda i: (i, 0)),
                  pl.BlockSpec((ts * SLAB, LANES), lambda i: (i, 0)),
                  _const_spec((1, BATCH, D_MODEL)),
                  _const_spec((1, D_MODEL))],
        out_specs=pl.BlockSpec((BATCH, FIN_STEPS, D_MODEL), lambda i: (0, i, 0)),
        compiler_params=pltpu.CompilerParams(
            dimension_semantics=("parallel",), vmem_limit_bytes=VMEM_LIMIT),
        name="fin",
    )(x1s, routed, gt2, fg)


def kernel(x, c, w_ada, b_ada, norm1_g, w_in, s5_lambda_re, s5_lambda_im, s5_log_dt, s5_b_re, s5_b_im, s5_c_re, s5_c_im, s5_d, s5_glu_wv, s5_glu_wg, conv_dw_w, conv_dw_b, conv_ln_g, conv_ln_b, conv_pw_w, w_out, norm2_g, router_w, router_bias, exp_w_gate, exp_w_up, exp_w_down, shared_w_gate, shared_w_up, shared_w_down, final_norm_g):
    l = 0
    row = lambda a: a.reshape(1, -1)

    mod = _ada(c, w_ada[l], b_ada[l])
    mods = [m.reshape(1, BATCH, D_MODEL) for m in jnp.split(mod, N_MOD, axis=-1)]

    abar_r, abar_i, bbar_r, bbar_i = _s5prep(s5_lambda_re[l], s5_lambda_im[l], s5_log_dt[l],
                                             s5_b_re[l], s5_b_im[l])
    per_group = lambda a: a.reshape(S5_GROUPS, S5_GROUP, S5_STATE)
    a_r = per_group(abar_r)[:, 0, :].reshape(1, N_STATE)
    a_i = per_group(abar_i)[:, 0, :].reshape(1, N_STATE)

    o1 = S5_WIDTH
    o2 = o1 + CONV_WIDTH
    o3 = o2 + CONV_WIDTH
    o4 = o3 + D_MODEL
    w = w_in[l].astype(BF16)
    sgu = jnp.concatenate([shared_w_gate[l], shared_w_up[l]], axis=1).astype(BF16)
    consts = (row(norm1_g[l]), row(norm2_g[l]),
              w[:, :o1], w[:, o1:o2], w[:, o2:o3], w[:, o3:o4], w[:, o4:],
              *_s5_tiles(per_group(bbar_r), per_group(bbar_i), s5_c_re[l], s5_c_im[l]),
              a_r, a_i, row(s5_d[l]),
              s5_glu_wv[l].astype(BF16), s5_glu_wg[l].astype(BF16),
              conv_dw_w[l], row(conv_dw_b[l]), row(conv_ln_g[l]), row(conv_ln_b[l]),
              conv_pw_w[l].astype(BF16), w_out[l].astype(BF16),
              router_w[l].T.astype(BF16), router_bias[l].reshape(N_EXPERTS, 1),
              sgu, shared_w_down[l].astype(BF16))
    x1s, h2s, kmap, wc, cnt, wg, wu, wd = _tok(x, mods, consts,
                                               (exp_w_gate[l], exp_w_up[l], exp_w_down[l]))

    off = jnp.cumsum(cnt, axis=1) - cnt
    dst, w_k = _slots(kmap, wc, off)
    as_ints = lambda a: a.astype(I32).reshape(-1)
    per_block = lambda a: jnp.swapaxes(a.reshape(TOP_K, MOE_BLOCKS, MOE_BLOCK), 0, 1).reshape(
        MOE_BLOCKS, 1, TOP_K * MOE_BLOCK)

    order = jnp.argsort(cnt[:, :, 0], axis=1)
    routed = _moe(as_ints(cnt), as_ints(off), as_ints(order), per_block(dst), per_block(w_k), h2s,
                  wg, wu, wd)
    return _fin(x1s, routed, mods[5], row(final_norm_g))
```

```python
import jax
import jax.numpy as jnp
from jax import lax
from jax.experimental import pallas as pl
from jax.experimental.pallas import tpu as pltpu

D_MODEL = 1024
BATCH = 8
SEQ = 4096
N_TOK = BATCH * SEQ
S5_WIDTH = 512
S5_GROUP = 16
S5_GROUPS = 32
S5_STATE = 64
N_STATE = S5_GROUPS * S5_STATE
CONV_WIDTH = 512
CONV_KERNEL = 31
N_EXPERTS = 64
TOP_K = 8
EXPERT_HIDDEN = 256
SHARED_HIDDEN = 256
ROUTED_SCALE = 2.5
N_MOD = 6
EPS = 1e-6

LANES = 128
SUBLANES = 8
VMEM_LIMIT = 56 * 1024 * 1024

TILE_STEPS = 64
TILE_ROWS = TILE_STEPS * BATCH
FIN_STEPS = 128
SLOT_ROWS = 1024
CONV_HALO = (CONV_KERNEL - 1) * BATCH
CONV_CHUNK = 64
SCAN_LANES = 512
MOE_BLOCK = 2048
MOE_BLOCKS = N_TOK // MOE_BLOCK
MOE_CHUNK = 512
MOE_GRAIN = 64
MOE_EXPERTS_PER_STEP = 2
MOE_EXPERT_STEPS = N_EXPERTS // MOE_EXPERTS_PER_STEP
MOE_UNROLL = 8
SLAB = 4
STAGE_ROWS = TOP_K * MOE_BLOCK + MOE_CHUNK

F32 = jnp.float32
BF16 = jnp.bfloat16
U32 = jnp.uint32
I32 = jnp.int32


_X_SPEC = pl.BlockSpec((BATCH, TILE_STEPS, D_MODEL), lambda i: (0, i, 0))


def _time_major(a):
    return jnp.swapaxes(a, 0, 1).reshape(a.shape[0] * a.shape[1], a.shape[2])


def _batch_major(a):
    return jnp.swapaxes(a.reshape(a.shape[0] // BATCH, BATCH, a.shape[1]), 0, 1)


def _per_batch(rows, mod, op):
    r3 = rows.reshape(rows.shape[0] // BATCH, BATCH, rows.shape[1])
    return op(r3, mod).reshape(rows.shape)


def _pack_rows(x):
    half = D_MODEL // 2
    return pltpu.pack_elementwise([x[:, :half], x[:, half:]], packed_dtype=BF16)


def _unpack_words(w):
    lo = pltpu.unpack_elementwise(w, index=0, packed_dtype=BF16, unpacked_dtype=F32)
    hi = pltpu.unpack_elementwise(w, index=1, packed_dtype=BF16, unpacked_dtype=F32)
    return lo, hi


def _store_slabs(ref, base, words):
    rows = words.shape[0]
    for s in range(SLAB):
        ref[pl.ds(base + s, rows, stride=SLAB), :] = words[:, s * LANES:(s + 1) * LANES]


def _load_slabs(ref, base, rows):
    return [ref[pl.ds(base + s, rows, stride=SLAB), :] for s in range(SLAB)]


def _sigmoid(x):
    return jax.nn.sigmoid(x)


def _const_spec(shape):
    zeros = (0,) * len(shape)
    return pl.BlockSpec(shape, lambda *_: zeros, pipeline_mode=pl.Buffered(1))


def _ada_kernel(c_ref, w_ref, b_ref, o_ref):
    c = c_ref[...]
    c_act = c * _sigmoid(c)
    o_ref[...] = jnp.dot(c_act, w_ref[...], preferred_element_type=F32,
                         precision=lax.Precision.HIGHEST) + b_ref[...]


def _ada(c, w_ada, b_ada):
    n_out = N_MOD * D_MODEL
    blk = 1536
    return pl.pallas_call(
        _ada_kernel,
        out_shape=jax.ShapeDtypeStruct((BATCH, n_out), F32),
        grid=(n_out // blk,),
        in_specs=[pl.BlockSpec((BATCH, D_MODEL), lambda j: (0, 0)),
                  pl.BlockSpec((D_MODEL, blk), lambda j: (0, j)),
                  pl.BlockSpec((1, blk), lambda j: (0, j))],
        out_specs=pl.BlockSpec((BATCH, blk), lambda j: (0, j)),
        compiler_params=pltpu.CompilerParams(vmem_limit_bytes=VMEM_LIMIT),
        name="ada",
    )(c, w_ada, b_ada.reshape(1, n_out))


def _s5prep_kernel(lr_ref, li_ref, ldt_ref, br_ref, bi_ref, ar_ref, ai_ref, bbr_ref, bbi_ref):
    lr = lr_ref[...]
    li = li_ref[...]
    dt = jnp.exp(ldt_ref[...])
    mag = jnp.exp(lr * dt)
    abar_r = mag * jnp.cos(li * dt)
    abar_i = mag * jnp.sin(li * dt)
    den = lr * lr + li * li
    nr = abar_r - 1.0
    ni = abar_i
    k_r = (nr * lr + ni * li) / den
    k_i = (ni * lr - nr * li) / den
    br = br_ref[...]
    bi = bi_ref[...]
    ar_ref[...] = abar_r
    ai_ref[...] = abar_i
    bbr_ref[...] = k_r * br - k_i * bi
    bbi_ref[...] = k_r * bi + k_i * br


def _s5prep(lam_re, lam_im, log_dt, b_re, b_im):
    rows = S5_GROUPS * S5_GROUP
    rep = lambda a: jnp.broadcast_to(a[:, None, :], (S5_GROUPS, S5_GROUP, S5_STATE)).reshape(rows, S5_STATE)
    lr = rep(lam_re)
    li = rep(lam_im)
    ldt = rep(jnp.broadcast_to(log_dt[:, None], (S5_GROUPS, S5_STATE)))
    br = jnp.transpose(b_re, (0, 2, 1)).reshape(rows, S5_STATE)
    bi = jnp.transpose(b_im, (0, 2, 1)).reshape(rows, S5_STATE)
    shp = jax.ShapeDtypeStruct((rows, S5_STATE), F32)
    return pl.pallas_call(_s5prep_kernel, out_shape=(shp, shp, shp, shp), name="s5prep")(lr, li, ldt, br, bi)


def _gelu_tanh(x):
    sqrt_2_over_pi = 0.7978845608028654
    cdf = 0.5 * (1.0 + jnp.tanh(sqrt_2_over_pi * (x + 0.044715 * (x * x * x))))
    return x * cdf


def _rms_mod(x, g, scale, shift):
    ms = jnp.mean(x * x, axis=-1, keepdims=True)
    h = x * lax.rsqrt(ms + EPS) * g
    return _per_batch(h, (scale, shift), lambda r, m: r * (1.0 + m[0]) + m[1])


def _tok_kernel(x_ref, sh1_ref, sc1_ref, gt1_ref, sh2_ref, sc2_ref, gt2_ref, g1_ref, g2_ref,
                wu_ref, wcv_ref, wcg_ref, wa_ref, wb_ref,
                bsp_ref, csr_ref, csi_ref, ar_ref, ai_ref, d_ref, wv_ref, wg_ref,
                dww_ref, dwb_ref, lng_ref, lnb_ref, pw_ref, wout_ref,
                rw_ref, rb_ref, sgu_ref, sd_ref, ew_gate_ref, ew_up_ref, ew_down_ref,
                x1s_ref, h2_ref, kmap_ref, wc_ref, cnt_ref, eb_gate_ref, eb_up_ref, eb_down_ref,
                bu_ref, sb_ref, sr_ref, si_ref, cbuf_ref, cv_ref):
    step = pl.program_id(0)

    @pl.when(step == 0)
    def _():
        sr_ref[...] = jnp.zeros_like(sr_ref)
        si_ref[...] = jnp.zeros_like(si_ref)
        cbuf_ref[0:CONV_HALO, :] = jnp.zeros((CONV_HALO, CONV_WIDTH), F32)

    x = _time_major(x_ref[...])
    h = _rms_mod(x, g1_ref[...], sc1_ref[...], sh1_ref[...]).astype(BF16)
    u = jnp.dot(h, wu_ref[...], preferred_element_type=F32)
    u_b = u.astype(BF16)
    zc = (jnp.dot(h, wcv_ref[...], preferred_element_type=F32)
          * _sigmoid(jnp.dot(h, wcg_ref[...], preferred_element_type=F32)))

    cbuf_ref[CONV_HALO:CONV_HALO + TILE_ROWS, :] = zc

    def conv_chunk(ci, carry):
        r0 = pl.multiple_of(ci * CONV_CHUNK, CONV_CHUNK)
        for lt in range(CONV_WIDTH // LANES):
            ls = slice(lt * LANES, (lt + 1) * LANES)
            win = cbuf_ref[pl.ds(r0, CONV_CHUNK + CONV_HALO), ls]
            acc = jnp.broadcast_to(dwb_ref[:, ls], (CONV_CHUNK, LANES))
            for k in range(CONV_KERNEL):
                acc = acc + dww_ref[k:k + 1, ls] * win[k * BATCH:k * BATCH + CONV_CHUNK]
            cv_ref[pl.ds(r0, CONV_CHUNK), ls] = acc
        return carry

    lax.fori_loop(0, TILE_ROWS // CONV_CHUNK, conv_chunk, 0)
    cbuf_ref[0:CONV_HALO, :] = cbuf_ref[TILE_ROWS:TILE_ROWS + CONV_HALO, :]

    for j in range(N_STATE // LANES):
        kb = (j * LANES // S5_STATE * S5_GROUP) // LANES
        res = jnp.dot(u_b[:, kb * LANES:(kb + 1) * LANES], bsp_ref[j], preferred_element_type=F32)
        bu_ref[:, j * LANES:(j + 1) * LANES] = res[:, :LANES]
        bu_ref[:, N_STATE + j * LANES:N_STATE + (j + 1) * LANES] = res[:, LANES:]

    gate_a = _sigmoid(jnp.dot(h, wa_ref[...], preferred_element_type=F32))
    gate_b = _sigmoid(jnp.dot(h, wb_ref[...], preferred_element_type=F32))

    for lg in range(N_STATE // SCAN_LANES):
        lo = lg * SCAN_LANES
        re = slice(lo, lo + SCAN_LANES)
        im = slice(N_STATE + lo, N_STATE + lo + SCAN_LANES)
        a_r = jnp.broadcast_to(ar_ref[:, re], (SUBLANES, SCAN_LANES))
        a_i = jnp.broadcast_to(ai_ref[:, re], (SUBLANES, SCAN_LANES))
        s_r = sr_ref[:, re]
        s_i = si_ref[:, re]
        for t in range(0, TILE_STEPS, 2):
            pair_r, pair_i = [], []
            for tt in (t, t + 1):
                rows = slice(tt * SUBLANES, (tt + 1) * SUBLANES)
                s_r, s_i = (a_r * s_r - a_i * s_i + bu_ref[rows, re],
                            a_r * s_i + a_i * s_r + bu_ref[rows, im])
                pair_r.append(s_r)
                pair_i.append(s_i)
            rows2 = slice(t * SUBLANES, (t + 2) * SUBLANES)
            sb_ref[rows2, re] = jnp.concatenate(pair_r, axis=0).astype(BF16)
            sb_ref[rows2, im] = jnp.concatenate(pair_i, axis=0).astype(BF16)
        sr_ref[:, re] = s_r
        si_ref[:, re] = s_i

    half_s = N_STATE // 2
    ys = []
    for hf in range(2):
        s_re = sb_ref[:, hf * half_s:(hf + 1) * half_s]
        s_im = sb_ref[:, N_STATE + hf * half_s:N_STATE + (hf + 1) * half_s]
        ys.append(jnp.dot(s_re, csr_ref[hf], preferred_element_type=F32)
                  - jnp.dot(s_im, csi_ref[hf], preferred_element_type=F32))
    y = jnp.concatenate(ys, axis=-1) + d_ref[...] * u
    z = _gelu_tanh(y).astype(BF16)
    y_a = (jnp.dot(z, wv_ref[...], preferred_element_type=F32)
           * _sigmoid(jnp.dot(z, wg_ref[...], preferred_element_type=F32)))

    cv = cv_ref[...]
    mu = jnp.mean(cv, axis=-1, keepdims=True)
    var = jnp.mean(jnp.square(cv - mu), axis=-1, keepdims=True)
    ln = (cv - mu) * lax.rsqrt(var + EPS) * lng_ref[...] + lnb_ref[...]
    zb = (ln * _sigmoid(ln)).astype(BF16)
    y_b = jnp.dot(zb, pw_ref[...], preferred_element_type=F32)

    m = gate_a * y_a + gate_b * y_b
    mixed = jnp.dot(m.astype(BF16), wout_ref[...], preferred_element_type=F32)

    x1 = x + _per_batch(mixed, gt1_ref[...], lambda r, m: r * m)
    hb = _rms_mod(x1, g2_ref[...], sc2_ref[...], sh2_ref[...]).astype(BF16)
    _store_slabs(h2_ref, 0, _pack_rows(hb.astype(F32)))

    scores = _sigmoid(lax.dot_general(rw_ref[...], hb, (((1,), (1,)), ((), ())),
                                      preferred_element_type=F32))
    expert = lax.broadcasted_iota(I32, scores.shape, 0)
    neg_inf = jnp.float32(-jnp.inf)
    work = scores + rb_ref[...]
    kmap = jnp.zeros(scores.shape, I32)
    for k in range(TOP_K):
        top = jnp.max(work, axis=0, keepdims=True)
        idx = jnp.min(jnp.where(work == top, expert, N_EXPERTS), axis=0, keepdims=True)
        hit = expert == idx
        kmap = jnp.where(hit, k + 1, kmap)
        work = jnp.where(hit, neg_inf, work)
    sel = kmap > 0
    w_sel = jnp.where(sel, scores, 0.0)
    kmap_ref[...] = kmap
    wc_ref[...] = w_sel / jnp.sum(w_sel, axis=0, keepdims=True) * ROUTED_SCALE

    @pl.when(step % (MOE_BLOCK // TILE_ROWS) == 0)
    def _():
        cnt_ref[...] = jnp.zeros_like(cnt_ref)

    cnt_ref[...] += jnp.sum(sel.astype(F32), axis=1, keepdims=True)

    gu = jnp.dot(hb, sgu_ref[...], preferred_element_type=F32)
    g = gu[:, :SHARED_HIDDEN]
    act = (g * _sigmoid(g)) * gu[:, SHARED_HIDDEN:]
    shared = jnp.dot(act.astype(BF16), sd_ref[...], preferred_element_type=F32)
    x1s_ref[...] = x1 + _per_batch(shared, gt2_ref[...], lambda r, m: r * m)

    eb_gate_ref[...] = ew_gate_ref[...].astype(BF16)
    eb_up_ref[...] = ew_up_ref[...].astype(BF16)
    eb_down_ref[...] = ew_down_ref[...].astype(BF16)


def _s5_tiles(bb_r, bb_i, c_re, c_im):
    n_tiles = N_STATE // LANES
    per_tile = LANES // S5_STATE
    per_k = LANES // S5_GROUP
    group = jnp.arange(n_tiles)[:, None] * per_tile + jnp.arange(per_tile)[None, :]
    place = (group[:, :, None] % per_k == jnp.arange(per_k)[None, None, :]).astype(F32)
    bb = jnp.stack([bb_r, bb_i]).reshape(2, n_tiles, per_tile, S5_GROUP, S5_STATE)
    bsp = jnp.einsum("jql,rjqhp->jlhrqp", place, bb).reshape(n_tiles, LANES, 2 * LANES)
    half = S5_GROUPS // 2
    eye = jnp.eye(half, dtype=F32)
    halves = lambda c: jnp.einsum("fghp,gk->fgpkh", c.reshape(2, half, S5_GROUP, S5_STATE), eye).reshape(
        2, half * S5_STATE, half * S5_GROUP)
    return bsp.astype(BF16), halves(c_re).astype(BF16), halves(c_im).astype(BF16)


def _tok(x, mods, consts, expert_weights):
    ts = TILE_ROWS
    steps = SEQ // TILE_STEPS
    tiles_per_block = MOE_BLOCK // ts
    row_spec = lambda rows, c: pl.BlockSpec((rows, c), lambda i: (i, 0))
    route_spec = pl.BlockSpec((N_EXPERTS, ts), lambda i: (0, i))
    per_step = N_EXPERTS // steps
    assert per_step * steps == N_EXPERTS
    cast_specs = [pl.BlockSpec((per_step,) + a.shape[1:], lambda i: (i, 0, 0)) for a in expert_weights]
    return pl.pallas_call(
        _tok_kernel,
        out_shape=(jax.ShapeDtypeStruct((N_TOK, D_MODEL), F32),
                   jax.ShapeDtypeStruct((N_TOK * SLAB, LANES), U32),
                   jax.ShapeDtypeStruct((N_EXPERTS, N_TOK), I32),
                   jax.ShapeDtypeStruct((N_EXPERTS, N_TOK), F32),
                   jax.ShapeDtypeStruct((MOE_BLOCKS, N_EXPERTS, 1), F32))
                  + tuple(jax.ShapeDtypeStruct(a.shape, BF16) for a in expert_weights),
        grid=(steps,),
        in_specs=[_X_SPEC] + [_const_spec(a.shape) for a in tuple(mods) + tuple(consts)] + cast_specs,
        out_specs=(row_spec(ts, D_MODEL), row_spec(ts * SLAB, LANES), route_spec, route_spec,
                   pl.BlockSpec((None, N_EXPERTS, 1), lambda i: (i // tiles_per_block, 0, 0)))
                  + tuple(cast_specs),
        scratch_shapes=[pltpu.VMEM((ts, 2 * N_STATE), F32),
                        pltpu.VMEM((ts, 2 * N_STATE), BF16),
                        pltpu.VMEM((SUBLANES, N_STATE), F32),
                        pltpu.VMEM((SUBLANES, N_STATE), F32),
                        pltpu.VMEM((CONV_HALO + ts, CONV_WIDTH), F32),
                        pltpu.VMEM((ts, CONV_WIDTH), F32)],
        compiler_params=pltpu.CompilerParams(
            dimension_semantics=("arbitrary",), vmem_limit_bytes=VMEM_LIMIT),
        name="tok",
    )(x, *mods, *consts, *expert_weights)


def _slots_kernel(kmap_ref, wc_ref, off_ref, tri_ref, dst_ref, wk_ref, seen_ref):
    @pl.when(pl.program_id(0) % (MOE_BLOCK // SLOT_ROWS) == 0)
    def _():
        seen_ref[...] = jnp.zeros_like(seen_ref)

    kmap = kmap_ref[...]
    wc = wc_ref[...]
    sel = (kmap > 0).astype(F32)
    seen = seen_ref[...]
    slot = jnp.dot(sel.astype(BF16), tri_ref[...], preferred_element_type=F32) + (seen + off_ref[...])
    seen_ref[...] = seen + jnp.sum(sel, axis=1, keepdims=True)
    dst, w_k = [], []
    for k in range(TOP_K):
        hit = kmap == k + 1
        dst.append(jnp.sum(jnp.where(hit, slot, 0.0), axis=0, keepdims=True))
        w_k.append(jnp.sum(jnp.where(hit, wc, 0.0), axis=0, keepdims=True))
    dst_ref[...] = jnp.concatenate(dst, axis=0).astype(I32) * SLAB
    wk_ref[...] = jnp.concatenate(w_k, axis=0)


def _slots(kmap, wc, off):
    ts = SLOT_ROWS
    tiles_per_block = MOE_BLOCK // ts
    route_spec = pl.BlockSpec((N_EXPERTS, ts), lambda i: (0, i))
    slot_spec = pl.BlockSpec((TOP_K, ts), lambda i: (0, i))
    tri = (lax.broadcasted_iota(I32, (ts, ts), 0) < lax.broadcasted_iota(I32, (ts, ts), 1)).astype(BF16)
    return pl.pallas_call(
        _slots_kernel,
        out_shape=(jax.ShapeDtypeStruct((TOP_K, N_TOK), I32), jax.ShapeDtypeStruct((TOP_K, N_TOK), F32)),
        grid=(N_TOK // ts,),
        in_specs=[route_spec, route_spec,
                  pl.BlockSpec((None, N_EXPERTS, 1), lambda i: (i // tiles_per_block, 0, 0)),
                  _const_spec(tri.shape)],
        out_specs=(slot_spec, slot_spec),
        scratch_shapes=[pltpu.VMEM((N_EXPERTS, 1), F32)],
        compiler_params=pltpu.CompilerParams(
            dimension_semantics=("arbitrary",), vmem_limit_bytes=VMEM_LIMIT),
        name="slots",
    )(kmap, wc, off, tri)


def _moe_kernel(cnt_ref, off_ref, order_ref, dst_ref, w_ref, h_ref, wg_hbm, wu_hbm, wd_hbm, o_ref,
                stage_ref, wg_buf, wu_buf, wd_buf, sem):
    blk = pl.program_id(0)

    def step_experts(step):
        a = order_ref[blk * N_EXPERTS + 2 * step]
        b = order_ref[blk * N_EXPERTS + 2 * step + 1]
        return jnp.minimum(a, b), jnp.maximum(a, b)

    def weight_copies(step, slot):
        pairs = ((wg_hbm, wg_buf), (wu_hbm, wu_buf), (wd_hbm, wd_buf))
        return [pltpu.make_async_copy(hbm.at[pl.ds(e, 1)], buf.at[slot, pl.ds(j, 1)], sem.at[i, slot, j])
                for i, (hbm, buf) in enumerate(pairs) for j, e in enumerate(step_experts(step))]

    def fetch(step, slot):
        for copy in weight_copies(step, slot):
            copy.start()

    def arrived(step, slot):
        for copy in weight_copies(step, slot):
            copy.wait()

    fetch(0, 0)

    @pl.when(blk == 0)
    def _():
        stage_ref[TOP_K * MOE_BLOCK * SLAB:, :] = jnp.zeros((MOE_CHUNK * SLAB, LANES), U32)

    def slab_at(ref, sublane_row):
        return ref.at[pl.ds(pl.multiple_of(sublane_row, SLAB), SLAB), :]

    def selections(ref, i):
        first = pl.multiple_of(i * MOE_UNROLL, MOE_UNROLL)
        return [ref.at[0, pl.ds(k * MOE_BLOCK + first, MOE_UNROLL)] for k in range(TOP_K)]

    def dispatch(i, carry):
        dst = selections(dst_ref, i)
        for j in range(MOE_UNROLL):
            slab = slab_at(h_ref, (i * MOE_UNROLL + j) * SLAB)[...]
            for k in range(TOP_K):
                slab_at(stage_ref, dst[k][j])[...] = slab
        return carry

    lax.fori_loop(0, MOE_BLOCK // MOE_UNROLL, dispatch, 0)

    def run_chunks(slot, firsts, live_rows, rows):
        words, outs = [], []
        for j, first in enumerate(firsts):
            words.append(_load_slabs(stage_ref, first * SLAB, rows))
        for j in range(len(firsts)):
            halves = [_unpack_words(w) for w in words[j]]
            xs = jnp.concatenate([lo for lo, _ in halves] + [hi for _, hi in halves], axis=-1)
            xs = xs.astype(BF16)
            g = jnp.dot(xs, wg_buf[slot, j], preferred_element_type=F32)
            act = (g * _sigmoid(g)) * jnp.dot(xs, wu_buf[slot, j], preferred_element_type=F32)
            outs.append(_pack_rows(jnp.dot(act.astype(BF16), wd_buf[slot, j], preferred_element_type=F32)))
        row = lax.broadcasted_iota(I32, (rows, LANES), 0)
        for j, first in enumerate(firsts):
            live = row < live_rows[j]
            for s in range(SLAB):
                merged = jnp.where(live, outs[j][:, s * LANES:(s + 1) * LANES], words[j][s])
                stage_ref[pl.ds(first * SLAB + s, rows, stride=SLAB), :] = merged

    def expert_step(step, carry):
        slot = step % 2

        @pl.when(step + 1 < MOE_EXPERT_STEPS)
        def _():
            fetch(step + 1, 1 - slot)

        arrived(step, slot)
        experts = step_experts(step)
        firsts = [off_ref[blk * N_EXPERTS + e] for e in experts]
        counts = [cnt_ref[blk * N_EXPERTS + e] for e in experts]
        rests = []
        for j in range(MOE_EXPERTS_PER_STEP):
            n_whole = jnp.maximum(counts[j] - 1, 0) // MOE_CHUNK

            def whole(c, carry, j=j):
                run_chunks_single(slot, j, firsts[j] + c * MOE_CHUNK)
                return carry

            lax.fori_loop(0, n_whole, whole, 0)
            firsts[j] = firsts[j] + n_whole * MOE_CHUNK
            rests.append(counts[j] - n_whole * MOE_CHUNK)
        size_class = (jnp.maximum(rests[0], rests[1]) + MOE_GRAIN - 1) // MOE_GRAIN
        for cls in range(1, MOE_CHUNK // MOE_GRAIN + 1):
            @pl.when(size_class == cls)
            def _(cls=cls):
                run_chunks(slot, firsts, rests, cls * MOE_GRAIN)
        return carry

    def run_chunks_single(slot, j, first):
        words = _load_slabs(stage_ref, first * SLAB, MOE_CHUNK)
        halves = [_unpack_words(w) for w in words]
        xs = jnp.concatenate([lo for lo, _ in halves] + [hi for _, hi in halves], axis=-1).astype(BF16)
        g = jnp.dot(xs, wg_buf[slot, j], preferred_element_type=F32)
        act = (g * _sigmoid(g)) * jnp.dot(xs, wu_buf[slot, j], preferred_element_type=F32)
        _store_slabs(stage_ref, first * SLAB,
                     _pack_rows(jnp.dot(act.astype(BF16), wd_buf[slot, j], preferred_element_type=F32)))

    lax.fori_loop(0, MOE_EXPERT_STEPS, expert_step, 0)

    def combine(i, carry):
        dst = selections(dst_ref, i)
        wts = selections(w_ref, i)
        for j in range(MOE_UNROLL):
            t = i * MOE_UNROLL + j
            terms = []
            for k in range(TOP_K):
                w = wts[k][j]
                lo, hi = _unpack_words(slab_at(stage_ref, dst[k][j])[...])
                terms.append((lo * w, hi * w))
            while len(terms) > 1:
                terms = [(a[0] + b[0], a[1] + b[1]) for a, b in zip(terms[0::2], terms[1::2])]
            slab_at(o_ref, t * SLAB)[...] = pltpu.pack_elementwise(list(terms[0]), packed_dtype=BF16)
        return carry

    lax.fori_loop(0, MOE_BLOCK // MOE_UNROLL, combine, 0)


def _moe(cnt, off, order, dst, w, h2s, wg, wu, wd):
    smem_spec = pl.BlockSpec((None, 1, TOP_K * MOE_BLOCK), lambda b, *_: (b, 0, 0), memory_space=pltpu.SMEM)
    hbm_spec = pl.BlockSpec(memory_space=pl.ANY)
    weight_buf = lambda rows, cols: pltpu.VMEM((2, MOE_EXPERTS_PER_STEP, rows, cols), BF16)
    return pl.pallas_call(
        _moe_kernel,
        out_shape=jax.ShapeDtypeStruct((N_TOK * SLAB, LANES), U32),
        grid_spec=pltpu.PrefetchScalarGridSpec(
            num_scalar_prefetch=3,
            grid=(MOE_BLOCKS,),
            in_specs=[smem_spec, smem_spec,
                      pl.BlockSpec((MOE_BLOCK * SLAB, LANES), lambda b, *_: (b, 0),
                                   pipeline_mode=pl.Buffered(1)),
                      hbm_spec, hbm_spec, hbm_spec],
            out_specs=pl.BlockSpec((MOE_BLOCK * SLAB, LANES), lambda b, *_: (b, 0)),
            scratch_shapes=[pltpu.VMEM((STAGE_ROWS * SLAB, LANES), U32),
                            weight_buf(D_MODEL, EXPERT_HIDDEN), weight_buf(D_MODEL, EXPERT_HIDDEN),
                            weight_buf(EXPERT_HIDDEN, D_MODEL),
                            pltpu.SemaphoreType.DMA((3, 2, MOE_EXPERTS_PER_STEP))]),
        compiler_params=pltpu.CompilerParams(
            dimension_semantics=("arbitrary",), vmem_limit_bytes=VMEM_LIMIT),
        name="moe",
    )(cnt, off, order, dst, w, h2s, wg, wu, wd)


def _fin_kernel(x1s_ref, r_ref, gt2_ref, fg_ref, o_ref):
    halves = [_unpack_words(w) for w in _load_slabs(r_ref, 0, x1s_ref.shape[0])]
    routed = jnp.concatenate([lo for lo, _ in halves] + [hi for _, hi in halves], axis=-1)
    x2 = x1s_ref[...] + _per_batch(routed, gt2_ref[...], lambda r, m: r * m)
    ms = jnp.mean(x2 * x2, axis=-1, keepdims=True)
    o_ref[...] = _batch_major(x2 * lax.rsqrt(ms + EPS) * fg_ref[...])


def _fin(x1s, routed, gt2, fg):
    ts = FIN_STEPS * BATCH
    return pl.pallas_call(
        _fin_kernel,
        out_shape=jax.ShapeDtypeStruct((BATCH, SEQ, D_MODEL), F32),
        grid=(SEQ // FIN_STEPS,),
        in_specs=[pl.BlockSpec((ts, D_MODEL), lambda i: (i, 0)),
                  pl.BlockSpec((ts * SLAB, LANES), lambda i: (i, 0)),
                  _const_spec((1, BATCH, D_MODEL)),
                  _const_spec((1, D_MODEL))],
        out_specs=pl.BlockSpec((BATCH, FIN_STEPS, D_MODEL), lambda i: (0, i, 0)),
        compiler_params=pltpu.CompilerParams(
            dimension_semantics=("parallel",), vmem_limit_bytes=VMEM_LIMIT),
        name="fin",
    )(x1s, routed, gt2, fg)


def kernel(x, c, w_ada, b_ada, norm1_g, w_in, s5_lambda_re, s5_lambda_im, s5_log_dt, s5_b_re, s5_b_im, s5_c_re, s5_c_im, s5_d, s5_glu_wv, s5_glu_wg, conv_dw_w, conv_dw_b, conv_ln_g, conv_ln_b, conv_pw_w, w_out, norm2_g, router_w, router_bias, exp_w_gate, exp_w_up, exp_w_down, shared_w_gate, shared_w_up, shared_w_down, final_norm_g):
    l = 0
    row = lambda a: a.reshape(1, -1)

    mod = _ada(c, w_ada[l], b_ada[l])
    mods = [m.reshape(1, BATCH, D_MODEL) for m in jnp.split(mod, N_MOD, axis=-1)]

    abar_r, abar_i, bbar_r, bbar_i = _s5prep(s5_lambda_re[l], s5_lambda_im[l], s5_log_dt[l],
                                             s5_b_re[l], s5_b_im[l])
    per_group = lambda a: a.reshape(S5_GROUPS, S5_GROUP, S5_STATE)
    a_r = per_group(abar_r)[:, 0, :].reshape(1, N_STATE)
    a_i = per_group(abar_i)[:, 0, :].reshape(1, N_STATE)

    o1 = S5_WIDTH
    o2 = o1 + CONV_WIDTH
    o3 = o2 + CONV_WIDTH
    o4 = o3 + D_MODEL
    w = w_in[l].astype(BF16)
    sgu = jnp.concatenate([shared_w_gate[l], shared_w_up[l]], axis=1).astype(BF16)
    consts = (row(norm1_g[l]), row(norm2_g[l]),
              w[:, :o1], w[:, o1:o2], w[:, o2:o3], w[:, o3:o4], w[:, o4:],
              *_s5_tiles(per_group(bbar_r), per_group(bbar_i), s5_c_re[l], s5_c_im[l]),
              a_r, a_i, row(s5_d[l]),
              s5_glu_wv[l].astype(BF16), s5_glu_wg[l].astype(BF16),
              conv_dw_w[l], row(conv_dw_b[l]), row(conv_ln_g[l]), row(conv_ln_b[l]),
              conv_pw_w[l].astype(BF16), w_out[l].astype(BF16),
              router_w[l].T.astype(BF16), router_bias[l].reshape(N_EXPERTS, 1),
              sgu, shared_w_down[l].astype(BF16))
    x1s, h2s, kmap, wc, cnt, wg, wu, wd = _tok(x, mods, consts,
                                               (exp_w_gate[l], exp_w_up[l], exp_w_down[l]))

    off = jnp.cumsum(cnt, axis=1) - cnt
    dst, w_k = _slots(kmap, wc, off)
    as_ints = lambda a: a.astype(I32).reshape(-1)
    per_block = lambda a: jnp.swapaxes(a.reshape(TOP_K, MOE_BLOCKS, MOE_BLOCK), 0, 1).reshape(
        MOE_BLOCKS, 1, TOP_K * MOE_BLOCK)

    order = jnp.argsort(cnt[:, :, 0], axis=1)
    routed = _moe(as_ints(cnt), as_ints(off), as_ints(order), per_block(dst), per_block(w_k), h2s,
                  wg, wu, wd)
    return _fin(x1s, routed, mods[5], row(final_norm_g))
```
